```python
import math
import jax, jax.numpy as jnp
from jax import lax
import numpy as np

D_MODEL = 1024
BATCH = 2
SEQ = 8192
DEPTH = 2
DEC_BATCH = 16
DEC_SEQ = 64
PAST_LEN = 4096

CHUNK = 64
N_META = 16
N_EVEN = (DEPTH + 1) // 2
N_ODD = DEPTH // 2
NORM_EPS = 1e-6

HA_HEADS = 4
HA_DK = 128
HA_DV = 128
HB_HEADS = 4
HB_DQK = 64
HB_DV = 128
GATE_CAP = 15.0
HC_QK_HEADS = 8
HC_V_HEADS = 16
HC_DK = 128
HC_DV = 128
CONV_W = 4
GDN_CONV_DIM = 2 * HC_QK_HEADS * HC_DK + HC_V_HEADS * HC_DV
FF_HIDDEN = -(-8 * D_MODEL // (3 * 256)) * 256

EVEN_SIZES = (HA_HEADS * HA_DK, HA_HEADS * HA_DK, HA_HEADS * HA_DV, HA_HEADS * HA_DV,
              HB_HEADS * HB_DQK, HB_HEADS * HB_DQK, HB_HEADS * HB_DV, HB_HEADS * HB_DV,
              HB_HEADS, HB_HEADS)
EVEN_COLS = sum(EVEN_SIZES)
EVEN_OUT = HA_HEADS * HA_DV + HB_HEADS * HB_DV
ODD_SIZES = (GDN_CONV_DIM, HC_V_HEADS * HC_DV, HC_V_HEADS, HC_V_HEADS)
ODD_COLS = sum(ODD_SIZES)

kernel_name = "hgrn2_mlstm_gdn_streaming_step"


def _rmsnorm(x, w):
    xf = x.astype(jnp.float32)
    y = xf * lax.rsqrt(jnp.mean(xf * xf, axis=-1, keepdims=True) + NORM_EPS)
    return (y * w.astype(jnp.float32)).astype(x.dtype)


def _l2norm(x):
    return x * lax.rsqrt(jnp.sum(x * x, axis=-1, keepdims=True) + NORM_EPS)


def _split(a, sizes):
    offs, acc = [], 0
    for s in sizes[:-1]:
        acc += s
        offs.append(acc)
    return jnp.split(a, offs, axis=-1)


def _causal_conv(x, buf, w):
    L = x.shape[1]
    xp = jnp.concatenate([buf, x], axis=1)
    y = sum(xp[:, j:j + L] * w[j] for j in range(CONV_W))
    return y, xp[:, xp.shape[1] - (CONV_W - 1):]


def _chunked_scan(step, state, seqs, block):
    n = seqs[0].shape[1] // block
    xs = tuple(jnp.swapaxes(a.reshape(a.shape[0], n, block, *a.shape[2:]), 0, 1) for a in seqs)
    state, ys = lax.scan(step, state, xs)
    ys = jnp.swapaxes(ys, 0, 1)
    return state, ys.reshape(ys.shape[0], n * block, *ys.shape[3:])


def _run_segments(step, state, seqs, segments):
    outs = []
    for start, stop, block in segments:
        state, y = _chunked_scan(step, state, tuple(a[:, start:stop] for a in seqs), block)
        outs.append(y)
    return state, jnp.concatenate(outs, axis=1)


def _hgrn_step(S, inp):
    q, k, log_f, v = inp
    c = q.shape[1]
    b = jnp.cumsum(log_f, axis=1)
    causal = jnp.tril(jnp.ones((c, c), bool))
    diff = b[:, :, None] - b[:, None, :]
    dec = jnp.exp(jnp.where(causal[None, :, :, None, None], diff, -jnp.inf))
    scores = jnp.einsum('bthk,btshk,bshk->bhts', q, dec, k)
    o = jnp.einsum('bhts,bshv->bthv', scores, v) + jnp.einsum('bthk,bhkv->bthv', q * jnp.exp(b), S)
    b_last = b[:, -1]
    S = jnp.exp(b_last)[..., None] * S + jnp.einsum('bshk,bshv->bhkv', k * jnp.exp(b_last[:, None] - b), v)
    return S, o


def _mlstm_step(state, inp):
    C, n, m = state
    q, k, v, ig, log_f = inp
    c = q.shape[1]
    F = jnp.cumsum(log_f, axis=1)
    causal = jnp.tril(jnp.ones((c, c), bool))
    logD = jnp.where(causal[None, :, :, None], F[:, :, None] - F[:, None, :] + ig[:, None, :], -jnp.inf)
    log_inter = F + m[:, None]
    m_t = jnp.maximum(log_inter, jnp.max(logD, axis=2))
    D = jnp.exp(logD - m_t[:, :, None])
    w_inter = jnp.exp(log_inter - m_t)
    qk = jnp.einsum('bthd,bshd->btsh', q, k) * D
    num = jnp.einsum('btsh,bshv->bthv', qk, v) + w_inter[..., None] * jnp.einsum('bthd,bhdv->bthv', q, C)
    den = jnp.sum(qk, axis=2) + w_inter * jnp.einsum('bthd,bhd->bth', q, n)
    h = num / jnp.maximum(jnp.abs(den), jnp.exp(-m_t))[..., None]
    F_last = F[:, -1]
    m_new = m_t[:, -1]
    w_s = jnp.exp(F_last[:, None] - F + ig - m_new[:, None])
    decay = jnp.exp(F_last + m - m_new)
    C = decay[..., None, None] * C + jnp.einsum('bsh,bshd,bshv->bhdv', w_s, k, v)
    n = decay[..., None] * n + jnp.einsum('bsh,bshd->bhd', w_s, k)
    return (C, n, m_new), h


def _gdn_step(S, inp):
    q, k, v, beta, g = inp
    c = q.shape[1]
    G = jnp.cumsum(g, axis=1)
    Gh = jnp.swapaxes(G, 1, 2)
    causal = jnp.tril(jnp.ones((c, c), bool))
    strict = jnp.tril(jnp.ones((c, c), bool), -1)
    decay = jnp.exp(jnp.where(causal, Gh[..., :, None] - Gh[..., None, :], -jnp.inf))
    kb = k * beta[..., None]
    lower = jnp.where(strict, jnp.einsum('bthd,bshd->bhts', kb, k) * decay, 0.0)
    a_mat = lower + jnp.eye(c, dtype=lower.dtype)
    rhs = jnp.concatenate([jnp.swapaxes(v * beta[..., None], 1, 2),
                           jnp.swapaxes(kb * jnp.exp(G)[..., None], 1, 2)], axis=-1)
    sol = lax.linalg.triangular_solve(a_mat, rhs, left_side=True, lower=True, unit_diagonal=True)
    u, w = sol[..., :HC_DV], sol[..., HC_DV:]
    v_new = u - jnp.einsum('bhtk,bhkv->bhtv', w, S)
    attn = jnp.einsum('bthd,bshd->bhts', q, k) * decay
    o = jnp.einsum('bthk,bhkv->bhtv', q * jnp.exp(G)[..., None], S) + jnp.einsum('bhts,bhsv->bhtv', attn, v_new)
    G_last = G[:, -1]
    S = jnp.exp(G_last)[..., None, None] * S + jnp.einsum('bshk,bsh,bhsv->bhkv', k, jnp.exp(G_last[:, None] - G), v_new)
    return S, jnp.swapaxes(o, 1, 2)


def _even_mixer(h, e, state, segments, w_in, w_out, lb_logits, hgrn_norm, b_i, b_f, mlstm_norm):
    B, L, _ = h.shape
    f32 = jnp.float32
    p = (h @ w_in).astype(f32)
    qa, fa, ia, ga, qb, kb, vb, ob, ib, fb = _split(p, EVEN_SIZES)
    lb = jnp.cumsum(jax.nn.softmax(lb_logits.astype(f32), axis=0), axis=0)[e].reshape(HA_HEADS, HA_DK)
    fa = fa.reshape(B, L, HA_HEADS, HA_DK)
    log_f = jnp.log(lb + (1.0 - lb) * jax.nn.sigmoid(fa))
    k_a = (1.0 - lb) * jax.nn.sigmoid(-fa)
    q_a = jax.nn.silu(qa.reshape(B, L, HA_HEADS, HA_DK))
    S_a, o_a = _run_segments(_hgrn_step, state[0].astype(f32),
                             (q_a, k_a, log_f, ia.reshape(B, L, HA_HEADS, HA_DV)), segments)
    o_a = _rmsnorm(o_a, hgrn_norm.reshape(HA_HEADS, HA_DV)) * jax.nn.silu(ga.reshape(B, L, HA_HEADS, HA_DV))
    ig = GATE_CAP * jnp.tanh((ib + b_i.astype(f32)) / GATE_CAP)
    fg = GATE_CAP * jnp.tanh((fb + b_f.astype(f32)) / GATE_CAP)
    mstate = tuple(s.astype(f32) for s in state[1:])
    mstate, h_b = _run_segments(_mlstm_step, mstate,
                                (qb.reshape(B, L, HB_HEADS, HB_DQK),
                                 kb.reshape(B, L, HB_HEADS, HB_DQK) * HB_DQK ** -0.5,
                                 vb.reshape(B, L, HB_HEADS, HB_DV), ig, jax.nn.log_sigmoid(fg)), segments)
    o_b = _rmsnorm(h_b, mlstm_norm.reshape(HB_HEADS, HB_DV)) * jax.nn.sigmoid(ob.reshape(B, L, HB_HEADS, HB_DV))
    merged = jnp.concatenate([o_a.reshape(B, L, -1), o_b.reshape(B, L, -1)], axis=-1).astype(h.dtype)
    return merged @ w_out, (S_a, mstate[0], mstate[1], mstate[2])


def _odd_mixer(h, S0, conv0, segments, w_in, conv_w, a_log, dt_bias, norm_w, w_out):
    B, L, _ = h.shape
    f32 = jnp.float32
    p = (h @ w_in).astype(f32)
    qkv, z, b, a = _split(p, ODD_SIZES)
    qkv, conv_new = _causal_conv(qkv, conv0.astype(f32), conv_w.astype(f32))
    q, k, v = _split(jax.nn.silu(qkv), (HC_QK_HEADS * HC_DK, HC_QK_HEADS * HC_DK, HC_V_HEADS * HC_DV))
    rep = HC_V_HEADS // HC_QK_HEADS
    q = jnp.repeat(_l2norm(q.reshape(B, L, HC_QK_HEADS, HC_DK)), rep, axis=2) * HC_DK ** -0.5
    k = jnp.repeat(_l2norm(k.reshape(B, L, HC_QK_HEADS, HC_DK)), rep, axis=2)
    v = v.reshape(B, L, HC_V_HEADS, HC_DV)
    beta = jax.nn.sigmoid(b)
    g = -jnp.exp(a_log.astype(f32)) * jax.nn.softplus(a + dt_bias.astype(f32))
    S, o = _run_segments(_gdn_step, S0.astype(f32), (q, k, v, beta, g), segments)
    o = _rmsnorm(o, norm_w) * jax.nn.silu(z.reshape(B, L, HC_V_HEADS, HC_DV))
    return o.reshape(B, L, -1).astype(h.dtype) @ w_out, S, conv_new


def _swiglu(h, w_in, w_out):
    gate, up = jnp.split(h @ w_in, 2, axis=-1)
    return (jax.nn.silu(gate) * up) @ w_out


def _trunk(x, states, segments, norm_mix, norm_ffn, norm_final, even_w_in, even_w_out, hgrn_lb_logits,
           hgrn_norm, mlstm_b_i, mlstm_b_f, mlstm_norm, odd_w_in, odd_conv_w, gdn_a_log, gdn_dt_bias,
           gdn_norm, odd_w_out, ffn_w_in, ffn_w_out):
    S_h, C_m, n_m, m_m, S_g, conv_g = states
    nh, nC, nn_, nm, ng, nconv = [], [], [], [], [], []
    for layer in range(DEPTH):
        h = _rmsnorm(x, norm_mix[layer])
        if layer % 2 == 0:
            e = layer // 2
            mix, st = _even_mixer(h, e, (S_h[e], C_m[e], n_m[e], m_m[e]), segments, even_w_in[e], even_w_out[e],
                                  hgrn_lb_logits, hgrn_norm[e], mlstm_b_i[e], mlstm_b_f[e], mlstm_norm[e])
            nh.append(st[0]); nC.append(st[1]); nn_.append(st[2]); nm.append(st[3])
        else:
            o = layer // 2
            mix, s_new, c_new = _odd_mixer(h, S_g[o], conv_g[o], segments, odd_w_in[o], odd_conv_w[o],
                                           gdn_a_log[o], gdn_dt_bias[o], gdn_norm[o], odd_w_out[o])
            ng.append(s_new); nconv.append(c_new)
        x = x + mix
        x = x + _swiglu(_rmsnorm(x, norm_ffn[layer]), ffn_w_in[layer], ffn_w_out[layer])
    y = _rmsnorm(x, norm_final)
    dt = x.dtype
    stk = lambda xs: jnp.stack(xs, axis=0).astype(dt)
    return y, (stk(nh), stk(nC), stk(nn_), stk(nm), stk(ng), stk(nconv))


def setup_inputs(seed: int = 0) -> dict:
    key = jax.random.key(seed)
    ks = iter(jax.random.split(key, 32))

    def nrm(shape, scale=1.0):
        return scale * jax.random.normal(next(ks), shape, jnp.float32)

    a_log = jnp.log(jax.random.uniform(next(ks), (N_ODD, HC_V_HEADS), jnp.float32, 1.0, 16.0))
    dt = jnp.exp(jax.random.uniform(next(ks), (N_ODD, HC_V_HEADS), jnp.float32, math.log(1e-3), math.log(1e-1)))
    dt_bias = dt + jnp.log(-jnp.expm1(-dt))
    return {
        "x_prompt": nrm((BATCH, SEQ, D_MODEL)),
        "x_sample": nrm((DEC_BATCH, DEC_SEQ, D_MODEL)),
        "state_hgrn_S": nrm((N_EVEN, DEC_BATCH, HA_HEADS, HA_DK, HA_DV), 0.3),
        "state_mlstm_C": nrm((N_EVEN, DEC_BATCH, HB_HEADS, HB_DQK, HB_DV), 0.3),
        "state_mlstm_n": nrm((N_EVEN, DEC_BATCH, HB_HEADS, HB_DQK), 0.3),
        "state_mlstm_m": nrm((N_EVEN, DEC_BATCH, HB_HEADS)),
        "state_gdn_S": nrm((N_ODD, DEC_BATCH, HC_V_HEADS, HC_DK, HC_DV), 0.1),
        "state_gdn_conv": nrm((N_ODD, DEC_BATCH, CONV_W - 1, GDN_CONV_DIM)),
        "meta_tokens": nrm((N_META, D_MODEL)),
        "norm_mix": 1.0 + nrm((DEPTH, D_MODEL), 0.02),
        "norm_ffn": 1.0 + nrm((DEPTH, D_MODEL), 0.02),
        "norm_final": 1.0 + nrm((D_MODEL,), 0.02),
        "even_w_in": nrm((N_EVEN, D_MODEL, EVEN_COLS), D_MODEL ** -0.5),
        "even_w_out": nrm((N_EVEN, EVEN_OUT, D_MODEL), EVEN_OUT ** -0.5),
        "hgrn_lb_logits": nrm((N_EVEN + 1, HA_HEADS * HA_DK), 0.5),
        "hgrn_norm": 1.0 + nrm((N_EVEN, HA_HEADS * HA_DV), 0.02),
        "mlstm_b_i": nrm((N_EVEN, HB_HEADS), 0.1),
        "mlstm_b_f": 3.0 + nrm((N_EVEN, HB_HEADS), 0.5),
        "mlstm_norm": 1.0 + nrm((N_EVEN, HB_HEADS * HB_DV), 0.02),
        "odd_w_in": nrm((N_ODD, D_MODEL, ODD_COLS), D_MODEL ** -0.5),
        "odd_conv_w": nrm((N_ODD, CONV_W, GDN_CONV_DIM), CONV_W ** -0.5),
        "gdn_a_log": a_log,
        "gdn_dt_bias": dt_bias,
        "gdn_norm": 1.0 + nrm((N_ODD, HC_DV), 0.02),
        "odd_w_out": nrm((N_ODD, HC_V_HEADS * HC_DV, D_MODEL), (HC_V_HEADS * HC_DV) ** -0.5),
        "ffn_w_in": nrm((DEPTH, D_MODEL, 2 * FF_HIDDEN), D_MODEL ** -0.5),
        "ffn_w_out": nrm((DEPTH, FF_HIDDEN, D_MODEL), FF_HIDDEN ** -0.5),
    }


def reference(x_prompt, x_sample, state_hgrn_S, state_mlstm_C, state_mlstm_n, state_mlstm_m, state_gdn_S,
              state_gdn_conv, meta_tokens, norm_mix, norm_ffn, norm_final, even_w_in, even_w_out, hgrn_lb_logits,
              hgrn_norm, mlstm_b_i, mlstm_b_f, mlstm_norm, odd_w_in, odd_conv_w, gdn_a_log, gdn_dt_bias, gdn_norm,
              odd_w_out, ffn_w_in, ffn_w_out):
    weights = (norm_mix, norm_ffn, norm_final, even_w_in, even_w_out, hgrn_lb_logits, hgrn_norm, mlstm_b_i,
               mlstm_b_f, mlstm_norm, odd_w_in, odd_conv_w, gdn_a_log, gdn_dt_bias, gdn_norm, odd_w_out,
               ffn_w_in, ffn_w_out)
    f32 = jnp.float32
    B, T, _ = x_prompt.shape
    meta = jnp.broadcast_to(meta_tokens.astype(x_prompt.dtype)[None], (B, N_META, D_MODEL))
    x0 = jnp.concatenate([meta, x_prompt], axis=1)
    zero_states = (
        jnp.zeros((N_EVEN, B, HA_HEADS, HA_DK, HA_DV), f32),
        jnp.zeros((N_EVEN, B, HB_HEADS, HB_DQK, HB_DV), f32),
        jnp.zeros((N_EVEN, B, HB_HEADS, HB_DQK), f32),
        jnp.zeros((N_EVEN, B, HB_HEADS), f32),
        jnp.zeros((N_ODD, B, HC_V_HEADS, HC_DK, HC_DV), f32),
        jnp.zeros((N_ODD, B, CONV_W - 1, GDN_CONV_DIM), x_prompt.dtype),
    )
    y0, p_states = _trunk(x0, zero_states, ((0, N_META, N_META), (N_META, N_META + T, CHUNK)), *weights)
    L = x_sample.shape[1]
    s_in = (state_hgrn_S, state_mlstm_C, state_mlstm_n, state_mlstm_m, state_gdn_S, state_gdn_conv)
    y1, s_states = _trunk(x_sample, s_in, ((0, L, L),), *weights)
    p_hgrn_S, p_mlstm_C, p_mlstm_n, p_mlstm_m, p_gdn_S, p_gdn_conv = p_states
    s_hgrn_S, s_mlstm_C, s_mlstm_n, s_mlstm_m, s_gdn_S, s_gdn_conv = s_states
    return (y0[:, N_META:], y1, p_hgrn_S, p_mlstm_C, p_mlstm_n, p_mlstm_m, p_gdn_S, p_gdn_conv,
            s_hgrn_S, s_mlstm_C, s_mlstm_n, s_mlstm_m, s_gdn_S, s_gdn_conv)
```

```python
import functools

import numpy as np
import jax
import jax.numpy as jnp
from jax import lax
from jax.experimental import pallas as pl
from jax.experimental.pallas import tpu as pltpu

F32 = jnp.float32
BF16 = jnp.bfloat16
HI = lax.Precision.HIGHEST

CHUNK = 64
N_META = 16
NORM_EPS = 1e-6
GATE_CAP = 15.0
NEG_BIG = -1e30

HA_HEADS, HA_DK, HA_DV = 4, 128, 128
HB_HEADS, HB_DQK, HB_DV = 4, 64, 128
HC_QK_HEADS, HC_V_HEADS, HC_DK, HC_DV = 8, 16, 128, 128
CONV_W = 4
LANE = 128
VMEM_LIMIT = 56 * 1024 * 1024


def _nn(a, b, prec=None):
    return lax.dot_general(a, b, (((1,), (0,)), ((), ())), precision=prec,
                           preferred_element_type=F32)


def _nt(a, b, prec=None):
    return lax.dot_general(a, b, (((1,), (1,)), ((), ())), precision=prec,
                           preferred_element_type=F32)


def _tn(a, b, prec=None):
    return lax.dot_general(a, b, (((0,), (0,)), ((), ())), precision=prec,
                           preferred_element_type=F32)


def _sigmoid(x):
    return 1.0 / (1.0 + jnp.exp(-x))


def _silu(x):
    return x * _sigmoid(x)


def _softplus(x):
    return jnp.maximum(x, 0.0) + jnp.log1p(jnp.exp(-jnp.abs(x)))


def _log_sigmoid(x):
    return jnp.minimum(x, 0.0) - jnp.log1p(jnp.exp(-jnp.abs(x)))


def _rms(x, w):
    ms = jnp.mean(x * x, axis=-1, keepdims=True)
    return x * lax.rsqrt(ms + NORM_EPS) * w


def _proj_kernel(x_ref, nw_ref, w_ref, o_ref, *, n_chunk):
    h = _rms(x_ref[...], nw_ref[...]).astype(BF16)
    n_total = w_ref.shape[1]
    for n0 in range(0, n_total, n_chunk):
        n1 = min(n0 + n_chunk, n_total)
        o_ref[:, n0:n1] = _nn(h, w_ref[:, n0:n1])


def _post_kernel(x_ref, m_ref, wo_ref, nw_ref, wi_ref, w2_ref, nf_ref, o_ref, act_ref,
                 *, hidden, h_chunk, final_norm):
    x1 = x_ref[...] + _nn(m_ref[...], wo_ref[...])
    h = _rms(x1, nw_ref[...]).astype(BF16)
    for c0 in range(0, hidden, h_chunk):
        c1 = min(c0 + h_chunk, hidden)
        gate = _nn(h, wi_ref[:, c0:c1])
        up = _nn(h, wi_ref[:, hidden + c0:hidden + c1])
        act_ref[:, c0:c1] = (_silu(gate) * up).astype(BF16)
    x2 = x1 + _nn(act_ref[...], w2_ref[...])
    if final_norm:
        x2 = _rms(x2, nf_ref[...])
    o_ref[...] = x2


def _pick_tm(m_rows, cap=512):
    best = CHUNK
    k = 1
    while CHUNK * k <= cap:
        if m_rows % (CHUNK * k) == 0:
            best = CHUNK * k
        k += 1
    return best


def _const_spec(shape):
    nd = len(shape)
    return pl.BlockSpec(shape, lambda i, _n=nd: (0,) * _n)


def _proj(x, nw, w, name):
    m_rows, d = x.shape
    n = w.shape[1]
    tm = _pick_tm(m_rows)
    return pl.pallas_call(
        functools.partial(_proj_kernel, n_chunk=512),
        grid=(m_rows // tm,),
        in_specs=[pl.BlockSpec((tm, d), lambda i: (i, 0)),
                  _const_spec((1, d)),
                  _const_spec((d, n))],
        out_specs=pl.BlockSpec((tm, n), lambda i: (i, 0)),
        out_shape=jax.ShapeDtypeStruct((m_rows, n), F32),
        compiler_params=pltpu.CompilerParams(dimension_semantics=("arbitrary",),
                                             vmem_limit_bytes=VMEM_LIMIT),
        name=name,
    )(x, nw, w)


def _post(x, mrg, wo, nw, wi, w2, nf, final_norm, name):
    m_rows, d = x.shape
    km = mrg.shape[1]
    hidden = w2.shape[0]
    tm = _pick_tm(m_rows)
    return pl.pallas_call(
        functools.partial(_post_kernel, hidden=hidden, h_chunk=256, final_norm=final_norm),
        grid=(m_rows // tm,),
        in_specs=[pl.BlockSpec((tm, d), lambda i: (i, 0)),
                  pl.BlockSpec((tm, km), lambda i: (i, 0)),
                  _const_spec((km, d)),
                  _const_spec((1, d)),
                  _const_spec((d, 2 * hidden)),
                  _const_spec((hidden, d)),
                  _const_spec((1, d))],
        out_specs=pl.BlockSpec((tm, d), lambda i: (i, 0)),
        out_shape=jax.ShapeDtypeStruct((m_rows, d), F32),
        scratch_shapes=[pltpu.VMEM((tm, hidden), BF16)],
        compiler_params=pltpu.CompilerParams(dimension_semantics=("arbitrary",),
                                             vmem_limit_bytes=VMEM_LIMIT),
        name=name,
    )(x, mrg, wo, nw, wi, w2, nf)


def _job_tables(n_prompt, chunks_prompt, n_sample, chunks_sample):
    n_slots = n_prompt + n_sample + 1
    slot, fs, insl = [n_slots - 1], [1], [0]
    for b in range(n_prompt):
        for c in range(chunks_prompt):
            slot.append(b); fs.append(2 if c == 0 else 0); insl.append(0)
    for s in range(n_sample):
        for c in range(chunks_sample):
            slot.append(n_prompt + s); fs.append(3 if c == 0 else 0); insl.append(s)
    to = lambda v: jnp.asarray(np.asarray(v, np.int32))
    return to(slot), to(fs), to(insl), n_slots


def _hgrn_tables(c):
    t = np.arange(c)[:, None]
    j = np.arange(c)[None, :]
    mats, masks = [], []
    m = c // 2
    while m >= 1:
        blk = t // (2 * m)
        bd = blk * 2 * m + m - 1
        second = (t % (2 * m)) >= m
        mat = np.where(second, (j > bd) & (j <= t), (j > t) & (j <= bd)).astype(np.float32)
        mats.append(mat)
        masks.append((blk == (j // (2 * m))).astype(np.float32))
        m //= 2
    mats.append((j <= t).astype(np.float32))
    masks.append((j == t).astype(np.float32))
    return (jnp.asarray(np.concatenate(mats, 0)), jnp.asarray(np.stack(masks, 0)),
            len(mats) - 1)


def _even_kernel(slot_ref, fs_ref, insl_ref,
                 pa_ref, qb_ref, kb_ref, vb_ref, ob_ref, gt_ref,
                 lbl_ref, hn_ref, mnw_ref, gbias_ref,
                 lvl_ref, lmask_ref, tri_ref, eye_ref,
                 hs_in, mc_in, mn_in, mm_in,
                 mrg_ref, hs_out, mc_out, mn_out, mm_out,
                 meta_hs, meta_mc, meta_mn, meta_mm,
                 *, lb_index, n_levels):
    c = CHUNK
    j = pl.program_id(0)
    fs = fs_ref[j]

    @pl.when(fs == 1)
    def _():
        hs_out[...] = jnp.zeros_like(hs_out)
        mc_out[...] = jnp.zeros_like(mc_out)
        mn_out[...] = jnp.zeros_like(mn_out)
        mm_out[...] = jnp.zeros_like(mm_out)

    @pl.when(fs == 2)
    def _():
        hs_out[...] = meta_hs[...]
        mc_out[...] = meta_mc[...]
        mn_out[...] = meta_mn[...]
        mm_out[...] = meta_mm[...]

    @pl.when(fs == 3)
    def _():
        hs_out[...] = hs_in[...]
        mc_out[...] = mc_in[...]
        mn_out[...] = mn_in[...]
        mm_out[...] = mm_in[...]

    hk = HA_HEADS * HA_DK
    lbl = lbl_ref[...]
    ex = jnp.exp(lbl - jnp.max(lbl, axis=0, keepdims=True))
    sm = ex / jnp.sum(ex, axis=0, keepdims=True)
    lb = jnp.sum(sm[:lb_index + 1], axis=0, keepdims=True)

    fa = pa_ref[:, hk:2 * hk]
    log_f = jnp.log(lb + (1.0 - lb) * _sigmoid(fa))
    k_all = (1.0 - lb) * _sigmoid(-fa)
    q_all = _silu(pa_ref[:, 0:hk])
    e_all = _nn(lvl_ref[...], log_f, HI)
    row = lax.broadcasted_iota(jnp.int32, (c, HA_DK), 0)

    for h in range(HA_HEADS):
        sl = slice(h * HA_DK, (h + 1) * HA_DK)
        q_h, k_h = q_all[:, sl], k_all[:, sl]
        v_h = pa_ref[:, 2 * hk + h * HA_DV:2 * hk + (h + 1) * HA_DV]
        scores = lmask_ref[n_levels] * _nt(q_h.astype(BF16), k_h.astype(BF16))
        m = c // 2
        for lvl in range(n_levels):
            dec = jnp.exp(e_all[lvl * c:(lvl + 1) * c, sl])
            second = (row & m) != 0
            qs = jnp.where(second, q_h * dec, 0.0).astype(BF16)
            ks = jnp.where(second, 0.0, k_h * dec).astype(BF16)
            part = _nt(qs, ks)
            scores = scores + (part if lvl == 0 else lmask_ref[lvl] * part)
            m //= 2
        b_h = e_all[n_levels * c:(n_levels + 1) * c, sl]
        st = hs_out[0, h]
        o = _nn(scores.astype(BF16), v_h.astype(BF16)) \
            + _nt((q_h * jnp.exp(b_h)).astype(BF16), st.astype(BF16))
        b_last = b_h[c - 1:c, :]
        k_til = k_h * jnp.exp(b_last - b_h)
        hs_out[0, h] = st * jnp.exp(b_last) + _tn(v_h.astype(BF16), k_til.astype(BF16))
        ga = pa_ref[:, 3 * hk + h * HA_DV:3 * hk + (h + 1) * HA_DV]
        mrg_ref[:, h * HA_DV:(h + 1) * HA_DV] = (_rms(o, hn_ref[:, sl]) * _silu(ga)).astype(BF16)

    lane = lax.broadcasted_iota(jnp.int32, (c, LANE), 1)
    rowg = lax.broadcasted_iota(jnp.int32, (c, LANE), 0)
    gcap = GATE_CAP * jnp.tanh((gt_ref[...] + gbias_ref[...]) / GATE_CAP)
    is_f = (lane >= HB_HEADS) & (lane < 2 * HB_HEADS)
    pad = rowg < jnp.where(j == 0, c - N_META, 0)
    lf_t = jnp.where(is_f & jnp.logical_not(pad), _log_sigmoid(gcap), 0.0)
    ig_t = jnp.where(pad, NEG_BIG, gcap)
    fc = _nn(tri_ref[...], lf_t, HI)
    fr = _nt(eye_ref[...], fc, HI)
    ir = _nt(eye_ref[...], ig_t, HI)
    r_i = lax.broadcasted_iota(jnp.int32, (c, c), 0)
    c_i = lax.broadcasted_iota(jnp.int32, (c, c), 1)
    causal = c_i <= r_i
    m_row = mm_out[0]
    m_row_new = m_row
    lane1 = lax.broadcasted_iota(jnp.int32, (1, LANE), 1)

    for h in range(HB_HEADS):
        fl = HB_HEADS + h
        q_h = qb_ref[:, h * HB_DQK:(h + 1) * HB_DQK]
        k_h = kb_ref[:, h * HB_DQK:(h + 1) * HB_DQK] * (HB_DQK ** -0.5)
        v_h = vb_ref[:, h * HB_DV:(h + 1) * HB_DV]
        fc_h = fc[:, fl:fl + 1]
        fr_h = fr[fl:fl + 1, :]
        igr_h = ir[h:h + 1, :]
        igc_h = ig_t[:, h:h + 1]
        m_prev = m_row[:, fl:fl + 1]
        log_d = jnp.where(causal, fc_h - fr_h + igr_h, -jnp.inf)
        log_inter = fc_h + m_prev
        m_t = jnp.maximum(log_inter, jnp.max(log_d, axis=1, keepdims=True))
        d_mat = jnp.exp(log_d - m_t)
        w_inter = jnp.exp(log_inter - m_t)
        qb16 = q_h.astype(BF16)
        qk = _nt(qb16, k_h.astype(BF16)) * d_mat
        ct = mc_out[0, h]
        n_row = mn_out[0, h]
        num = _nn(qk.astype(BF16), v_h.astype(BF16)) + w_inter * _nt(qb16, ct.astype(BF16))
        den = jnp.sum(qk, axis=1, keepdims=True) \
            + w_inter * jnp.sum(q_h * n_row, axis=1, keepdims=True)
        hh = num / jnp.maximum(jnp.abs(den), jnp.exp(-m_t))
        f_last = fc_h[c - 1:c, :]
        m_new = m_t[c - 1:c, :]
        w_s = jnp.exp(f_last - fc_h + igc_h - m_new)
        decay = jnp.exp(f_last + m_prev - m_new)
        kw = k_h * w_s
        mc_out[0, h] = decay * ct + _tn(v_h.astype(BF16), kw.astype(BF16))
        mn_out[0, h] = decay * n_row + jnp.sum(kw, axis=0, keepdims=True)
        m_row_new = jnp.where(lane1 == fl, m_new, m_row_new)
        ob = ob_ref[:, h * HB_DV:(h + 1) * HB_DV]
        col0 = HA_HEADS * HA_DV + h * HB_DV
        mrg_ref[:, col0:col0 + HB_DV] = (
            _rms(hh, mnw_ref[:, h * HB_DV:(h + 1) * HB_DV]) * _sigmoid(ob)).astype(BF16)
    mm_out[0] = m_row_new

    @pl.when(j == 0)
    def _():
        meta_hs[...] = hs_out[...]
        meta_mc[...] = mc_out[...]
        meta_mn[...] = mn_out[...]
        meta_mm[...] = mm_out[...]


def _even_mixer(p, tables, lb_logits, lb_index, hgrn_norm, mlstm_norm, gbias, hs_in, mc_in, mn_in, mm_in):
    slot, fs, insl, n_slots = tables
    c = CHUNK
    n_jobs = p.shape[0] // c
    lvl, lmask, n_levels = _hgrn_tables(c)
    tri = jnp.asarray(np.tril(np.ones((c, c), np.float32)))
    eye = jnp.eye(LANE, dtype=F32)
    hk = HA_HEADS * HA_DK
    wa = 4 * hk
    qk_w = HB_HEADS * HB_DQK
    v_w = HB_HEADS * HB_DV
    g_col = wa + 2 * qk_w + 2 * v_w
    cst = lambda shape: pl.BlockSpec(shape, lambda j, *_: (0,) * len(shape))
    row_blk = lambda w, idx: pl.BlockSpec((c, w), lambda j, *_: (j, idx))
    st_in = lambda shape: pl.BlockSpec((1,) + shape, lambda j, s, f, i: (i[j],) + (0,) * len(shape))
    st_out = lambda shape: pl.BlockSpec((1,) + shape, lambda j, s, f, i: (s[j],) + (0,) * len(shape))
    sh_hs = (HA_HEADS, HA_DV, HA_DK)
    sh_mc = (HB_HEADS, HB_DV, HB_DQK)
    sh_mn = (HB_HEADS, 1, HB_DQK)
    sh_mm = (1, LANE)
    grid_spec = pltpu.PrefetchScalarGridSpec(
        num_scalar_prefetch=3,
        grid=(n_jobs,),
        in_specs=[row_blk(wa, 0),
                  row_blk(qk_w, wa // qk_w), row_blk(qk_w, wa // qk_w + 1),
                  row_blk(v_w, (wa + 2 * qk_w) // v_w), row_blk(v_w, (wa + 2 * qk_w) // v_w + 1),
                  row_blk(LANE, g_col // LANE),
                  cst(lb_logits.shape), cst((1, hk)), cst((1, v_w)), cst((1, LANE)),
                  cst(lvl.shape), cst(lmask.shape), cst((c, c)), cst((LANE, LANE)),
                  st_in(sh_hs), st_in(sh_mc), st_in(sh_mn), st_in(sh_mm)],
        out_specs=[pl.BlockSpec((c, hk + v_w), lambda j, *_: (j, 0)),
                   st_out(sh_hs), st_out(sh_mc), st_out(sh_mn), st_out(sh_mm)],
        scratch_shapes=[pltpu.VMEM((1,) + sh_hs, F32), pltpu.VMEM((1,) + sh_mc, F32),
                        pltpu.VMEM((1,) + sh_mn, F32), pltpu.VMEM((1,) + sh_mm, F32)],
    )
    out_shape = [jax.ShapeDtypeStruct((p.shape[0], hk + v_w), BF16),
                 jax.ShapeDtypeStruct((n_slots,) + sh_hs, F32),
                 jax.ShapeDtypeStruct((n_slots,) + sh_mc, F32),
                 jax.ShapeDtypeStruct((n_slots,) + sh_mn, F32),
                 jax.ShapeDtypeStruct((n_slots,) + sh_mm, F32)]
    return pl.pallas_call(
        functools.partial(_even_kernel, lb_index=lb_index, n_levels=n_levels),
        grid_spec=grid_spec,
        out_shape=out_shape,
        compiler_params=pltpu.CompilerParams(dimension_semantics=("arbitrary",),
                                             vmem_limit_bytes=VMEM_LIMIT),
        name="even_mixer",
    )(slot, fs, insl, p, p, p, p, p, p, lb_logits, hgrn_norm, mlstm_norm, gbias,
      lvl, lmask, tri, eye, hs_in, mc_in, mn_in, mm_in)


def _gdn_kernel(slot_ref, fs_ref, insl_ref,
                qkv_ref, z_ref, gt_ref, cw_ref, alog_ref, dtb_ref, gnw_ref,
                tri_ref, eye_ref, s_in, cv_in,
                mrg_ref, s_out, cv_out,
                xpad, meta_s, meta_cv, *, n_sq):
    c = CHUNK
    hp = pl.program_id(0)
    j = pl.program_id(1)
    fs = fs_ref[j]
    nk = CONV_W - 1
    base = 8 - nk

    @pl.when(fs == 1)
    def _():
        s_out[...] = jnp.zeros_like(s_out)
        xpad[base:8, :] = jnp.zeros((nk, xpad.shape[1]), F32)

    @pl.when(fs == 2)
    def _():
        s_out[...] = meta_s[...]
        xpad[base:8, :] = meta_cv[...]

    @pl.when(fs == 3)
    def _():
        s_out[...] = s_in[...]
        xpad[base:8, :] = cv_in[0]

    xpad[8:8 + c, :] = qkv_ref[...]
    y = cw_ref[0:1, :] * xpad[base:base + c, :]
    for t in range(1, CONV_W):
        y = y + cw_ref[t:t + 1, :] * xpad[base + t:base + t + c, :]
    carry = xpad[base + c:8 + c, :]
    cv_out[0] = carry
    xpad[base:8, :] = carry
    act = _silu(y)
    q = act[:, 0:HC_DK]
    k = act[:, HC_DK:2 * HC_DK]
    qn = q * lax.rsqrt(jnp.sum(q * q, axis=-1, keepdims=True) + NORM_EPS) * (HC_DK ** -0.5)
    kn = k * lax.rsqrt(jnp.sum(k * k, axis=-1, keepdims=True) + NORM_EPS)
    qn16, kn16 = qn.astype(BF16), kn.astype(BF16)
    qk = _nt(qn16, kn16)
    kk = _nt(kn16, kn16)

    gt = gt_ref[...]
    beta_t = _sigmoid(gt)
    g_t = -jnp.exp(alog_ref[...]) * _softplus(gt + dtb_ref[...])
    gc = _nn(tri_ref[...], g_t, HI)
    gr = _nt(eye_ref[...], gc, HI)
    lane = lax.broadcasted_iota(jnp.int32, (c, LANE), 1)
    sub = lax.broadcasted_iota(jnp.int32, (LANE, c), 0)
    r_i = lax.broadcasted_iota(jnp.int32, (c, c), 0)
    c_i = lax.broadcasted_iota(jnp.int32, (c, c), 1)
    causal = c_i <= r_i
    strict = c_i < r_i
    eye_c = (c_i == r_i).astype(F32)

    for jj in range(2):
        hv = 2 * hp + jj
        beta = jnp.sum(jnp.where(lane == hv, beta_t, 0.0), axis=1, keepdims=True)
        gc_h = jnp.sum(jnp.where(lane == HC_V_HEADS + hv, gc, 0.0), axis=1, keepdims=True)
        gr_h = jnp.sum(jnp.where(sub == HC_V_HEADS + hv, gr, 0.0), axis=0, keepdims=True)
        decay = jnp.exp(jnp.where(causal, gc_h - gr_h, -jnp.inf))
        low = jnp.where(strict, beta * kk * decay, 0.0)
        inv = eye_c - low
        pw = _nn(low, low, HI)
        for i in range(n_sq):
            inv = inv + _nn(inv, pw, HI)
            if i < n_sq - 1:
                pw = _nn(pw, pw, HI)
        v_h = act[:, (2 + jj) * HC_DK:(3 + jj) * HC_DK]
        eg = jnp.exp(gc_h)
        rhs = jnp.concatenate([v_h * beta, kn * (beta * eg)], axis=1)
        sol = _nn(inv, rhs, HI)
        u, w = sol[:, :HC_DV], sol[:, HC_DV:]
        s = s_out[0, jj]
        s16 = s.astype(BF16)
        v_new = u - _nn(w.astype(BF16), s16)
        attn = qk * decay
        o = _nn((qn * eg).astype(BF16), s16) + _nn(attn.astype(BF16), v_new.astype(BF16))
        g_last = gc_h[c - 1:c, :]
        k_til = kn * jnp.exp(g_last - gc_h)
        s_out[0, jj] = jnp.exp(g_last) * s + _tn(k_til.astype(BF16), v_new.astype(BF16))
        z = z_ref[:, jj * HC_DV:(jj + 1) * HC_DV]
        mrg_ref[:, jj * HC_DV:(jj + 1) * HC_DV] = (_rms(o, gnw_ref[...]) * _silu(z)).astype(BF16)

    @pl.when(j == 0)
    def _():
        meta_s[...] = s_out[...]
        meta_cv[...] = xpad[base:8, :]


def _gdn_mixer(p, tables, conv_w, alog_row, dtb_row, gnw, s_in, cv_in):
    slot, fs, insl, n_slots = tables
    c = CHUNK
    n_jobs = p.shape[0] // c
    n_hp = HC_QK_HEADS
    wq = 4 * HC_DK
    wz = 2 * HC_DV
    qkv_w = n_hp * wq
    z_w = HC_V_HEADS * HC_DV
    n_sq = int(np.log2(c)) - 1
    tri = jnp.asarray(np.tril(np.ones((c, c), np.float32)))
    eye = jnp.eye(LANE, dtype=F32)
    nk = CONV_W - 1
    cst = lambda shape: pl.BlockSpec(shape, lambda h, j, *_: (0,) * len(shape))
    grid_spec = pltpu.PrefetchScalarGridSpec(
        num_scalar_prefetch=3,
        grid=(n_hp, n_jobs),
        in_specs=[pl.BlockSpec((c, wq), lambda h, j, *_: (j, h)),
                  pl.BlockSpec((c, wz), lambda h, j, *_: (j, qkv_w // wz + h)),
                  pl.BlockSpec((c, LANE), lambda h, j, *_: (j, (qkv_w + z_w) // LANE)),
                  pl.BlockSpec((CONV_W, wq), lambda h, j, *_: (0, h)),
                  cst((1, LANE)), cst((1, LANE)), cst((1, HC_DV)),
                  cst((c, c)), cst((LANE, LANE)),
                  pl.BlockSpec((1, 2, HC_DK, HC_DV), lambda h, j, s, f, i: (i[j], h, 0, 0)),
                  pl.BlockSpec((1, nk, wq), lambda h, j, s, f, i: (i[j], 0, h))],
        out_specs=[pl.BlockSpec((c, wz), lambda h, j, *_: (j, h)),
                   pl.BlockSpec((1, 2, HC_DK, HC_DV), lambda h, j, s, f, i: (s[j], h, 0, 0)),
                   pl.BlockSpec((1, nk, wq), lambda h, j, s, f, i: (s[j], 0, h))],
        scratch_shapes=[pltpu.VMEM((8 + c, wq), F32),
                        pltpu.VMEM((1, 2, HC_DK, HC_DV), F32),
                        pltpu.VMEM((nk, wq), F32)],
    )
    out_shape = [jax.ShapeDtypeStruct((p.shape[0], z_w), BF16),
                 jax.ShapeDtypeStruct((n_slots, HC_V_HEADS, HC_DK, HC_DV), F32),
                 jax.ShapeDtypeStruct((n_slots, nk, qkv_w), F32)]
    return pl.pallas_call(
        functools.partial(_gdn_kernel, n_sq=n_sq),
        grid_spec=grid_spec,
        out_shape=out_shape,
        compiler_params=pltpu.CompilerParams(dimension_semantics=("arbitrary", "arbitrary"),
                                             vmem_limit_bytes=VMEM_LIMIT),
        name="gdn_mixer",
    )(slot, fs, insl, p, p, p, conv_w, alog_row, dtb_row, gnw, tri, eye, s_in, cv_in)


def _gdn_perm():
    nq = HC_QK_HEADS * HC_DK
    idx = []
    for hp in range(HC_QK_HEADS):
        idx.append(np.arange(hp * HC_DK, (hp + 1) * HC_DK))
        idx.append(nq + np.arange(hp * HC_DK, (hp + 1) * HC_DK))
        idx.append(2 * nq + np.arange(2 * hp * HC_DV, (2 * hp + 2) * HC_DV))
    return np.concatenate(idx)


def _pad_cols(w, width):
    return jnp.pad(w, ((0, 0), (0, width - w.shape[1])))


def kernel(x_prompt, x_sample, state_hgrn_S, state_mlstm_C, state_mlstm_n, state_mlstm_m, state_gdn_S,
           state_gdn_conv, meta_tokens, norm_mix, norm_ffn, norm_final, even_w_in, even_w_out,
           hgrn_lb_logits, hgrn_norm, mlstm_b_i, mlstm_b_f, mlstm_norm, odd_w_in, odd_conv_w, gdn_a_log,
           gdn_dt_bias, gdn_norm, odd_w_out, ffn_w_in, ffn_w_out):
    n_b, t_len, d = x_prompt.shape
    n_s, l_s, _ = x_sample.shape
    c = CHUNK
    assert t_len % c == 0 and l_s % c == 0 and meta_tokens.shape[0] == N_META
    assert norm_mix.shape[0] == 2, "one even (HGRN2+mLSTM) and one odd (DeltaNet) layer"
    dt = x_prompt.dtype

    x = jnp.concatenate([jnp.zeros((c - N_META, d), dt), meta_tokens.astype(dt),
                         x_prompt.reshape(n_b * t_len, d), x_sample.reshape(n_s * l_s, d)], axis=0)
    tables = _job_tables(n_b, t_len // c, n_s, l_s // c)
    n_slots = tables[3]
    row = lambda v: v.reshape(1, -1).astype(F32)

    e = 0
    n_main = 4 * HA_HEADS * HA_DK + 2 * HB_HEADS * HB_DQK + 2 * HB_HEADS * HB_DV
    w_even = jnp.concatenate([even_w_in[e][:, :n_main], _pad_cols(even_w_in[e][:, n_main:], LANE)],
                             axis=1).astype(BF16)
    p_even = _proj(x, row(norm_mix[0]), w_even, "proj_even")
    gbias = _pad_cols(jnp.concatenate([mlstm_b_i[e], mlstm_b_f[e]]).reshape(1, -1).astype(F32), LANE)
    hs_in = jnp.swapaxes(state_hgrn_S[e].astype(F32), -1, -2)
    mc_in = jnp.swapaxes(state_mlstm_C[e].astype(F32), -1, -2)
    mn_in = state_mlstm_n[e].astype(F32)[:, :, None, :]
    mm_in = jnp.pad(state_mlstm_m[e].astype(F32), ((0, 0), (HB_HEADS, LANE - 2 * HB_HEADS)))[:, None, :]
    mrg0, hs_o, mc_o, mn_o, mm_o = _even_mixer(
        p_even, tables, hgrn_lb_logits.astype(F32), e, row(hgrn_norm[e]), row(mlstm_norm[e]), gbias,
        hs_in, mc_in, mn_in, mm_in)
    x = _post(x, mrg0, even_w_out[e].astype(BF16), row(norm_ffn[0]), ffn_w_in[0].astype(BF16),
              ffn_w_out[0].astype(BF16), row(norm_final), False, "post_even")

    o = 0
    perm = _gdn_perm()
    n_qkv = perm.shape[0]
    n_z = HC_V_HEADS * HC_DV
    w_odd = jnp.concatenate([odd_w_in[o][:, perm], odd_w_in[o][:, n_qkv:n_qkv + n_z],
                             _pad_cols(odd_w_in[o][:, n_qkv + n_z:], LANE)], axis=1).astype(BF16)
    p_odd = _proj(x, row(norm_mix[1]), w_odd, "proj_odd")
    lane_pad = lambda v: jnp.pad(v.reshape(1, -1).astype(F32),
                                 ((0, 0), (HC_V_HEADS, LANE - 2 * HC_V_HEADS)))
    mrg1, gs_o, cv_o = _gdn_mixer(
        p_odd, tables, odd_conv_w[o].astype(F32)[:, perm], lane_pad(gdn_a_log[o]), lane_pad(gdn_dt_bias[o]),
        row(gdn_norm[o]), state_gdn_S[o].astype(F32), state_gdn_conv[o].astype(F32)[:, :, perm])
    y = _post(x, mrg1, odd_w_out[o].astype(BF16), row(norm_ffn[1]), ffn_w_in[1].astype(BF16),
              ffn_w_out[1].astype(BF16), row(norm_final), True, "post_odd")

    y_prompt = y[c:c + n_b * t_len].reshape(n_b, t_len, d)
    y_sample = y[c + n_b * t_len:].reshape(n_s, l_s, d)
    inv_perm = np.argsort(perm)
    hs_o = jnp.swapaxes(hs_o, -1, -2)
    mc_o = jnp.swapaxes(mc_o, -1, -2)
    mn_o = mn_o[:, :, 0, :]
    mm_o = mm_o[:, 0, HB_HEADS:2 * HB_HEADS]
    cv_o = cv_o[:, :, inv_perm]
    stk = lambda a, lo, hi: a[lo:hi][None].astype(dt)
    p_sl, s_sl = (0, n_b), (n_b, n_b + n_s)
    outs = [y_prompt, y_sample]
    for lo, hi in (p_sl, s_sl):
        outs += [stk(hs_o, lo, hi), stk(mc_o, lo, hi), stk(mn_o, lo, hi), stk(mm_o, lo, hi),
                 stk(gs_o, lo, hi), stk(cv_o, lo, hi)]
    return tuple(outs)
```

```python
import functools

import numpy as np
import jax
import jax.numpy as jnp
from jax import lax
from jax.experimental import pallas as pl
from jax.experimental.pallas import tpu as pltpu

F32 = jnp.float32
BF16 = jnp.bfloat16
HI = lax.Precision.HIGHEST

CHUNK = 64
N_META = 16
NORM_EPS = 1e-6
GATE_CAP = 15.0
NEG_BIG = -1e30

HA_HEADS, HA_DK, HA_DV = 4, 128, 128
HB_HEADS, HB_DQK, HB_DV = 4, 64, 128
HC_QK_HEADS, HC_V_HEADS, HC_DK, HC_DV = 8, 16, 128, 128
CONV_W = 4
LANE = 128
VMEM_LIMIT = 56 * 1024 * 1024


def _nn(a, b, prec=None):
    return lax.dot_general(a, b, (((1,), (0,)), ((), ())), precision=prec,
                           preferred_element_type=F32)


def _nt(a, b, prec=None):
    return lax.dot_general(a, b, (((1,), (1,)), ((), ())), precision=prec,
                           preferred_element_type=F32)


def _tn(a, b, prec=None):
    return lax.dot_general(a, b, (((0,), (0,)), ((), ())), precision=prec,
                           preferred_element_type=F32)


def _sigmoid(x):
    return 1.0 / (1.0 + jnp.exp(-x))


def _silu(x):
    return x * _sigmoid(x)


def _softplus(x):
    return jnp.maximum(x, 0.0) + jnp.log1p(jnp.exp(-jnp.abs(x)))


def _log_sigmoid(x):
    return jnp.minimum(x, 0.0) - jnp.log1p(jnp.exp(-jnp.abs(x)))


def _rms(x, w):
    ms = jnp.mean(x * x, axis=-1, keepdims=True)
    return x * lax.rsqrt(ms + NORM_EPS) * w


def _proj_kernel(x_ref, nw_ref, w_ref, o_ref, *, n_chunk):
    h = _rms(x_ref[...], nw_ref[...]).astype(BF16)
    n_total = w_ref.shape[1]
    for n0 in range(0, n_total, n_chunk):
        n1 = min(n0 + n_chunk, n_total)
        o_ref[:, n0:n1] = _nn(h, w_ref[:, n0:n1])


def _post_kernel(x_ref, m_ref, wo_ref, nw_ref, wi_ref, w2_ref, nf_ref, o_ref, act_ref,
                 *, hidden, h_chunk, final_norm):
    x1 = x_ref[...] + _nn(m_ref[...], wo_ref[...])
    h = _rms(x1, nw_ref[...]).astype(BF16)
    for c0 in range(0, hidden, h_chunk):
        c1 = min(c0 + h_chunk, hidden)
        gate = _nn(h, wi_ref[:, c0:c1])
        up = _nn(h, wi_ref[:, hidden + c0:hidden + c1])
        act_ref[:, c0:c1] = (_silu(gate) * up).astype(BF16)
    x2 = x1 + _nn(act_ref[...], w2_ref[...])
    if final_norm:
        x2 = _rms(x2, nf_ref[...])
    o_ref[...] = x2


def _pick_tm(m_rows, cap=512):
    best = CHUNK
    k = 1
    while CHUNK * k <= cap:
        if m_rows % (CHUNK * k) == 0:
            best = CHUNK * k
        k += 1
    return best


def _const_spec(shape):
    nd = len(shape)
    return pl.BlockSpec(shape, lambda i, _n=nd: (0,) * _n)


def _proj(x, nw, w, name):
    m_rows, d = x.shape
    n = w.shape[1]
    tm = _pick_tm(m_rows)
    return pl.pallas_call(
        functools.partial(_proj_kernel, n_chunk=512),
        grid=(m_rows // tm,),
        in_specs=[pl.BlockSpec((tm, d), lambda i: (i, 0)),
                  _const_spec((1, d)),
                  _const_spec((d, n))],
        out_specs=pl.BlockSpec((tm, n), lambda i: (i, 0)),
        out_shape=jax.ShapeDtypeStruct((m_rows, n), F32),
        compiler_params=pltpu.CompilerParams(dimension_semantics=("arbitrary",),
                                             vmem_limit_bytes=VMEM_LIMIT),
        name=name,
    )(x, nw, w)


def _post(x, mrg, wo, nw, wi, w2, nf, final_norm, name):
    m_rows, d = x.shape
    km = mrg.shape[1]
    hidden = w2.shape[0]
    tm = _pick_tm(m_rows)
    return pl.pallas_call(
        functools.partial(_post_kernel, hidden=hidden, h_chunk=256, final_norm=final_norm),
        grid=(m_rows // tm,),
        in_specs=[pl.BlockSpec((tm, d), lambda i: (i, 0)),
                  pl.BlockSpec((tm, km), lambda i: (i, 0)),
                  _const_spec((km, d)),
                  _const_spec((1, d)),
                  _const_spec((d, 2 * hidden)),
                  _const_spec((hidden, d)),
                  _const_spec((1, d))],
        out_specs=pl.BlockSpec((tm, d), lambda i: (i, 0)),
        out_shape=jax.ShapeDtypeStruct((m_rows, d), F32),
        scratch_shapes=[pltpu.VMEM((tm, hidden), BF16)],
        compiler_params=pltpu.CompilerParams(dimension_semantics=("arbitrary",),
                                             vmem_limit_bytes=VMEM_LIMIT),
        name=name,
    )(x, mrg, wo, nw, wi, w2, nf)


def _job_tables(n_prompt, chunks_prompt, n_sample, chunks_sample):
    n_slots = n_prompt + n_sample + 1
    slot, fs, insl = [n_slots - 1], [1], [0]
    for b in range(n_prompt):
        for c in range(chunks_prompt):
            slot.append(b); fs.append(2 if c == 0 else 0); insl.append(0)
    for s in range(n_sample):
        for c in range(chunks_sample):
            slot.append(n_prompt + s); fs.append(3 if c == 0 else 0); insl.append(s)
    to = lambda v: jnp.asarray(np.asarray(v, np.int32))
    return to(slot), to(fs), to(insl), n_slots


def _hgrn_tables(c):
    t = np.arange(c)[:, None]
    j = np.arange(c)[None, :]
    mats, masks = [], []
    m = c // 2
    while m >= 1:
        blk = t // (2 * m)
        bd = blk * 2 * m + m - 1
        second = (t % (2 * m)) >= m
        mat = np.where(second, (j > bd) & (j <= t), (j > t) & (j <= bd)).astype(np.float32)
        mats.append(mat)
        masks.append((blk == (j // (2 * m))).astype(np.float32))
        m //= 2
    mats.append((j <= t).astype(np.float32))
    masks.append((j == t).astype(np.float32))
    return (jnp.asarray(np.concatenate(mats, 0)), jnp.asarray(np.stack(masks, 0)),
            len(mats) - 1)


def _even_kernel(slot_ref, fs_ref, insl_ref,
                 pa_ref, qb_ref, kb_ref, vb_ref, ob_ref, gt_ref,
                 lbl_ref, hn_ref, mnw_ref, gbias_ref,
                 lvl_ref, lmask_ref, tri_ref, eye_ref,
                 hs_in, mc_in, mn_in, mm_in,
                 mrg_ref, hs_out, mc_out, mn_out, mm_out,
                 meta_hs, meta_mc, meta_mn, meta_mm,
                 *, lb_index, n_levels):
    c = CHUNK
    j = pl.program_id(0)
    fs = fs_ref[j]

    @pl.when(fs == 1)
    def _():
        hs_out[...] = jnp.zeros_like(hs_out)
        mc_out[...] = jnp.zeros_like(mc_out)
        mn_out[...] = jnp.zeros_like(mn_out)
        mm_out[...] = jnp.zeros_like(mm_out)

    @pl.when(fs == 2)
    def _():
        hs_out[...] = meta_hs[...]
        mc_out[...] = meta_mc[...]
        mn_out[...] = meta_mn[...]
        mm_out[...] = meta_mm[...]

    @pl.when(fs == 3)
    def _():
        hs_out[...] = hs_in[...]
        mc_out[...] = mc_in[...]
        mn_out[...] = mn_in[...]
        mm_out[...] = mm_in[...]

    hk = HA_HEADS * HA_DK
    lbl = lbl_ref[...]
    ex = jnp.exp(lbl - jnp.max(lbl, axis=0, keepdims=True))
    sm = ex / jnp.sum(ex, axis=0, keepdims=True)
    lb = jnp.sum(sm[:lb_index + 1], axis=0, keepdims=True)

    fa = pa_ref[:, hk:2 * hk]
    log_f = jnp.log(lb + (1.0 - lb) * _sigmoid(fa))
    k_all = (1.0 - lb) * _sigmoid(-fa)
    q_all = _silu(pa_ref[:, 0:hk])
    e_all = _nn(lvl_ref[...], log_f, HI)
    row = lax.broadcasted_iota(jnp.int32, (c, HA_DK), 0)

    for h in range(HA_HEADS):
        sl = slice(h * HA_DK, (h + 1) * HA_DK)
        q_h, k_h = q_all[:, sl], k_all[:, sl]
        v_h = pa_ref[:, 2 * hk + h * HA_DV:2 * hk + (h + 1) * HA_DV]
        scores = lmask_ref[n_levels] * _nt(q_h.astype(BF16), k_h.astype(BF16))
        m = c // 2
        for lvl in range(n_levels):
            dec = jnp.exp(e_all[lvl * c:(lvl + 1) * c, sl])
            second = (row & m) != 0
            qs = jnp.where(second, q_h * dec, 0.0).astype(BF16)
            ks = jnp.where(second, 0.0, k_h * dec).astype(BF16)
            part = _nt(qs, ks)
            scores = scores + (part if lvl == 0 else lmask_ref[lvl] * part)
            m //= 2
        b_h = e_all[n_levels * c:(n_levels + 1) * c, sl]
        st = hs_out[0, h]
        o = _nn(scores.astype(BF16), v_h.astype(BF16)) \
            + _nt((q_h * jnp.exp(b_h)).astype(BF16), st.astype(BF16))
        b_last = b_h[c - 1:c, :]
        k_til = k_h * jnp.exp(b_last - b_h)
        hs_out[0, h] = st * jnp.exp(b_last) + _tn(v_h.astype(BF16), k_til.astype(BF16))
        ga = pa_ref[:, 3 * hk + h * HA_DV:3 * hk + (h + 1) * HA_DV]
        mrg_ref[:, h * HA_DV:(h + 1) * HA_DV] = (_rms(o, hn_ref[:, sl]) * _silu(ga)).astype(BF16)

    lane = lax.broadcasted_iota(jnp.int32, (c, LANE), 1)
    rowg = lax.broadcasted_iota(jnp.int32, (c, LANE), 0)
    gcap = GATE_CAP * jnp.tanh((gt_ref[...] + gbias_ref[...]) / GATE_CAP)
    is_f = (lane >= HB_HEADS) & (lane < 2 * HB_HEADS)
    pad = rowg < jnp.where(j == 0, c - N_META, 0)
    lf_t = jnp.where(is_f & jnp.logical_not(pad), _log_sigmoid(gcap), 0.0)
    ig_t = jnp.where(pad, NEG_BIG, gcap)
    fc = _nn(tri_ref[...], lf_t, HI)
    fr = _nt(eye_ref[...], fc, HI)
    ir = _nt(eye_ref[...], ig_t, HI)
    r_i = lax.broadcasted_iota(jnp.int32, (c, c), 0)
    c_i = lax.broadcasted_iota(jnp.int32, (c, c), 1)
    causal = c_i <= r_i
    m_row = mm_out[0]
    m_row_new = m_row
    lane1 = lax.broadcasted_iota(jnp.int32, (1, LANE), 1)

    for h in range(HB_HEADS):
        fl = HB_HEADS + h
        q_h = qb_ref[:, h * HB_DQK:(h + 1) * HB_DQK]
        k_h = kb_ref[:, h * HB_DQK:(h + 1) * HB_DQK] * (HB_DQK ** -0.5)
        v_h = vb_ref[:, h * HB_DV:(h + 1) * HB_DV]
        fc_h = fc[:, fl:fl + 1]
        fr_h = fr[fl:fl + 1, :]
        igr_h = ir[h:h + 1, :]
        igc_h = ig_t[:, h:h + 1]
        m_prev = m_row[:, fl:fl + 1]
        log_d = jnp.where(causal, fc_h - fr_h + igr_h, -jnp.inf)
        log_inter = fc_h + m_prev
        m_t = jnp.maximum(log_inter, jnp.max(log_d, axis=1, keepdims=True))
        d_mat = jnp.exp(log_d - m_t)
        w_inter = jnp.exp(log_inter - m_t)
        qb16 = q_h.astype(BF16)
        qk = _nt(qb16, k_h.astype(BF16)) * d_mat
        ct = mc_out[0, h]
        n_row = mn_out[0, h]
        num = _nn(qk.astype(BF16), v_h.astype(BF16)) + w_inter * _nt(qb16, ct.astype(BF16))
        den = jnp.sum(qk, axis=1, keepdims=True) \
            + w_inter * jnp.sum(q_h * n_row, axis=1, keepdims=True)
        hh = num / jnp.maximum(jnp.abs(den), jnp.exp(-m_t))
        f_last = fc_h[c - 1:c, :]
        m_new = m_t[c - 1:c, :]
        w_s = jnp.exp(f_last - fc_h + igc_h - m_new)
        decay = jnp.exp(f_last + m_prev - m_new)
        kw = k_h * w_s
        mc_out[0, h] = decay * ct + _tn(v_h.astype(BF16), kw.astype(BF16))
        mn_out[0, h] = decay * n_row + jnp.sum(kw, axis=0, keepdims=True)
        m_row_new = jnp.where(lane1 == fl, m_new, m_row_new)
        ob = ob_ref[:, h * HB_DV:(h + 1) * HB_DV]
        col0 = HA_HEADS * HA_DV + h * HB_DV
        mrg_ref[:, col0:col0 + HB_DV] = (
            _rms(hh, mnw_ref[:, h * HB_DV:(h + 1) * HB_DV]) * _sigmoid(ob)).astype(BF16)
    mm_out[0] = m_row_new

    @pl.when(j == 0)
    def _():
        meta_hs[...] = hs_out[...]
        meta_mc[...] = mc_out[...]
        meta_mn[...] = mn_out[...]
        meta_mm[...] = mm_out[...]


def _even_mixer(p, tables, lb_logits, lb_index, hgrn_norm, mlstm_norm, gbias, hs_in, mc_in, mn_in, mm_in):
    slot, fs, insl, n_slots = tables
    c = CHUNK
    n_jobs = p.shape[0] // c
    lvl, lmask, n_levels = _hgrn_tables(c)
    tri = jnp.asarray(np.tril(np.ones((c, c), np.float32)))
    eye = jnp.eye(LANE, dtype=F32)
    hk = HA_HEADS * HA_DK
    wa = 4 * hk
    qk_w = HB_HEADS * HB_DQK
    v_w = HB_HEADS * HB_DV
    g_col = wa + 2 * qk_w + 2 * v_w
    cst = lambda shape: pl.BlockSpec(shape, lambda j, *_: (0,) * len(shape))
    row_blk = lambda w, idx: pl.BlockSpec((c, w), lambda j, *_: (j, idx))
    st_in = lambda shape: pl.BlockSpec((1,) + shape, lambda j, s, f, i: (i[j],) + (0,) * len(shape))
    st_out = lambda shape: pl.BlockSpec((1,) + shape, lambda j, s, f, i: (s[j],) + (0,) * len(shape))
    sh_hs = (HA_HEADS, HA_DV, HA_DK)
    sh_mc = (HB_HEADS, HB_DV, HB_DQK)
    sh_mn = (HB_HEADS, 1, HB_DQK)
    sh_mm = (1, LANE)
    grid_spec = pltpu.PrefetchScalarGridSpec(
        num_scalar_prefetch=3,
        grid=(n_jobs,),
        in_specs=[row_blk(wa, 0),
                  row_blk(qk_w, wa // qk_w), row_blk(qk_w, wa // qk_w + 1),
                  row_blk(v_w, (wa + 2 * qk_w) // v_w), row_blk(v_w, (wa + 2 * qk_w) // v_w + 1),
                  row_blk(LANE, g_col // LANE),
                  cst(lb_logits.shape), cst((1, hk)), cst((1, v_w)), cst((1, LANE)),
                  cst(lvl.shape), cst(lmask.shape), cst((c, c)), cst((LANE, LANE)),
                  st_in(sh_hs), st_in(sh_mc), st_in(sh_mn), st_in(sh_mm)],
        out_specs=[pl.BlockSpec((c, hk + v_w), lambda j, *_: (j, 0)),
                   st_out(sh_hs), st_out(sh_mc), st_out(sh_mn), st_out(sh_mm)],
        scratch_shapes=[pltpu.VMEM((1,) + sh_hs, F32), pltpu.VMEM((1,) + sh_mc, F32),
                        pltpu.VMEM((1,) + sh_mn, F32), pltpu.VMEM((1,) + sh_mm, F32)],
    )
    out_shape = [jax.ShapeDtypeStruct((p.shape[0], hk + v_w), BF16),
                 jax.ShapeDtypeStruct((n_slots,) + sh_hs, F32),
                 jax.ShapeDtypeStruct((n_slots,) + sh_mc, F32),
                 jax.ShapeDtypeStruct((n_slots,) + sh_mn, F32),
                 jax.ShapeDtypeStruct((n_slots,) + sh_mm, F32)]
    return pl.pallas_call(
        functools.partial(_even_kernel, lb_index=lb_index, n_levels=n_levels),
        grid_spec=grid_spec,
        out_shape=out_shape,
        compiler_params=pltpu.CompilerParams(dimension_semantics=("arbitrary",),
                                             vmem_limit_bytes=VMEM_LIMIT),
        name="even_mixer",
    )(slot, fs, insl, p, p, p, p, p, p, lb_logits, hgrn_norm, mlstm_norm, gbias,
      lvl, lmask, tri, eye, hs_in, mc_in, mn_in, mm_in)


def _gdn_kernel(slot_ref, fs_ref, insl_ref,
                qkv_ref, z_ref, gt_ref, cw_ref, alog_ref, dtb_ref, gnw_ref,
                tri_ref, eye_ref, s_in, cv_in,
                mrg_ref, s_out, cv_out,
                xpad, meta_s, meta_cv, *, n_sq, hps):
    c = CHUNK
    hg = pl.program_id(0)
    j = pl.program_id(1)
    fs = fs_ref[j]
    nk = CONV_W - 1
    base = 8 - nk

    @pl.when(fs == 1)
    def _():
        s_out[...] = jnp.zeros_like(s_out)
        xpad[base:8, :] = jnp.zeros((nk, xpad.shape[1]), F32)

    @pl.when(fs == 2)
    def _():
        s_out[...] = meta_s[...]
        xpad[base:8, :] = meta_cv[...]

    @pl.when(fs == 3)
    def _():
        s_out[...] = s_in[...]
        xpad[base:8, :] = cv_in[0]

    xpad[8:8 + c, :] = qkv_ref[...]
    y = cw_ref[0:1, :] * xpad[base:base + c, :]
    for t in range(1, CONV_W):
        y = y + cw_ref[t:t + 1, :] * xpad[base + t:base + t + c, :]
    carry = xpad[base + c:8 + c, :]
    cv_out[0] = carry
    xpad[base:8, :] = carry
    act = _silu(y)

    gt = gt_ref[...]
    beta_t = _sigmoid(gt)
    g_t = -jnp.exp(alog_ref[...]) * _softplus(gt + dtb_ref[...])
    gc = _nn(tri_ref[...], g_t, HI)
    gr = _nt(eye_ref[...], gc, HI)
    lane = lax.broadcasted_iota(jnp.int32, (c, LANE), 1)
    sub = lax.broadcasted_iota(jnp.int32, (LANE, c), 0)
    r_i = lax.broadcasted_iota(jnp.int32, (c, c), 0)
    c_i = lax.broadcasted_iota(jnp.int32, (c, c), 1)
    causal = c_i <= r_i
    strict = c_i < r_i
    eye_c = (c_i == r_i).astype(F32)
    wq = 4 * HC_DK

    heads = [(p, jj) for p in range(hps) for jj in range(2)]
    qn, kn, qk, kk = [], [], [], []
    for p in range(hps):
        q = act[:, p * wq:p * wq + HC_DK]
        k = act[:, p * wq + HC_DK:p * wq + 2 * HC_DK]
        qn.append(q * lax.rsqrt(jnp.sum(q * q, axis=-1, keepdims=True) + NORM_EPS) * (HC_DK ** -0.5))
        kn.append(k * lax.rsqrt(jnp.sum(k * k, axis=-1, keepdims=True) + NORM_EPS))
    for p in range(hps):
        qn16, kn16 = qn[p].astype(BF16), kn[p].astype(BF16)
        qk.append(_nt(qn16, kn16))
        kk.append(_nt(kn16, kn16))

    beta, gc_h, decay, inv, pw = [], [], [], [], []
    for p, jj in heads:
        hv = 2 * (hg * hps + p) + jj
        beta.append(jnp.sum(jnp.where(lane == hv, beta_t, 0.0), axis=1, keepdims=True))
        gc_h.append(jnp.sum(jnp.where(lane == HC_V_HEADS + hv, gc, 0.0), axis=1, keepdims=True))
        gr_h = jnp.sum(jnp.where(sub == HC_V_HEADS + hv, gr, 0.0), axis=0, keepdims=True)
        decay.append(jnp.exp(jnp.where(causal, gc_h[-1] - gr_h, -jnp.inf)))
        low = jnp.where(strict, beta[-1] * kk[p] * decay[-1], 0.0)
        inv.append(eye_c - low)
        pw.append(low.astype(BF16))
    pw = [_nn(x, x).astype(BF16) for x in pw]
    for i in range(n_sq):
        inv = [x + _nn(x.astype(BF16), y) for x, y in zip(inv, pw)]
        if i < n_sq - 1:
            pw = [_nn(y, y).astype(BF16) for y in pw]

    eg = [jnp.exp(x) for x in gc_h]
    sol = []
    for n, (p, jj) in enumerate(heads):
        v_h = act[:, p * wq + (2 + jj) * HC_DK:p * wq + (3 + jj) * HC_DK]
        rhs = jnp.concatenate([v_h * beta[n], kn[p] * (beta[n] * eg[n])], axis=1)
        sol.append(_nn(inv[n].astype(BF16), rhs.astype(BF16)))
    s_old = [s_out[0, n] for n in range(len(heads))]
    ws = []
    for n, (p, jj) in enumerate(heads):
        lhs = jnp.concatenate([sol[n][:, HC_DV:], qn[p] * eg[n]], axis=0)
        ws.append(_nn(lhs.astype(BF16), s_old[n].astype(BF16)))
    v_new = [(sol[n][:, :HC_DV] - ws[n][:c]).astype(BF16) for n in range(len(heads))]
    for n, (p, jj) in enumerate(heads):
        o = ws[n][c:] + _nn((qk[p] * decay[n]).astype(BF16), v_new[n])
        g_last = gc_h[n][c - 1:c, :]
        k_til = kn[p] * jnp.exp(g_last - gc_h[n])
        s_out[0, n] = jnp.exp(g_last) * s_old[n] + _tn(k_til.astype(BF16), v_new[n])
        z = z_ref[:, n * HC_DV:(n + 1) * HC_DV]
        mrg_ref[:, n * HC_DV:(n + 1) * HC_DV] = (_rms(o, gnw_ref[...]) * _silu(z)).astype(BF16)

    @pl.when(j == 0)
    def _():
        meta_s[...] = s_out[...]
        meta_cv[...] = xpad[base:8, :]


def _gdn_mixer(p, tables, conv_w, alog_row, dtb_row, gnw, s_in, cv_in, hps=8):
    slot, fs, insl, n_slots = tables
    c = CHUNK
    n_jobs = p.shape[0] // c
    n_hg = HC_QK_HEADS // hps
    wq = hps * 4 * HC_DK
    wz = hps * 2 * HC_DV
    qkv_w = HC_QK_HEADS * 4 * HC_DK
    z_w = HC_V_HEADS * HC_DV
    n_sq = int(np.log2(c)) - 1
    tri = jnp.asarray(np.tril(np.ones((c, c), np.float32)))
    eye = jnp.eye(LANE, dtype=F32)
    nk = CONV_W - 1
    sh_s = (1, 2 * hps, HC_DK, HC_DV)
    cst = lambda shape: pl.BlockSpec(shape, lambda h, j, *_: (0,) * len(shape))
    grid_spec = pltpu.PrefetchScalarGridSpec(
        num_scalar_prefetch=3,
        grid=(n_hg, n_jobs),
        in_specs=[pl.BlockSpec((c, wq), lambda h, j, *_: (j, h)),
                  pl.BlockSpec((c, wz), lambda h, j, *_: (j, qkv_w // wz + h)),
                  pl.BlockSpec((c, LANE), lambda h, j, *_: (j, (qkv_w + z_w) // LANE)),
                  pl.BlockSpec((CONV_W, wq), lambda h, j, *_: (0, h)),
                  cst((1, LANE)), cst((1, LANE)), cst((1, HC_DV)),
                  cst((c, c)), cst((LANE, LANE)),
                  pl.BlockSpec(sh_s, lambda h, j, s, f, i: (i[j], h, 0, 0)),
                  pl.BlockSpec((1, nk, wq), lambda h, j, s, f, i: (i[j], 0, h))],
        out_specs=[pl.BlockSpec((c, wz), lambda h, j, *_: (j, h)),
                   pl.BlockSpec(sh_s, lambda h, j, s, f, i: (s[j], h, 0, 0)),
                   pl.BlockSpec((1, nk, wq), lambda h, j, s, f, i: (s[j], 0, h))],
        scratch_shapes=[pltpu.VMEM((8 + c, wq), F32),
                        pltpu.VMEM(sh_s, F32),
                        pltpu.VMEM((nk, wq), F32)],
    )
    out_shape = [jax.ShapeDtypeStruct((p.shape[0], z_w), BF16),
                 jax.ShapeDtypeStruct((n_slots, HC_V_HEADS, HC_DK, HC_DV), F32),
                 jax.ShapeDtypeStruct((n_slots, nk, qkv_w), F32)]
    return pl.pallas_call(
        functools.partial(_gdn_kernel, n_sq=n_sq, hps=hps),
        grid_spec=grid_spec,
        out_shape=out_shape,
        compiler_params=pltpu.CompilerParams(dimension_semantics=("arbitrary", "arbitrary"),
                                             vmem_limit_bytes=VMEM_LIMIT),
        name="gdn_mixer",
    )(slot, fs, insl, p, p, p, conv_w, alog_row, dtb_row, gnw, tri, eye, s_in, cv_in)


def _gdn_perm():
    nq = HC_QK_HEADS * HC_DK
    idx = []
    for hp in range(HC_QK_HEADS):
        idx.append(np.arange(hp * HC_DK, (hp + 1) * HC_DK))
        idx.append(nq + np.arange(hp * HC_DK, (hp + 1) * HC_DK))
        idx.append(2 * nq + np.arange(2 * hp * HC_DV, (2 * hp + 2) * HC_DV))
    return np.concatenate(idx)


def _pad_cols(w, width):
    return jnp.pad(w, ((0, 0), (0, width - w.shape[1])))


def kernel(x_prompt, x_sample, state_hgrn_S, state_mlstm_C, state_mlstm_n, state_mlstm_m, state_gdn_S,
           state_gdn_conv, meta_tokens, norm_mix, norm_ffn, norm_final, even_w_in, even_w_out,
           hgrn_lb_logits, hgrn_norm, mlstm_b_i, mlstm_b_f, mlstm_norm, odd_w_in, odd_conv_w, gdn_a_log,
           gdn_dt_bias, gdn_norm, odd_w_out, ffn_w_in, ffn_w_out):
    n_b, t_len, d = x_prompt.shape
    n_s, l_s, _ = x_sample.shape
    c = CHUNK
    assert t_len % c == 0 and l_s % c == 0 and meta_tokens.shape[0] == N_META
    assert norm_mix.shape[0] == 2, "one even (HGRN2+mLSTM) and one odd (DeltaNet) layer"
    dt = x_prompt.dtype

    x = jnp.concatenate([jnp.zeros((c - N_META, d), dt), meta_tokens.astype(dt),
                         x_prompt.reshape(n_b * t_len, d), x_sample.reshape(n_s * l_s, d)], axis=0)
    tables = _job_tables(n_b, t_len // c, n_s, l_s // c)
    n_slots = tables[3]
    row = lambda v: v.reshape(1, -1).astype(F32)

    e = 0
    n_main = 4 * HA_HEADS * HA_DK + 2 * HB_HEADS * HB_DQK + 2 * HB_HEADS * HB_DV
    w_even = jnp.concatenate([even_w_in[e][:, :n_main], _pad_cols(even_w_in[e][:, n_main:], LANE)],
                             axis=1).astype(BF16)
    p_even = _proj(x, row(norm_mix[0]), w_even, "proj_even")
    gbias = _pad_cols(jnp.concatenate([mlstm_b_i[e], mlstm_b_f[e]]).reshape(1, -1).astype(F32), LANE)
    hs_in = jnp.swapaxes(state_hgrn_S[e].astype(F32), -1, -2)
    mc_in = jnp.swapaxes(state_mlstm_C[e].astype(F32), -1, -2)
    mn_in = state_mlstm_n[e].astype(F32)[:, :, None, :]
    mm_in = jnp.pad(state_mlstm_m[e].astype(F32), ((0, 0), (HB_HEADS, LANE - 2 * HB_HEADS)))[:, None, :]
    mrg0, hs_o, mc_o, mn_o, mm_o = _even_mixer(
        p_even, tables, hgrn_lb_logits.astype(F32), e, row(hgrn_norm[e]), row(mlstm_norm[e]), gbias,
        hs_in, mc_in, mn_in, mm_in)
    x = _post(x, mrg0, even_w_out[e].astype(BF16), row(norm_ffn[0]), ffn_w_in[0].astype(BF16),
              ffn_w_out[0].astype(BF16), row(norm_final), False, "post_even")

    o = 0
    perm = _gdn_perm()
    n_qkv = perm.shape[0]
    n_z = HC_V_HEADS * HC_DV
    w_odd = jnp.concatenate([odd_w_in[o][:, perm], odd_w_in[o][:, n_qkv:n_qkv + n_z],
                             _pad_cols(odd_w_in[o][:, n_qkv + n_z:], LANE)], axis=1).astype(BF16)
    p_odd = _proj(x, row(norm_mix[1]), w_odd, "proj_odd")
    lane_pad = lambda v: jnp.pad(v.reshape(1, -1).astype(F32),
                                 ((0, 0), (HC_V_HEADS, LANE - 2 * HC_V_HEADS)))
    mrg1, gs_o, cv_o = _gdn_mixer(
        p_odd, tables, odd_conv_w[o].astype(F32)[:, perm], lane_pad(gdn_a_log[o]), lane_pad(gdn_dt_bias[o]),
        row(gdn_norm[o]), state_gdn_S[o].astype(F32), state_gdn_conv[o].astype(F32)[:, :, perm])
    y = _post(x, mrg1, odd_w_out[o].astype(BF16), row(norm_ffn[1]), ffn_w_in[1].astype(BF16),
              ffn_w_out[1].astype(BF16), row(norm_final), True, "post_odd")

    y_prompt = y[c:c + n_b * t_len].reshape(n_b, t_len, d)
    y_sample = y[c + n_b * t_len:].reshape(n_s, l_s, d)
    inv_perm = np.argsort(perm)
    hs_o = jnp.swapaxes(hs_o, -1, -2)
    mc_o = jnp.swapaxes(mc_o, -1, -2)
    mn_o = mn_o[:, :, 0, :]
    mm_o = mm_o[:, 0, HB_HEADS:2 * HB_HEADS]
    cv_o = cv_o[:, :, inv_perm]
    stk = lambda a, lo, hi: a[lo:hi][None].astype(dt)
    p_sl, s_sl = (0, n_b), (n_b, n_b + n_s)
    outs = [y_prompt, y_sample]
    for lo, hi in (p_sl, s_sl):
        outs += [stk(hs_o, lo, hi), stk(mc_o, lo, hi), stk(mn_o, lo, hi), stk(mm_o, lo, hi),
                 stk(gs_o, lo, hi), stk(cv_o, lo, hi)]
    return tuple(outs)
```

```python
import functools

import numpy as np
import jax
import jax.numpy as jnp
from jax import lax
from jax.experimental import pallas as pl
from jax.experimental.pallas import tpu as pltpu

F32 = jnp.float32
BF16 = jnp.bfloat16
HI = lax.Precision.HIGHEST

CHUNK = 64
N_META = 16
NORM_EPS = 1e-6
GATE_CAP = 15.0
NEG_BIG = -1e30

HA_HEADS, HA_DK, HA_DV = 4, 128, 128
HB_HEADS, HB_DQK, HB_DV = 4, 64, 128
HC_QK_HEADS, HC_V_HEADS, HC_DK, HC_DV = 8, 16, 128, 128
CONV_W = 4
LANE = 128
VMEM_LIMIT = 56 * 1024 * 1024


def _nn(a, b, prec=None):
    return lax.dot_general(a, b, (((1,), (0,)), ((), ())), precision=prec,
                           preferred_element_type=F32)


def _nt(a, b, prec=None):
    return lax.dot_general(a, b, (((1,), (1,)), ((), ())), precision=prec,
                           preferred_element_type=F32)


def _tn(a, b, prec=None):
    return lax.dot_general(a, b, (((0,), (0,)), ((), ())), precision=prec,
                           preferred_element_type=F32)


def _dot_pieces(dot, exact, x, n):
    e16 = exact.astype(BF16)
    acc = None
    for _ in range(n):
        piece = x.astype(BF16)
        part = dot(e16, piece)
        acc = part if acc is None else acc + part
        x = x - piece.astype(F32)
    return acc


def _sigmoid(x):
    return 1.0 / (1.0 + jnp.exp(-x))


def _silu(x):
    return x * _sigmoid(x)


def _softplus(x):
    return jnp.maximum(x, 0.0) + jnp.log1p(jnp.exp(-jnp.abs(x)))


def _log_sigmoid(x):
    return jnp.minimum(x, 0.0) - jnp.log1p(jnp.exp(-jnp.abs(x)))


def _rms(x, w):
    ms = jnp.mean(x * x, axis=-1, keepdims=True)
    return x * lax.rsqrt(ms + NORM_EPS) * w


def _proj_kernel(x_ref, nw_ref, w_ref, o_ref, *, n_chunk):
    h = _rms(x_ref[...], nw_ref[...]).astype(BF16)
    n_total = w_ref.shape[1]
    for n0 in range(0, n_total, n_chunk):
        n1 = min(n0 + n_chunk, n_total)
        o_ref[:, n0:n1] = _nn(h, w_ref[:, n0:n1])


def _post_kernel(x_ref, m_ref, wo_ref, nw_ref, wi_ref, w2_ref, nf_ref, o_ref, act_ref,
                 *, hidden, h_chunk, final_norm):
    x1 = x_ref[...] + _nn(m_ref[...], wo_ref[...])
    h = _rms(x1, nw_ref[...]).astype(BF16)
    for c0 in range(0, hidden, h_chunk):
        c1 = min(c0 + h_chunk, hidden)
        gate = _nn(h, wi_ref[:, c0:c1])
        up = _nn(h, wi_ref[:, hidden + c0:hidden + c1])
        act_ref[:, c0:c1] = (_silu(gate) * up).astype(BF16)
    x2 = x1 + _nn(act_ref[...], w2_ref[...])
    if final_norm:
        x2 = _rms(x2, nf_ref[...])
    o_ref[...] = x2


def _pick_tm(m_rows, cap=512):
    best = CHUNK
    k = 1
    while CHUNK * k <= cap:
        if m_rows % (CHUNK * k) == 0:
            best = CHUNK * k
        k += 1
    return best


def _const_spec(shape):
    nd = len(shape)
    return pl.BlockSpec(shape, lambda i, _n=nd: (0,) * _n)


def _proj(x, nw, w, name):
    m_rows, d = x.shape
    n = w.shape[1]
    tm = _pick_tm(m_rows)
    return pl.pallas_call(
        functools.partial(_proj_kernel, n_chunk=512),
        grid=(m_rows // tm,),
        in_specs=[pl.BlockSpec((tm, d), lambda i: (i, 0)),
                  _const_spec((1, d)),
                  _const_spec((d, n))],
        out_specs=pl.BlockSpec((tm, n), lambda i: (i, 0)),
        out_shape=jax.ShapeDtypeStruct((m_rows, n), F32),
        compiler_params=pltpu.CompilerParams(dimension_semantics=("arbitrary",),
                                             vmem_limit_bytes=VMEM_LIMIT),
        name=name,
    )(x, nw, w)


def _post(x, mrg, wo, nw, wi, w2, nf, final_norm, name):
    m_rows, d = x.shape
    km = mrg.shape[1]
    hidden = w2.shape[0]
    tm = _pick_tm(m_rows)
    return pl.pallas_call(
        functools.partial(_post_kernel, hidden=hidden, h_chunk=256, final_norm=final_norm),
        grid=(m_rows // tm,),
        in_specs=[pl.BlockSpec((tm, d), lambda i: (i, 0)),
                  pl.BlockSpec((tm, km), lambda i: (i, 0)),
                  _const_spec((km, d)),
                  _const_spec((1, d)),
                  _const_spec((d, 2 * hidden)),
                  _const_spec((hidden, d)),
                  _const_spec((1, d))],
        out_specs=pl.BlockSpec((tm, d), lambda i: (i, 0)),
        out_shape=jax.ShapeDtypeStruct((m_rows, d), F32),
        scratch_shapes=[pltpu.VMEM((tm, hidden), BF16)],
        compiler_params=pltpu.CompilerParams(dimension_semantics=("arbitrary",),
                                             vmem_limit_bytes=VMEM_LIMIT),
        name=name,
    )(x, mrg, wo, nw, wi, w2, nf)


def _job_tables(n_prompt, chunks_prompt, n_sample, chunks_sample):
    n_slots = n_prompt + n_sample + 1
    slot, fs, insl = [n_slots - 1], [1], [0]
    for b in range(n_prompt):
        for c in range(chunks_prompt):
            slot.append(b); fs.append(2 if c == 0 else 0); insl.append(0)
    for s in range(n_sample):
        for c in range(chunks_sample):
            slot.append(n_prompt + s); fs.append(3 if c == 0 else 0); insl.append(s)
    to = lambda v: jnp.asarray(np.asarray(v, np.int32))
    return to(slot), to(fs), to(insl), n_slots


def _hgrn_tables(c):
    t = np.arange(c)[:, None]
    j = np.arange(c)[None, :]
    mats, masks = [], []
    m = c // 2
    while m >= 1:
        blk = t // (2 * m)
        bd = blk * 2 * m + m - 1
        second = (t % (2 * m)) >= m
        mat = np.where(second, (j > bd) & (j <= t), (j > t) & (j <= bd)).astype(np.float32)
        mats.append(mat)
        masks.append((blk == (j // (2 * m))).astype(np.float32))
        m //= 2
    mats.append((j <= t).astype(np.float32))
    masks.append((j == t).astype(np.float32))
    return (jnp.asarray(np.concatenate(mats, 0)), jnp.asarray(np.stack(masks, 0)),
            len(mats) - 1)


def _even_kernel(slot_ref, fs_ref, insl_ref,
                 pa_ref, qb_ref, kb_ref, vb_ref, ob_ref, gt_ref,
                 lbl_ref, hn_ref, mnw_ref, gbias_ref,
                 lvl_ref, lmask_ref, tri_ref, eye_ref,
                 hs_in, mc_in, mn_in, mm_in,
                 mrg_ref, hs_out, mc_out, mn_out, mm_out,
                 meta_hs, meta_mc, meta_mn, meta_mm,
                 *, lb_index, n_levels):
    c = CHUNK
    j = pl.program_id(0)
    fs = fs_ref[j]

    @pl.when(fs == 1)
    def _():
        hs_out[...] = jnp.zeros_like(hs_out)
        mc_out[...] = jnp.zeros_like(mc_out)
        mn_out[...] = jnp.zeros_like(mn_out)
        mm_out[...] = jnp.zeros_like(mm_out)

    @pl.when(fs == 2)
    def _():
        hs_out[...] = meta_hs[...]
        mc_out[...] = meta_mc[...]
        mn_out[...] = meta_mn[...]
        mm_out[...] = meta_mm[...]

    @pl.when(fs == 3)
    def _():
        hs_out[...] = hs_in[...]
        mc_out[...] = mc_in[...]
        mn_out[...] = mn_in[...]
        mm_out[...] = mm_in[...]

    hk = HA_HEADS * HA_DK
    lbl = lbl_ref[...]
    ex = jnp.exp(lbl - jnp.max(lbl, axis=0, keepdims=True))
    sm = ex / jnp.sum(ex, axis=0, keepdims=True)
    lb = jnp.sum(sm[:lb_index + 1], axis=0, keepdims=True)
    fa = pa_ref[:, hk:2 * hk]
    log_f = jnp.log(lb + (1.0 - lb) * _sigmoid(fa))
    k_all = (1.0 - lb) * _sigmoid(-fa)
    q_all = _silu(pa_ref[:, 0:hk])
    e_all = _dot_pieces(_nn, lvl_ref[...], log_f, 2)

    lane = lax.broadcasted_iota(jnp.int32, (c, LANE), 1)
    rowg = lax.broadcasted_iota(jnp.int32, (c, LANE), 0)
    gcap = GATE_CAP * jnp.tanh((gt_ref[...] + gbias_ref[...]) / GATE_CAP)
    is_f = (lane >= HB_HEADS) & (lane < 2 * HB_HEADS)
    pad = rowg < jnp.where(j == 0, c - N_META, 0)
    lf_t = jnp.where(is_f & jnp.logical_not(pad), _log_sigmoid(gcap), 0.0)
    ig_t = jnp.where(pad, NEG_BIG, gcap)
    fc = _dot_pieces(_nn, tri_ref[...], lf_t, 3)
    fr = _dot_pieces(_nt, eye_ref[...], fc, 3)
    ir = _dot_pieces(_nt, eye_ref[...], ig_t, 3)

    row = lax.broadcasted_iota(jnp.int32, (c, HA_DK), 0)
    hsl = [slice(h * HA_DK, (h + 1) * HA_DK) for h in range(HA_HEADS)]
    q_h = [q_all[:, s] for s in hsl]
    k_h = [k_all[:, s] for s in hsl]
    v16 = [pa_ref[:, 2 * hk + h * HA_DV:2 * hk + (h + 1) * HA_DV].astype(BF16) for h in range(HA_HEADS)]
    b_h = [e_all[n_levels * c:(n_levels + 1) * c, s] for s in hsl]
    st = [hs_out[0, h] for h in range(HA_HEADS)]
    o_inter = [_nt((q_h[h] * jnp.exp(b_h[h])).astype(BF16), st[h].astype(BF16)) for h in range(HA_HEADS)]
    for h in range(HA_HEADS):
        b_last = b_h[h][c - 1:c, :]
        k_til = k_h[h] * jnp.exp(b_last - b_h[h])
        hs_out[0, h] = st[h] * jnp.exp(b_last) + _tn(v16[h], k_til.astype(BF16))
    scores = [lmask_ref[n_levels] * _nt(q_h[h].astype(BF16), k_h[h].astype(BF16)) for h in range(HA_HEADS)]
    m = c // 2
    for lvl in range(n_levels):
        second = (row & m) != 0
        for h in range(HA_HEADS):
            dec = jnp.exp(e_all[lvl * c:(lvl + 1) * c, hsl[h]])
            qs = jnp.where(second, q_h[h] * dec, 0.0).astype(BF16)
            ks = jnp.where(second, 0.0, k_h[h] * dec).astype(BF16)
            part = _nt(qs, ks)
            scores[h] = scores[h] + (part if lvl == 0 else lmask_ref[lvl] * part)
        m //= 2

    r_i = lax.broadcasted_iota(jnp.int32, (c, c), 0)
    c_i = lax.broadcasted_iota(jnp.int32, (c, c), 1)
    causal = c_i <= r_i
    m_row = mm_out[0]
    m_row_new = m_row
    lane1 = lax.broadcasted_iota(jnp.int32, (1, LANE), 1)
    mq, mk, mv16, qkd, inter, m_ts, w_inters = [], [], [], [], [], [], []
    for h in range(HB_HEADS):
        mq.append(qb_ref[:, h * HB_DQK:(h + 1) * HB_DQK])
        mk.append(kb_ref[:, h * HB_DQK:(h + 1) * HB_DQK] * (HB_DQK ** -0.5))
        mv16.append(vb_ref[:, h * HB_DV:(h + 1) * HB_DV].astype(BF16))
    ct = [mc_out[0, h] for h in range(HB_HEADS)]
    n_row = [mn_out[0, h] for h in range(HB_HEADS)]
    qk_raw = [_nt(mq[h].astype(BF16), mk[h].astype(BF16)) for h in range(HB_HEADS)]
    inter = [_nt(mq[h].astype(BF16), ct[h].astype(BF16)) for h in range(HB_HEADS)]
    for h in range(HB_HEADS):
        fl = HB_HEADS + h
        fc_h = fc[:, fl:fl + 1]
        fr_h = fr[fl:fl + 1, :]
        igr_h = ir[h:h + 1, :]
        igc_h = ig_t[:, h:h + 1]
        m_prev = m_row[:, fl:fl + 1]
        log_d = jnp.where(causal, fc_h - fr_h + igr_h, -jnp.inf)
        log_inter = fc_h + m_prev
        m_t = jnp.maximum(log_inter, jnp.max(log_d, axis=1, keepdims=True))
        m_ts.append(m_t)
        w_inters.append(jnp.exp(log_inter - m_t))
        qkd.append(qk_raw[h] * jnp.exp(log_d - m_t))
        f_last = fc_h[c - 1:c, :]
        m_new = m_t[c - 1:c, :]
        w_s = jnp.exp(f_last - fc_h + igc_h - m_new)
        decay = jnp.exp(f_last + m_prev - m_new)
        kw = mk[h] * w_s
        mc_out[0, h] = decay * ct[h] + _tn(mv16[h], kw.astype(BF16))
        mn_out[0, h] = decay * n_row[h] + jnp.sum(kw, axis=0, keepdims=True)
        m_row_new = jnp.where(lane1 == fl, m_new, m_row_new)
    mm_out[0] = m_row_new

    o_a = [_nn(scores[h].astype(BF16), v16[h]) + o_inter[h] for h in range(HA_HEADS)]
    num = [_nn(qkd[h].astype(BF16), mv16[h]) + w_inters[h] * inter[h] for h in range(HB_HEADS)]
    for h in range(HA_HEADS):
        ga = pa_ref[:, 3 * hk + h * HA_DV:3 * hk + (h + 1) * HA_DV]
        mrg_ref[:, h * HA_DV:(h + 1) * HA_DV] = (_rms(o_a[h], hn_ref[:, hsl[h]]) * _silu(ga)).astype(BF16)
    for h in range(HB_HEADS):
        den = jnp.sum(qkd[h], axis=1, keepdims=True) \
            + w_inters[h] * jnp.sum(mq[h] * n_row[h], axis=1, keepdims=True)
        hh = num[h] / jnp.maximum(jnp.abs(den), jnp.exp(-m_ts[h]))
        ob = ob_ref[:, h * HB_DV:(h + 1) * HB_DV]
        col0 = HA_HEADS * HA_DV + h * HB_DV
        mrg_ref[:, col0:col0 + HB_DV] = (
            _rms(hh, mnw_ref[:, h * HB_DV:(h + 1) * HB_DV]) * _sigmoid(ob)).astype(BF16)

    @pl.when(j == 0)
    def _():
        meta_hs[...] = hs_out[...]
        meta_mc[...] = mc_out[...]
        meta_mn[...] = mn_out[...]
        meta_mm[...] = mm_out[...]


def _even_mixer(p, tables, lb_logits, lb_index, hgrn_norm, mlstm_norm, gbias, hs_in, mc_in, mn_in, mm_in):
    slot, fs, insl, n_slots = tables
    c = CHUNK
    n_jobs = p.shape[0] // c
    lvl, lmask, n_levels = _hgrn_tables(c)
    tri = jnp.asarray(np.tril(np.ones((c, c), np.float32)))
    eye = jnp.eye(LANE, dtype=F32)
    hk = HA_HEADS * HA_DK
    wa = 4 * hk
    qk_w = HB_HEADS * HB_DQK
    v_w = HB_HEADS * HB_DV
    g_col = wa + 2 * qk_w + 2 * v_w
    cst = lambda shape: pl.BlockSpec(shape, lambda j, *_: (0,) * len(shape))
    row_blk = lambda w, idx: pl.BlockSpec((c, w), lambda j, *_: (j, idx))
    st_in = lambda shape: pl.BlockSpec((1,) + shape, lambda j, s, f, i: (i[j],) + (0,) * len(shape))
    st_out = lambda shape: pl.BlockSpec((1,) + shape, lambda j, s, f, i: (s[j],) + (0,) * len(shape))
    sh_hs = (HA_HEADS, HA_DV, HA_DK)
    sh_mc = (HB_HEADS, HB_DV, HB_DQK)
    sh_mn = (HB_HEADS, 1, HB_DQK)
    sh_mm = (1, LANE)
    grid_spec = pltpu.PrefetchScalarGridSpec(
        num_scalar_prefetch=3,
        grid=(n_jobs,),
        in_specs=[row_blk(wa, 0),
                  row_blk(qk_w, wa // qk_w), row_blk(qk_w, wa // qk_w + 1),
                  row_blk(v_w, (wa + 2 * qk_w) // v_w), row_blk(v_w, (wa + 2 * qk_w) // v_w + 1),
                  row_blk(LANE, g_col // LANE),
                  cst(lb_logits.shape), cst((1, hk)), cst((1, v_w)), cst((1, LANE)),
                  cst(lvl.shape), cst(lmask.shape), cst((c, c)), cst((LANE, LANE)),
                  st_in(sh_hs), st_in(sh_mc), st_in(sh_mn), st_in(sh_mm)],
        out_specs=[pl.BlockSpec((c, hk + v_w), lambda j, *_: (j, 0)),
                   st_out(sh_hs), st_out(sh_mc), st_out(sh_mn), st_out(sh_mm)],
        scratch_shapes=[pltpu.VMEM((1,) + sh_hs, F32), pltpu.VMEM((1,) + sh_mc, F32),
                        pltpu.VMEM((1,) + sh_mn, F32), pltpu.VMEM((1,) + sh_mm, F32)],
    )
    out_shape = [jax.ShapeDtypeStruct((p.shape[0], hk + v_w), BF16),
                 jax.ShapeDtypeStruct((n_slots,) + sh_hs, F32),
                 jax.ShapeDtypeStruct((n_slots,) + sh_mc, F32),
                 jax.ShapeDtypeStruct((n_slots,) + sh_mn, F32),
                 jax.ShapeDtypeStruct((n_slots,) + sh_mm, F32)]
    return pl.pallas_call(
        functools.partial(_even_kernel, lb_index=lb_index, n_levels=n_levels),
        grid_spec=grid_spec,
        out_shape=out_shape,
        compiler_params=pltpu.CompilerParams(dimension_semantics=("arbitrary",),
                                             vmem_limit_bytes=VMEM_LIMIT),
        name="even_mixer",
    )(slot, fs, insl, p, p, p, p, p, p, lb_logits, hgrn_norm, mlstm_norm, gbias,
      lvl, lmask, tri, eye, hs_in, mc_in, mn_in, mm_in)


def _gdn_kernel(slot_ref, fs_ref, insl_ref,
                qkv_ref, z_ref, gt_ref, cw_ref, alog_ref, dtb_ref, gnw_ref,
                tri_ref, eye_ref, s_in, cv_in,
                mrg_ref, s_out, cv_out,
                xpad, meta_s, meta_cv, *, n_sq, hps):
    c = CHUNK
    hg = pl.program_id(0)
    j = pl.program_id(1)
    fs = fs_ref[j]
    nk = CONV_W - 1
    base = 8 - nk

    @pl.when(fs == 1)
    def _():
        s_out[...] = jnp.zeros_like(s_out)
        xpad[base:8, :] = jnp.zeros((nk, xpad.shape[1]), F32)

    @pl.when(fs == 2)
    def _():
        s_out[...] = meta_s[...]
        xpad[base:8, :] = meta_cv[...]

    @pl.when(fs == 3)
    def _():
        s_out[...] = s_in[...]
        xpad[base:8, :] = cv_in[0]

    xpad[8:8 + c, :] = qkv_ref[...]
    y = cw_ref[0:1, :] * xpad[base:base + c, :]
    for t in range(1, CONV_W):
        y = y + cw_ref[t:t + 1, :] * xpad[base + t:base + t + c, :]
    carry = xpad[base + c:8 + c, :]
    cv_out[0] = carry
    xpad[base:8, :] = carry
    act = _silu(y)

    gt = gt_ref[...]
    beta_t = _sigmoid(gt)
    g_t = -jnp.exp(alog_ref[...]) * _softplus(gt + dtb_ref[...])
    gc = _dot_pieces(_nn, tri_ref[...], g_t, 3)
    gr = _dot_pieces(_nt, eye_ref[...], gc, 3)
    lane = lax.broadcasted_iota(jnp.int32, (c, LANE), 1)
    sub = lax.broadcasted_iota(jnp.int32, (LANE, c), 0)
    r_i = lax.broadcasted_iota(jnp.int32, (c, c), 0)
    c_i = lax.broadcasted_iota(jnp.int32, (c, c), 1)
    causal = c_i <= r_i
    strict = c_i < r_i
    eye_c = (c_i == r_i).astype(F32)
    wq = 4 * HC_DK

    heads = [(p, jj) for p in range(hps) for jj in range(2)]
    qn, kn, qk, kk = [], [], [], []
    for p in range(hps):
        q = act[:, p * wq:p * wq + HC_DK]
        k = act[:, p * wq + HC_DK:p * wq + 2 * HC_DK]
        qn.append(q * lax.rsqrt(jnp.sum(q * q, axis=-1, keepdims=True) + NORM_EPS) * (HC_DK ** -0.5))
        kn.append(k * lax.rsqrt(jnp.sum(k * k, axis=-1, keepdims=True) + NORM_EPS))
    for p in range(hps):
        qn16, kn16 = qn[p].astype(BF16), kn[p].astype(BF16)
        qk.append(_nt(qn16, kn16))
        kk.append(_nt(kn16, kn16))

    beta, gc_h, decay, inv, pw = [], [], [], [], []
    for p, jj in heads:
        hv = 2 * (hg * hps + p) + jj
        beta.append(jnp.sum(jnp.where(lane == hv, beta_t, 0.0), axis=1, keepdims=True))
        gc_h.append(jnp.sum(jnp.where(lane == HC_V_HEADS + hv, gc, 0.0), axis=1, keepdims=True))
        gr_h = jnp.sum(jnp.where(sub == HC_V_HEADS + hv, gr, 0.0), axis=0, keepdims=True)
        decay.append(jnp.exp(jnp.where(causal, gc_h[-1] - gr_h, -jnp.inf)))
        low = jnp.where(strict, beta[-1] * kk[p] * decay[-1], 0.0)
        inv.append(eye_c - low)
        pw.append(low.astype(BF16))
    pw = [_nn(x, x).astype(BF16) for x in pw]
    for i in range(n_sq):
        inv = [x + _nn(x.astype(BF16), y) for x, y in zip(inv, pw)]
        if i < n_sq - 1:
            pw = [_nn(y, y).astype(BF16) for y in pw]

    eg = [jnp.exp(x) for x in gc_h]
    sol = []
    for n, (p, jj) in enumerate(heads):
        v_h = act[:, p * wq + (2 + jj) * HC_DK:p * wq + (3 + jj) * HC_DK]
        rhs = jnp.concatenate([v_h * beta[n], kn[p] * (beta[n] * eg[n])], axis=1)
        sol.append(_nn(inv[n].astype(BF16), rhs.astype(BF16)))
    s_old = [s_out[0, n] for n in range(len(heads))]
    ws = []
    for n, (p, jj) in enumerate(heads):
        lhs = jnp.concatenate([sol[n][:, HC_DV:], qn[p] * eg[n]], axis=0)
        ws.append(_nn(lhs.astype(BF16), s_old[n].astype(BF16)))
    v_new = [(sol[n][:, :HC_DV] - ws[n][:c]).astype(BF16) for n in range(len(heads))]
    for n, (p, jj) in enumerate(heads):
        o = ws[n][c:] + _nn((qk[p] * decay[n]).astype(BF16), v_new[n])
        g_last = gc_h[n][c - 1:c, :]
        k_til = kn[p] * jnp.exp(g_last - gc_h[n])
        s_out[0, n] = jnp.exp(g_last) * s_old[n] + _tn(k_til.astype(BF16), v_new[n])
        z = z_ref[:, n * HC_DV:(n + 1) * HC_DV]
        mrg_ref[:, n * HC_DV:(n + 1) * HC_DV] = (_rms(o, gnw_ref[...]) * _silu(z)).astype(BF16)

    @pl.when(j == 0)
    def _():
        meta_s[...] = s_out[...]
        meta_cv[...] = xpad[base:8, :]


def _gdn_mixer(p, tables, conv_w, alog_row, dtb_row, gnw, s_in, cv_in, hps=8):
    slot, fs, insl, n_slots = tables
    c = CHUNK
    n_jobs = p.shape[0] // c
    n_hg = HC_QK_HEADS // hps
    wq = hps * 4 * HC_DK
    wz = hps * 2 * HC_DV
    qkv_w = HC_QK_HEADS * 4 * HC_DK
    z_w = HC_V_HEADS * HC_DV
    n_sq = int(np.log2(c)) - 1
    tri = jnp.asarray(np.tril(np.ones((c, c), np.float32)))
    eye = jnp.eye(LANE, dtype=F32)
    nk = CONV_W - 1
    sh_s = (1, 2 * hps, HC_DK, HC_DV)
    cst = lambda shape: pl.BlockSpec(shape, lambda h, j, *_: (0,) * len(shape))
    grid_spec = pltpu.PrefetchScalarGridSpec(
        num_scalar_prefetch=3,
        grid=(n_hg, n_jobs),
        in_specs=[pl.BlockSpec((c, wq), lambda h, j, *_: (j, h)),
                  pl.BlockSpec((c, wz), lambda h, j, *_: (j, qkv_w // wz + h)),
                  pl.BlockSpec((c, LANE), lambda h, j, *_: (j, (qkv_w + z_w) // LANE)),
                  pl.BlockSpec((CONV_W, wq), lambda h, j, *_: (0, h)),
                  cst((1, LANE)), cst((1, LANE)), cst((1, HC_DV)),
                  cst((c, c)), cst((LANE, LANE)),
                  pl.BlockSpec(sh_s, lambda h, j, s, f, i: (i[j], h, 0, 0)),
                  pl.BlockSpec((1, nk, wq), lambda h, j, s, f, i: (i[j], 0, h))],
        out_specs=[pl.BlockSpec((c, wz), lambda h, j, *_: (j, h)),
                   pl.BlockSpec(sh_s, lambda h, j, s, f, i: (s[j], h, 0, 0)),
                   pl.BlockSpec((1, nk, wq), lambda h, j, s, f, i: (s[j], 0, h))],
        scratch_shapes=[pltpu.VMEM((8 + c, wq), F32),
                        pltpu.VMEM(sh_s, F32),
                        pltpu.VMEM((nk, wq), F32)],
    )
    out_shape = [jax.ShapeDtypeStruct((p.shape[0], z_w), BF16),
                 jax.ShapeDtypeStruct((n_slots, HC_V_HEADS, HC_DK, HC_DV), F32),
                 jax.ShapeDtypeStruct((n_slots, nk, qkv_w), F32)]
    return pl.pallas_call(
        functools.partial(_gdn_kernel, n_sq=n_sq, hps=hps),
        grid_spec=grid_spec,
        out_shape=out_shape,
        compiler_params=pltpu.CompilerParams(dimension_semantics=("arbitrary", "arbitrary"),
                                             vmem_limit_bytes=VMEM_LIMIT),
        name="gdn_mixer",
    )(slot, fs, insl, p, p, p, conv_w, alog_row, dtb_row, gnw, tri, eye, s_in, cv_in)


def _gdn_perm():
    nq = HC_QK_HEADS * HC_DK
    idx = []
    for hp in range(HC_QK_HEADS):
        idx.append(np.arange(hp * HC_DK, (hp + 1) * HC_DK))
        idx.append(nq + np.arange(hp * HC_DK, (hp + 1) * HC_DK))
        idx.append(2 * nq + np.arange(2 * hp * HC_DV, (2 * hp + 2) * HC_DV))
    return np.concatenate(idx)


def _pad_cols(w, width):
    return jnp.pad(w, ((0, 0), (0, width - w.shape[1])))


def kernel(x_prompt, x_sample, state_hgrn_S, state_mlstm_C, state_mlstm_n, state_mlstm_m, state_gdn_S,
           state_gdn_conv, meta_tokens, norm_mix, norm_ffn, norm_final, even_w_in, even_w_out,
           hgrn_lb_logits, hgrn_norm, mlstm_b_i, mlstm_b_f, mlstm_norm, odd_w_in, odd_conv_w, gdn_a_log,
           gdn_dt_bias, gdn_norm, odd_w_out, ffn_w_in, ffn_w_out):
    n_b, t_len, d = x_prompt.shape
    n_s, l_s, _ = x_sample.shape
    c = CHUNK
    assert t_len % c == 0 and l_s % c == 0 and meta_tokens.shape[0] == N_META
    assert norm_mix.shape[0] == 2, "one even (HGRN2+mLSTM) and one odd (DeltaNet) layer"
    dt = x_prompt.dtype

    x = jnp.concatenate([jnp.zeros((c - N_META, d), dt), meta_tokens.astype(dt),
                         x_prompt.reshape(n_b * t_len, d), x_sample.reshape(n_s * l_s, d)], axis=0)
    tables = _job_tables(n_b, t_len // c, n_s, l_s // c)
    n_slots = tables[3]
    row = lambda v: v.reshape(1, -1).astype(F32)

    e = 0
    n_main = 4 * HA_HEADS * HA_DK + 2 * HB_HEADS * HB_DQK + 2 * HB_HEADS * HB_DV
    w_even = jnp.concatenate([even_w_in[e][:, :n_main], _pad_cols(even_w_in[e][:, n_main:], LANE)],
                             axis=1).astype(BF16)
    p_even = _proj(x, row(norm_mix[0]), w_even, "proj_even")
    gbias = _pad_cols(jnp.concatenate([mlstm_b_i[e], mlstm_b_f[e]]).reshape(1, -1).astype(F32), LANE)
    hs_in = jnp.swapaxes(state_hgrn_S[e].astype(F32), -1, -2)
    mc_in = jnp.swapaxes(state_mlstm_C[e].astype(F32), -1, -2)
    mn_in = state_mlstm_n[e].astype(F32)[:, :, None, :]
    mm_in = jnp.pad(state_mlstm_m[e].astype(F32), ((0, 0), (HB_HEADS, LANE - 2 * HB_HEADS)))[:, None, :]
    mrg0, hs_o, mc_o, mn_o, mm_o = _even_mixer(
        p_even, tables, hgrn_lb_logits.astype(F32), e, row(hgrn_norm[e]), row(mlstm_norm[e]), gbias,
        hs_in, mc_in, mn_in, mm_in)
    x = _post(x, mrg0, even_w_out[e].astype(BF16), row(norm_ffn[0]), ffn_w_in[0].astype(BF16),
              ffn_w_out[0].astype(BF16), row(norm_final), False, "post_even")

    o = 0
    perm = _gdn_perm()
    n_qkv = perm.shape[0]
    n_z = HC_V_HEADS * HC_DV
    w_odd = jnp.concatenate([odd_w_in[o][:, perm], odd_w_in[o][:, n_qkv:n_qkv + n_z],
                             _pad_cols(odd_w_in[o][:, n_qkv + n_z:], LANE)], axis=1).astype(BF16)
    p_odd = _proj(x, row(norm_mix[1]), w_odd, "proj_odd")
    lane_pad = lambda v: jnp.pad(v.reshape(1, -1).astype(F32),
                                 ((0, 0), (HC_V_HEADS, LANE - 2 * HC_V_HEADS)))
    mrg1, gs_o, cv_o = _gdn_mixer(
        p_odd, tables, odd_conv_w[o].astype(F32)[:, perm], lane_pad(gdn_a_log[o]), lane_pad(gdn_dt_bias[o]),
        row(gdn_norm[o]), state_gdn_S[o].astype(F32), state_gdn_conv[o].astype(F32)[:, :, perm])
    y = _post(x, mrg1, odd_w_out[o].astype(BF16), row(norm_ffn[1]), ffn_w_in[1].astype(BF16),
              ffn_w_out[1].astype(BF16), row(norm_final), True, "post_odd")

    y_prompt = y[c:c + n_b * t_len].reshape(n_b, t_len, d)
    y_sample = y[c + n_b * t_len:].reshape(n_s, l_s, d)
    inv_perm = np.argsort(perm)
    hs_o = jnp.swapaxes(hs_o, -1, -2)
    mc_o = jnp.swapaxes(mc_o, -1, -2)
    mn_o = mn_o[:, :, 0, :]
    mm_o = mm_o[:, 0, HB_HEADS:2 * HB_HEADS]
    cv_o = cv_o[:, :, inv_perm]
    stk = lambda a, lo, hi: a[lo:hi][None].astype(dt)
    p_sl, s_sl = (0, n_b), (n_b, n_b + n_s)
    outs = [y_prompt, y_sample]
    for lo, hi in (p_sl, s_sl):
        outs += [stk(hs_o, lo, hi), stk(mc_o, lo, hi), stk(mn_o, lo, hi), stk(mm_o, lo, hi),
                 stk(gs_o, lo, hi), stk(cv_o, lo, hi)]
    return tuple(outs)
```

```python
import functools

import numpy as np
import jax
import jax.numpy as jnp
from jax import lax
from jax.experimental import pallas as pl
from jax.experimental.pallas import tpu as pltpu

F32 = jnp.float32
BF16 = jnp.bfloat16
HI = lax.Precision.HIGHEST

CHUNK = 64
N_META = 16
NORM_EPS = 1e-6
GATE_CAP = 15.0
NEG_BIG = -1e30

HA_HEADS, HA_DK, HA_DV = 4, 128, 128
HB_HEADS, HB_DQK, HB_DV = 4, 64, 128
HC_QK_HEADS, HC_V_HEADS, HC_DK, HC_DV = 8, 16, 128, 128
CONV_W = 4
LANE = 128
VMEM_LIMIT = 56 * 1024 * 1024


def _nn(a, b, prec=None):
    return lax.dot_general(a, b, (((1,), (0,)), ((), ())), precision=prec,
                           preferred_element_type=F32)


def _nt(a, b, prec=None):
    return lax.dot_general(a, b, (((1,), (1,)), ((), ())), precision=prec,
                           preferred_element_type=F32)


def _tn(a, b, prec=None):
    return lax.dot_general(a, b, (((0,), (0,)), ((), ())), precision=prec,
                           preferred_element_type=F32)


def _dot_pieces(dot, exact, x, n):
    e16 = exact.astype(BF16)
    acc = None
    for _ in range(n):
        piece = x.astype(BF16)
        part = dot(e16, piece)
        acc = part if acc is None else acc + part
        x = x - piece.astype(F32)
    return acc


def _sigmoid(x):
    return 1.0 / (1.0 + jnp.exp(-x))


def _silu(x):
    return x * _sigmoid(x)


def _softplus(x):
    return jnp.maximum(x, 0.0) + jnp.log1p(jnp.exp(-jnp.abs(x)))


def _log_sigmoid(x):
    return jnp.minimum(x, 0.0) - jnp.log1p(jnp.exp(-jnp.abs(x)))


def _rms(x, w):
    ms = jnp.mean(x * x, axis=-1, keepdims=True)
    return x * lax.rsqrt(ms + NORM_EPS) * w


def _proj_kernel(x_ref, nw_ref, w_ref, o_ref, *, n_chunk):
    h = _rms(x_ref[...], nw_ref[...]).astype(BF16)
    n_total = w_ref.shape[1]
    for n0 in range(0, n_total, n_chunk):
        n1 = min(n0 + n_chunk, n_total)
        o_ref[:, n0:n1] = _nn(h, w_ref[:, n0:n1])


def _post_kernel(x_ref, m_ref, wo_ref, nw_ref, wi_ref, w2_ref, nf_ref, o_ref, act_ref,
                 *, hidden, h_chunk, final_norm):
    x1 = x_ref[...] + _nn(m_ref[...], wo_ref[...])
    h = _rms(x1, nw_ref[...]).astype(BF16)
    for c0 in range(0, hidden, h_chunk):
        c1 = min(c0 + h_chunk, hidden)
        gate = _nn(h, wi_ref[:, c0:c1])
        up = _nn(h, wi_ref[:, hidden + c0:hidden + c1])
        act_ref[:, c0:c1] = (_silu(gate) * up).astype(BF16)
    x2 = x1 + _nn(act_ref[...], w2_ref[...])
    if final_norm:
        x2 = _rms(x2, nf_ref[...])
    o_ref[...] = x2


def _pick_tm(m_rows, cap=512):
    best = CHUNK
    k = 1
    while CHUNK * k <= cap:
        if m_rows % (CHUNK * k) == 0:
            best = CHUNK * k
        k += 1
    return best


def _const_spec(shape):
    nd = len(shape)
    return pl.BlockSpec(shape, lambda i, _n=nd: (0,) * _n)


def _proj(x, nw, w, name):
    m_rows, d = x.shape
    n = w.shape[1]
    tm = _pick_tm(m_rows)
    return pl.pallas_call(
        functools.partial(_proj_kernel, n_chunk=512),
        grid=(m_rows // tm,),
        in_specs=[pl.BlockSpec((tm, d), lambda i: (i, 0)),
                  _const_spec((1, d)),
                  _const_spec((d, n))],
        out_specs=pl.BlockSpec((tm, n), lambda i: (i, 0)),
        out_shape=jax.ShapeDtypeStruct((m_rows, n), F32),
        compiler_params=pltpu.CompilerParams(dimension_semantics=("arbitrary",),
                                             vmem_limit_bytes=VMEM_LIMIT),
        name=name,
    )(x, nw, w)


def _post(x, mrg, wo, nw, wi, w2, nf, final_norm, name):
    m_rows, d = x.shape
    km = mrg.shape[1]
    hidden = w2.shape[0]
    tm = _pick_tm(m_rows)
    return pl.pallas_call(
        functools.partial(_post_kernel, hidden=hidden, h_chunk=256, final_norm=final_norm),
        grid=(m_rows // tm,),
        in_specs=[pl.BlockSpec((tm, d), lambda i: (i, 0)),
                  pl.BlockSpec((tm, km), lambda i: (i, 0)),
                  _const_spec((km, d)),
                  _const_spec((1, d)),
                  _const_spec((d, 2 * hidden)),
                  _const_spec((hidden, d)),
                  _const_spec((1, d))],
        out_specs=pl.BlockSpec((tm, d), lambda i: (i, 0)),
        out_shape=jax.ShapeDtypeStruct((m_rows, d), F32),
        scratch_shapes=[pltpu.VMEM((tm, hidden), BF16)],
        compiler_params=pltpu.CompilerParams(dimension_semantics=("arbitrary",),
                                             vmem_limit_bytes=VMEM_LIMIT),
        name=name,
    )(x, mrg, wo, nw, wi, w2, nf)


def _job_tables(n_prompt, chunks_prompt, n_sample, chunks_sample):
    n_slots = n_prompt + n_sample + 1
    slot, fs, insl = [n_slots - 1], [1], [0]
    for b in range(n_prompt):
        for c in range(chunks_prompt):
            slot.append(b); fs.append(2 if c == 0 else 0); insl.append(0)
    for s in range(n_sample):
        for c in range(chunks_sample):
            slot.append(n_prompt + s); fs.append(3 if c == 0 else 0); insl.append(s)
    to = lambda v: jnp.asarray(np.asarray(v, np.int32))
    return to(slot), to(fs), to(insl), n_slots


def _hgrn_tables(c):
    t = np.arange(c)[:, None]
    j = np.arange(c)[None, :]
    mats, masks = [], []
    m = c // 2
    while m >= 1:
        blk = t // (2 * m)
        bd = blk * 2 * m + m - 1
        second = (t % (2 * m)) >= m
        mat = np.where(second, (j > bd) & (j <= t), (j > t) & (j <= bd)).astype(np.float32)
        mats.append(mat)
        masks.append((blk == (j // (2 * m))).astype(np.float32))
        m //= 2
    mats.append((j <= t).astype(np.float32))
    masks.append((j == t).astype(np.float32))
    return (jnp.asarray(np.concatenate(mats, 0)), jnp.asarray(np.stack(masks, 0)),
            len(mats) - 1)


def _even_kernel(slot_ref, fs_ref, insl_ref,
                 pa_ref, qb_ref, kb_ref, vb_ref, ob_ref, gt_ref,
                 lbl_ref, hn_ref, mnw_ref, gbias_ref,
                 lvl_ref, lmask_ref, tri_ref, eye_ref,
                 hs_in, mc_in, mn_in, mm_in,
                 mrg_ref, hs_out, mc_out, mn_out, mm_out,
                 meta_hs, meta_mc, meta_mn, meta_mm,
                 *, lb_index, n_levels):
    c = CHUNK
    j = pl.program_id(0)
    fs = fs_ref[j]

    @pl.when(fs == 1)
    def _():
        hs_out[...] = jnp.zeros_like(hs_out)
        mc_out[...] = jnp.zeros_like(mc_out)
        mn_out[...] = jnp.zeros_like(mn_out)
        mm_out[...] = jnp.zeros_like(mm_out)

    @pl.when(fs == 2)
    def _():
        hs_out[...] = meta_hs[...]
        mc_out[...] = meta_mc[...]
        mn_out[...] = meta_mn[...]
        mm_out[...] = meta_mm[...]

    @pl.when(fs == 3)
    def _():
        hs_out[...] = hs_in[...]
        mc_out[...] = mc_in[...]
        mn_out[...] = mn_in[...]
        mm_out[...] = mm_in[...]

    hk = HA_HEADS * HA_DK
    lbl = lbl_ref[...]
    ex = jnp.exp(lbl - jnp.max(lbl, axis=0, keepdims=True))
    sm = ex / jnp.sum(ex, axis=0, keepdims=True)
    lb = jnp.sum(sm[:lb_index + 1], axis=0, keepdims=True)
    fa = pa_ref[:, hk:2 * hk]
    log_f = jnp.log(lb + (1.0 - lb) * _sigmoid(fa))
    k_all = (1.0 - lb) * _sigmoid(-fa)
    q_all = _silu(pa_ref[:, 0:hk])
    e_all = _dot_pieces(_nn, lvl_ref[...], log_f, 2)

    lane = lax.broadcasted_iota(jnp.int32, (c, LANE), 1)
    rowg = lax.broadcasted_iota(jnp.int32, (c, LANE), 0)
    gcap = GATE_CAP * jnp.tanh((gt_ref[...] + gbias_ref[...]) / GATE_CAP)
    is_f = (lane >= HB_HEADS) & (lane < 2 * HB_HEADS)
    pad = rowg < jnp.where(j == 0, c - N_META, 0)
    lf_t = jnp.where(is_f & jnp.logical_not(pad), _log_sigmoid(gcap), 0.0)
    ig_t = jnp.where(pad, NEG_BIG, gcap)
    fc = _dot_pieces(_nn, tri_ref[...], lf_t, 3)
    fr = _dot_pieces(_nt, eye_ref[...], fc, 3)
    ir = _dot_pieces(_nt, eye_ref[...], ig_t, 3)

    row = lax.broadcasted_iota(jnp.int32, (c, HA_DK), 0)
    hsl = [slice(h * HA_DK, (h + 1) * HA_DK) for h in range(HA_HEADS)]
    q_h = [q_all[:, s] for s in hsl]
    k_h = [k_all[:, s] for s in hsl]
    v16 = [pa_ref[:, 2 * hk + h * HA_DV:2 * hk + (h + 1) * HA_DV].astype(BF16) for h in range(HA_HEADS)]
    b_h = [e_all[n_levels * c:(n_levels + 1) * c, s] for s in hsl]
    st = [hs_out[0, h] for h in range(HA_HEADS)]
    o_inter = [_nt((q_h[h] * jnp.exp(b_h[h])).astype(BF16), st[h].astype(BF16)) for h in range(HA_HEADS)]
    for h in range(HA_HEADS):
        b_last = b_h[h][c - 1:c, :]
        k_til = k_h[h] * jnp.exp(b_last - b_h[h])
        hs_out[0, h] = st[h] * jnp.exp(b_last) + _tn(v16[h], k_til.astype(BF16))
    scores = [lmask_ref[n_levels] * _nt(q_h[h].astype(BF16), k_h[h].astype(BF16)) for h in range(HA_HEADS)]
    m = c // 2
    for lvl in range(n_levels):
        second = (row & m) != 0
        for h in range(HA_HEADS):
            dec = jnp.exp(e_all[lvl * c:(lvl + 1) * c, hsl[h]])
            qs = jnp.where(second, q_h[h] * dec, 0.0).astype(BF16)
            ks = jnp.where(second, 0.0, k_h[h] * dec).astype(BF16)
            part = _nt(qs, ks)
            scores[h] = scores[h] + (part if lvl == 0 else lmask_ref[lvl] * part)
        m //= 2

    r_i = lax.broadcasted_iota(jnp.int32, (c, c), 0)
    c_i = lax.broadcasted_iota(jnp.int32, (c, c), 1)
    causal = c_i <= r_i
    m_row = mm_out[0]
    m_row_new = m_row
    lane1 = lax.broadcasted_iota(jnp.int32, (1, LANE), 1)
    mq, mk, mv16, qkd, inter, m_ts, w_inters = [], [], [], [], [], [], []
    for h in range(HB_HEADS):
        mq.append(qb_ref[:, h * HB_DQK:(h + 1) * HB_DQK])
        mk.append(kb_ref[:, h * HB_DQK:(h + 1) * HB_DQK] * (HB_DQK ** -0.5))
        mv16.append(vb_ref[:, h * HB_DV:(h + 1) * HB_DV].astype(BF16))
    ct = [mc_out[0, h] for h in range(HB_HEADS)]
    n_row = [mn_out[0, h] for h in range(HB_HEADS)]
    qk_raw = [_nt(mq[h].astype(BF16), mk[h].astype(BF16)) for h in range(HB_HEADS)]
    inter = [_nt(mq[h].astype(BF16), ct[h].astype(BF16)) for h in range(HB_HEADS)]
    for h in range(HB_HEADS):
        fl = HB_HEADS + h
        fc_h = fc[:, fl:fl + 1]
        fr_h = fr[fl:fl + 1, :]
        igr_h = ir[h:h + 1, :]
        igc_h = ig_t[:, h:h + 1]
        m_prev = m_row[:, fl:fl + 1]
        log_d = jnp.where(causal, fc_h - fr_h + igr_h, -jnp.inf)
        log_inter = fc_h + m_prev
        m_t = jnp.maximum(log_inter, jnp.max(log_d, axis=1, keepdims=True))
        m_ts.append(m_t)
        w_inters.append(jnp.exp(log_inter - m_t))
        qkd.append(qk_raw[h] * jnp.exp(log_d - m_t))
        f_last = fc_h[c - 1:c, :]
        m_new = m_t[c - 1:c, :]
        w_s = jnp.exp(f_last - fc_h + igc_h - m_new)
        decay = jnp.exp(f_last + m_prev - m_new)
        kw = mk[h] * w_s
        mc_out[0, h] = decay * ct[h] + _tn(mv16[h], kw.astype(BF16))
        mn_out[0, h] = decay * n_row[h] + jnp.sum(kw, axis=0, keepdims=True)
        m_row_new = jnp.where(lane1 == fl, m_new, m_row_new)
    mm_out[0] = m_row_new

    o_a = [_nn(scores[h].astype(BF16), v16[h]) + o_inter[h] for h in range(HA_HEADS)]
    num = [_nn(qkd[h].astype(BF16), mv16[h]) + w_inters[h] * inter[h] for h in range(HB_HEADS)]
    for h in range(HA_HEADS):
        ga = pa_ref[:, 3 * hk + h * HA_DV:3 * hk + (h + 1) * HA_DV]
        mrg_ref[:, h * HA_DV:(h + 1) * HA_DV] = (_rms(o_a[h], hn_ref[:, hsl[h]]) * _silu(ga)).astype(BF16)
    for h in range(HB_HEADS):
        den = jnp.sum(qkd[h], axis=1, keepdims=True) \
            + w_inters[h] * jnp.sum(mq[h] * n_row[h], axis=1, keepdims=True)
        hh = num[h] / jnp.maximum(jnp.abs(den), jnp.exp(-m_ts[h]))
        ob = ob_ref[:, h * HB_DV:(h + 1) * HB_DV]
        col0 = HA_HEADS * HA_DV + h * HB_DV
        mrg_ref[:, col0:col0 + HB_DV] = (
            _rms(hh, mnw_ref[:, h * HB_DV:(h + 1) * HB_DV]) * _sigmoid(ob)).astype(BF16)

    @pl.when(j == 0)
    def _():
        meta_hs[...] = hs_out[...]
        meta_mc[...] = mc_out[...]
        meta_mn[...] = mn_out[...]
        meta_mm[...] = mm_out[...]


def _even_mixer(p, tables, lb_logits, lb_index, hgrn_norm, mlstm_norm, gbias, hs_in, mc_in, mn_in, mm_in):
    slot, fs, insl, n_slots = tables
    c = CHUNK
    n_jobs = p.shape[0] // c
    lvl, lmask, n_levels = _hgrn_tables(c)
    tri = jnp.asarray(np.tril(np.ones((c, c), np.float32)))
    eye = jnp.eye(LANE, dtype=F32)
    hk = HA_HEADS * HA_DK
    wa = 4 * hk
    qk_w = HB_HEADS * HB_DQK
    v_w = HB_HEADS * HB_DV
    g_col = wa + 2 * qk_w + 2 * v_w
    cst = lambda shape: pl.BlockSpec(shape, lambda j, *_: (0,) * len(shape))
    row_blk = lambda w, idx: pl.BlockSpec((c, w), lambda j, *_: (j, idx))
    st_in = lambda shape: pl.BlockSpec((1,) + shape, lambda j, s, f, i: (i[j],) + (0,) * len(shape))
    st_out = lambda shape: pl.BlockSpec((1,) + shape, lambda j, s, f, i: (s[j],) + (0,) * len(shape))
    sh_hs = (HA_HEADS, HA_DV, HA_DK)
    sh_mc = (HB_HEADS, HB_DV, HB_DQK)
    sh_mn = (HB_HEADS, 1, HB_DQK)
    sh_mm = (1, LANE)
    grid_spec = pltpu.PrefetchScalarGridSpec(
        num_scalar_prefetch=3,
        grid=(n_jobs,),
        in_specs=[row_blk(wa, 0),
                  row_blk(qk_w, wa // qk_w), row_blk(qk_w, wa // qk_w + 1),
                  row_blk(v_w, (wa + 2 * qk_w) // v_w), row_blk(v_w, (wa + 2 * qk_w) // v_w + 1),
                  row_blk(LANE, g_col // LANE),
                  cst(lb_logits.shape), cst((1, hk)), cst((1, v_w)), cst((1, LANE)),
                  cst(lvl.shape), cst(lmask.shape), cst((c, c)), cst((LANE, LANE)),
                  st_in(sh_hs), st_in(sh_mc), st_in(sh_mn), st_in(sh_mm)],
        out_specs=[pl.BlockSpec((c, hk + v_w), lambda j, *_: (j, 0)),
                   st_out(sh_hs), st_out(sh_mc), st_out(sh_mn), st_out(sh_mm)],
        scratch_shapes=[pltpu.VMEM((1,) + sh_hs, F32), pltpu.VMEM((1,) + sh_mc, F32),
                        pltpu.VMEM((1,) + sh_mn, F32), pltpu.VMEM((1,) + sh_mm, F32)],
    )
    out_shape = [jax.ShapeDtypeStruct((p.shape[0], hk + v_w), BF16),
                 jax.ShapeDtypeStruct((n_slots,) + sh_hs, F32),
                 jax.ShapeDtypeStruct((n_slots,) + sh_mc, F32),
                 jax.ShapeDtypeStruct((n_slots,) + sh_mn, F32),
                 jax.ShapeDtypeStruct((n_slots,) + sh_mm, F32)]
    return pl.pallas_call(
        functools.partial(_even_kernel, lb_index=lb_index, n_levels=n_levels),
        grid_spec=grid_spec,
        out_shape=out_shape,
        compiler_params=pltpu.CompilerParams(dimension_semantics=("arbitrary",),
                                             vmem_limit_bytes=VMEM_LIMIT),
        name="even_mixer",
    )(slot, fs, insl, p, p, p, p, p, p, lb_logits, hgrn_norm, mlstm_norm, gbias,
      lvl, lmask, tri, eye, hs_in, mc_in, mn_in, mm_in)


def _gdn_kernel(slot_ref, fs_ref, insl_ref,
                qkv_ref, z_ref, gt_ref, cw_ref, alog_ref, dtb_ref, gnw_ref,
                tri_ref, eye_ref, s_in, cv_in,
                mrg_ref, s_out, cv_out,
                xpad, meta_s, meta_cv, *, n_sq, hps):
    c = CHUNK
    hg = pl.program_id(0)
    j = pl.program_id(1)
    fs = fs_ref[j]
    nk = CONV_W - 1
    base = 8 - nk

    @pl.when(fs == 1)
    def _():
        s_out[...] = jnp.zeros_like(s_out)
        xpad[base:8, :] = jnp.zeros((nk, xpad.shape[1]), F32)

    @pl.when(fs == 2)
    def _():
        s_out[...] = meta_s[...]
        xpad[base:8, :] = meta_cv[...]

    @pl.when(fs == 3)
    def _():
        s_out[...] = s_in[...]
        xpad[base:8, :] = cv_in[0]

    xpad[8:8 + c, :] = qkv_ref[...]
    xp = xpad[...]
    y = cw_ref[nk:CONV_W, :] * xp[8:8 + c]
    for t in range(nk):
        y = y + cw_ref[t:t + 1, :] * pltpu.roll(xp, nk - t, 0)[8:8 + c]
    carry = xpad[base + c:8 + c, :]
    cv_out[0] = carry
    xpad[base:8, :] = carry
    act = _silu(y)

    gt = gt_ref[...]
    beta_t = _sigmoid(gt)
    g_t = -jnp.exp(alog_ref[...]) * _softplus(gt + dtb_ref[...])
    gc = _dot_pieces(_nn, tri_ref[...], g_t, 3)
    lane = lax.broadcasted_iota(jnp.int32, (c, LANE), 1)
    row = lax.broadcasted_iota(jnp.int32, (c, LANE), 0)
    first = lane < c
    s_idx = jnp.where(first, lane, lane - c)
    causal = s_idx <= row
    strict = s_idx < row
    diag = s_idx == row
    eye_c = diag.astype(F32)
    r2 = lax.broadcasted_iota(jnp.int32, (2 * c, LANE), 0)
    l2 = lax.broadcasted_iota(jnp.int32, (2 * c, LANE), 1)
    same_head = (r2 < c) == (l2 < c)
    wq = 4 * HC_DK

    def col(tile, idx):
        return jnp.sum(jnp.where(lane == idx, tile, 0.0), axis=1, keepdims=True)

    def block_diag(m16):
        return jnp.where(same_head, jnp.concatenate([m16, m16], axis=0), jnp.zeros_like(m16[:1, :1]))

    pairs = range(hps)
    qn, kn, qk, kk = [], [], [], []
    for p in pairs:
        q = act[:, p * wq:p * wq + HC_DK]
        k = act[:, p * wq + HC_DK:p * wq + 2 * HC_DK]
        qn.append(q * lax.rsqrt(jnp.sum(q * q, axis=-1, keepdims=True) + NORM_EPS) * (HC_DK ** -0.5))
        kn.append(k * lax.rsqrt(jnp.sum(k * k, axis=-1, keepdims=True) + NORM_EPS))
    for p in pairs:
        kn16 = kn[p].astype(BF16)
        k2 = jnp.concatenate([kn16, kn16], axis=0)
        qk.append(_nt(qn[p].astype(BF16), k2))
        kk.append(_nt(kn16, k2))

    beta, gcol, decay, inv, pw = [], [], [], [], []
    for p in pairs:
        hv = 2 * (hg * hps + p)
        b_a, b_b = col(beta_t, hv), col(beta_t, hv + 1)
        g_a, g_b = col(gc, HC_V_HEADS + hv), col(gc, HC_V_HEADS + hv + 1)
        beta.append((b_a, b_b))
        gcol.append((g_a, g_b))
        g_c = jnp.where(first, g_a, g_b)
        g_r = jnp.sum(jnp.where(diag, g_c, 0.0), axis=0, keepdims=True)
        decay.append(jnp.exp(jnp.where(causal, g_c - g_r, -jnp.inf)))
        low = jnp.where(strict, jnp.where(first, b_a, b_b) * kk[p] * decay[p], 0.0)
        inv.append(eye_c - low)
        pw.append(low.astype(BF16))
    pw = [_nn(x, block_diag(x)).astype(BF16) for x in pw]
    for i in range(n_sq):
        if i < n_sq - 1:
            prod = [_nn(jnp.concatenate([x.astype(BF16), y], axis=0), block_diag(y)) for x, y in zip(inv, pw)]
            inv = [x + r[:c] for x, r in zip(inv, prod)]
            pw = [r[c:].astype(BF16) for r in prod]
        else:
            inv = [x + _nn(x.astype(BF16), block_diag(y)) for x, y in zip(inv, pw)]

    sol, eg = [], []
    for p in pairs:
        (b_a, b_b), (g_a, g_b) = beta[p], gcol[p]
        e_a, e_b = jnp.exp(g_a), jnp.exp(g_b)
        eg.append((e_a, e_b))
        v_a = act[:, p * wq + 2 * HC_DK:p * wq + 3 * HC_DK]
        v_b = act[:, p * wq + 3 * HC_DK:p * wq + 4 * HC_DK]
        rhs = jnp.concatenate([jnp.concatenate([v_a * b_a, kn[p] * (b_a * e_a)], axis=1),
                               jnp.concatenate([v_b * b_b, kn[p] * (b_b * e_b)], axis=1)],
                              axis=0).astype(BF16)
        sol.append((_nn(jnp.where(first, inv[p], 0.0).astype(BF16), rhs),
                    _nn(jnp.where(first, 0.0, inv[p]).astype(BF16), rhs)))
    s_old = [s_out[0, n] for n in range(2 * hps)]
    ws = []
    for p in pairs:
        for jj in range(2):
            n = 2 * p + jj
            lhs = jnp.concatenate([sol[p][jj][:, HC_DV:], qn[p] * eg[p][jj]], axis=0)
            ws.append(_nn(lhs.astype(BF16), s_old[n].astype(BF16)))
    for p in pairs:
        v_new = [sol[p][jj][:, :HC_DV] - ws[2 * p + jj][:c] for jj in range(2)]
        v_st = jnp.concatenate(v_new, axis=0).astype(BF16)
        attn = qk[p] * decay[p]
        o = (ws[2 * p][c:] + _nn(jnp.where(first, attn, 0.0).astype(BF16), v_st),
             ws[2 * p + 1][c:] + _nn(jnp.where(first, 0.0, attn).astype(BF16), v_st))
        for jj in range(2):
            n = 2 * p + jj
            g_h = gcol[p][jj]
            g_last = g_h[c - 1:c, :]
            k_til = kn[p] * jnp.exp(g_last - g_h)
            s_out[0, n] = jnp.exp(g_last) * s_old[n] + _tn(k_til.astype(BF16), v_new[jj].astype(BF16))
            z = z_ref[:, n * HC_DV:(n + 1) * HC_DV]
            mrg_ref[:, n * HC_DV:(n + 1) * HC_DV] = (_rms(o[jj], gnw_ref[...]) * _silu(z)).astype(BF16)

    @pl.when(j == 0)
    def _():
        meta_s[...] = s_out[...]
        meta_cv[...] = xpad[base:8, :]


def _gdn_mixer(p, tables, conv_w, alog_row, dtb_row, gnw, s_in, cv_in, hps=8):
    slot, fs, insl, n_slots = tables
    c = CHUNK
    n_jobs = p.shape[0] // c
    n_hg = HC_QK_HEADS // hps
    wq = hps * 4 * HC_DK
    wz = hps * 2 * HC_DV
    qkv_w = HC_QK_HEADS * 4 * HC_DK
    z_w = HC_V_HEADS * HC_DV
    n_sq = int(np.log2(c)) - 1
    tri = jnp.asarray(np.tril(np.ones((c, c), np.float32)))
    eye = jnp.eye(LANE, dtype=F32)
    nk = CONV_W - 1
    sh_s = (1, 2 * hps, HC_DK, HC_DV)
    cst = lambda shape: pl.BlockSpec(shape, lambda h, j, *_: (0,) * len(shape))
    grid_spec = pltpu.PrefetchScalarGridSpec(
        num_scalar_prefetch=3,
        grid=(n_hg, n_jobs),
        in_specs=[pl.BlockSpec((c, wq), lambda h, j, *_: (j, h)),
                  pl.BlockSpec((c, wz), lambda h, j, *_: (j, qkv_w // wz + h)),
                  pl.BlockSpec((c, LANE), lambda h, j, *_: (j, (qkv_w + z_w) // LANE)),
                  pl.BlockSpec((CONV_W, wq), lambda h, j, *_: (0, h)),
                  cst((1, LANE)), cst((1, LANE)), cst((1, HC_DV)),
                  cst((c, c)), cst((LANE, LANE)),
                  pl.BlockSpec(sh_s, lambda h, j, s, f, i: (i[j], h, 0, 0)),
                  pl.BlockSpec((1, nk, wq), lambda h, j, s, f, i: (i[j], 0, h))],
        out_specs=[pl.BlockSpec((c, wz), lambda h, j, *_: (j, h)),
                   pl.BlockSpec(sh_s, lambda h, j, s, f, i: (s[j], h, 0, 0)),
                   pl.BlockSpec((1, nk, wq), lambda h, j, s, f, i: (s[j], 0, h))],
        scratch_shapes=[pltpu.VMEM((8 + c, wq), F32),
                        pltpu.VMEM(sh_s, F32),
                        pltpu.VMEM((nk, wq), F32)],
    )
    out_shape = [jax.ShapeDtypeStruct((p.shape[0], z_w), BF16),
                 jax.ShapeDtypeStruct((n_slots, HC_V_HEADS, HC_DK, HC_DV), F32),
                 jax.ShapeDtypeStruct((n_slots, nk, qkv_w), F32)]
    return pl.pallas_call(
        functools.partial(_gdn_kernel, n_sq=n_sq, hps=hps),
        grid_spec=grid_spec,
        out_shape=out_shape,
        compiler_params=pltpu.CompilerParams(dimension_semantics=("arbitrary", "arbitrary"),
                                             vmem_limit_bytes=VMEM_LIMIT),
        name="gdn_mixer",
    )(slot, fs, insl, p, p, p, conv_w, alog_row, dtb_row, gnw, tri, eye, s_in, cv_in)


def _gdn_perm():
    nq = HC_QK_HEADS * HC_DK
    idx = []
    for hp in range(HC_QK_HEADS):
        idx.append(np.arange(hp * HC_DK, (hp + 1) * HC_DK))
        idx.append(nq + np.arange(hp * HC_DK, (hp + 1) * HC_DK))
        idx.append(2 * nq + np.arange(2 * hp * HC_DV, (2 * hp + 2) * HC_DV))
    return np.concatenate(idx)


def _pad_cols(w, width):
    return jnp.pad(w, ((0, 0), (0, width - w.shape[1])))


def kernel(x_prompt, x_sample, state_hgrn_S, state_mlstm_C, state_mlstm_n, state_mlstm_m, state_gdn_S,
           state_gdn_conv, meta_tokens, norm_mix, norm_ffn, norm_final, even_w_in, even_w_out,
           hgrn_lb_logits, hgrn_norm, mlstm_b_i, mlstm_b_f, mlstm_norm, odd_w_in, odd_conv_w, gdn_a_log,
           gdn_dt_bias, gdn_norm, odd_w_out, ffn_w_in, ffn_w_out):
    n_b, t_len, d = x_prompt.shape
    n_s, l_s, _ = x_sample.shape
    c = CHUNK
    assert t_len % c == 0 and l_s % c == 0 and meta_tokens.shape[0] == N_META
    assert norm_mix.shape[0] == 2, "one even (HGRN2+mLSTM) and one odd (DeltaNet) layer"
    dt = x_prompt.dtype

    x = jnp.concatenate([jnp.zeros((c - N_META, d), dt), meta_tokens.astype(dt),
                         x_prompt.reshape(n_b * t_len, d), x_sample.reshape(n_s * l_s, d)], axis=0)
    tables = _job_tables(n_b, t_len // c, n_s, l_s // c)
    n_slots = tables[3]
    row = lambda v: v.reshape(1, -1).astype(F32)

    e = 0
    n_main = 4 * HA_HEADS * HA_DK + 2 * HB_HEADS * HB_DQK + 2 * HB_HEADS * HB_DV
    w_even = jnp.concatenate([even_w_in[e][:, :n_main], _pad_cols(even_w_in[e][:, n_main:], LANE)],
                             axis=1).astype(BF16)
    p_even = _proj(x, row(norm_mix[0]), w_even, "proj_even")
    gbias = _pad_cols(jnp.concatenate([mlstm_b_i[e], mlstm_b_f[e]]).reshape(1, -1).astype(F32), LANE)
    hs_in = jnp.swapaxes(state_hgrn_S[e].astype(F32), -1, -2)
    mc_in = jnp.swapaxes(state_mlstm_C[e].astype(F32), -1, -2)
    mn_in = state_mlstm_n[e].astype(F32)[:, :, None, :]
    mm_in = jnp.pad(state_mlstm_m[e].astype(F32), ((0, 0), (HB_HEADS, LANE - 2 * HB_HEADS)))[:, None, :]
    mrg0, hs_o, mc_o, mn_o, mm_o = _even_mixer(
        p_even, tables, hgrn_lb_logits.astype(F32), e, row(hgrn_norm[e]), row(mlstm_norm[e]), gbias,
        hs_in, mc_in, mn_in, mm_in)
    x = _post(x, mrg0, even_w_out[e].astype(BF16), row(norm_ffn[0]), ffn_w_in[0].astype(BF16),
              ffn_w_out[0].astype(BF16), row(norm_final), False, "post_even")

    o = 0
    perm = _gdn_perm()
    n_qkv = perm.shape[0]
    n_z = HC_V_HEADS * HC_DV
    w_odd = jnp.concatenate([odd_w_in[o][:, perm], odd_w_in[o][:, n_qkv:n_qkv + n_z],
                             _pad_cols(odd_w_in[o][:, n_qkv + n_z:], LANE)], axis=1).astype(BF16)
    p_odd = _proj(x, row(norm_mix[1]), w_odd, "proj_odd")
    lane_pad = lambda v: jnp.pad(v.reshape(1, -1).astype(F32),
                                 ((0, 0), (HC_V_HEADS, LANE - 2 * HC_V_HEADS)))
    mrg1, gs_o, cv_o = _gdn_mixer(
        p_odd, tables, odd_conv_w[o].astype(F32)[:, perm], lane_pad(gdn_a_log[o]), lane_pad(gdn_dt_bias[o]),
        row(gdn_norm[o]), state_gdn_S[o].astype(F32), state_gdn_conv[o].astype(F32)[:, :, perm])
    y = _post(x, mrg1, odd_w_out[o].astype(BF16), row(norm_ffn[1]), ffn_w_in[1].astype(BF16),
              ffn_w_out[1].astype(BF16), row(norm_final), True, "post_odd")

    y_prompt = y[c:c + n_b * t_len].reshape(n_b, t_len, d)
    y_sample = y[c + n_b * t_len:].reshape(n_s, l_s, d)
    inv_perm = np.argsort(perm)
    hs_o = jnp.swapaxes(hs_o, -1, -2)
    mc_o = jnp.swapaxes(mc_o, -1, -2)
    mn_o = mn_o[:, :, 0, :]
    mm_o = mm_o[:, 0, HB_HEADS:2 * HB_HEADS]
    cv_o = cv_o[:, :, inv_perm]
    stk = lambda a, lo, hi: a[lo:hi][None].astype(dt)
    p_sl, s_sl = (0, n_b), (n_b, n_b + n_s)
    outs = [y_prompt, y_sample]
    for lo, hi in (p_sl, s_sl):
        outs += [stk(hs_o, lo, hi), stk(mc_o, lo, hi), stk(mn_o, lo, hi), stk(mm_o, lo, hi),
                 stk(gs_o, lo, hi), stk(cv_o, lo, hi)]
    return tuple(outs)
```

```python
import functools

import numpy as np
import jax
import jax.numpy as jnp
from jax import lax
from jax.experimental import pallas as pl
from jax.experimental.pallas import tpu as pltpu

F32 = jnp.float32
BF16 = jnp.bfloat16

CHUNK = 64
LANES = 2
N_META = 16
NORM_EPS = 1e-6
GATE_CAP = 15.0
NEG_BIG = -1e30

HA_HEADS, HA_DK, HA_DV = 4, 128, 128
HB_HEADS, HB_DQK, HB_DV = 4, 64, 128
HC_QK_HEADS, HC_V_HEADS, HC_DK, HC_DV = 8, 16, 128, 128
CONV_W = 4
LANE = 128
DENSE_TM = 512
VMEM_LIMIT = 56 * 1024 * 1024


def _nn(a, b):
    return lax.dot_general(a, b, (((1,), (0,)), ((), ())), preferred_element_type=F32)


def _nt(a, b):
    return lax.dot_general(a, b, (((1,), (1,)), ((), ())), preferred_element_type=F32)


def _tn(a, b):
    return lax.dot_general(a, b, (((0,), (0,)), ((), ())), preferred_element_type=F32)


def _dot_pieces(dot, exact, x, n):
    e16 = exact.astype(BF16)
    acc = None
    for _ in range(n):
        piece = x.astype(BF16)
        part = dot(e16, piece)
        acc = part if acc is None else acc + part
        x = x - piece.astype(F32)
    return acc


def _sigmoid(x):
    return 1.0 / (1.0 + jnp.exp(-x))


def _silu(x):
    return x * _sigmoid(x)


def _softplus(x):
    return jnp.maximum(x, 0.0) + jnp.log1p(jnp.exp(-jnp.abs(x)))


def _log_sigmoid(x):
    return jnp.minimum(x, 0.0) - jnp.log1p(jnp.exp(-jnp.abs(x)))


def _rms(x, w):
    ms = jnp.mean(x * x, axis=-1, keepdims=True)
    return x * lax.rsqrt(ms + NORM_EPS) * w


def _proj_kernel(x_ref, nw_ref, w_ref, o_ref, *, n_chunk):
    h = _rms(x_ref[...], nw_ref[...]).astype(BF16)
    n_total = w_ref.shape[1]
    for n0 in range(0, n_total, n_chunk):
        n1 = min(n0 + n_chunk, n_total)
        o_ref[:, n0:n1] = _nn(h, w_ref[:, n0:n1])


def _post_kernel(x_ref, m_ref, wo_ref, nw_ref, wi_ref, w2_ref, nf_ref, o_ref, act_ref,
                 *, hidden, h_chunk, final_norm, unpair):
    x1 = x_ref[...] + _nn(m_ref[...], wo_ref[...])
    h = _rms(x1, nw_ref[...]).astype(BF16)
    for c0 in range(0, hidden, h_chunk):
        c1 = min(c0 + h_chunk, hidden)
        gate = _nn(h, wi_ref[:, c0:c1])
        up = _nn(h, wi_ref[:, hidden + c0:hidden + c1])
        act_ref[:, c0:c1] = (_silu(gate) * up).astype(BF16)
    x2 = x1 + _nn(act_ref[...], w2_ref[...])
    if final_norm:
        x2 = _rms(x2, nf_ref[...])
    if unpair:
        for k in range(x2.shape[0] // (LANES * CHUNK)):
            for ln in range(LANES):
                r0 = (k * LANES + ln) * CHUNK
                o_ref[k // unpair, ln, k % unpair] = x2[r0:r0 + CHUNK]
    else:
        o_ref[...] = x2


def _const_spec(shape):
    nd = len(shape)
    return pl.BlockSpec(shape, lambda i, _n=nd: (0,) * _n)


def _proj(x, nw, w, tm, name):
    m_rows, d = x.shape
    n = w.shape[1]
    return pl.pallas_call(
        functools.partial(_proj_kernel, n_chunk=512),
        grid=(m_rows // tm,),
        in_specs=[pl.BlockSpec((tm, d), lambda i: (i, 0)),
                  _const_spec((1, d)),
                  _const_spec((d, n))],
        out_specs=pl.BlockSpec((tm, n), lambda i: (i, 0)),
        out_shape=jax.ShapeDtypeStruct((m_rows, n), F32),
        compiler_params=pltpu.CompilerParams(dimension_semantics=("arbitrary",),
                                             vmem_limit_bytes=VMEM_LIMIT),
        name=name,
    )(x, nw, w)


def _post(x, mrg, wo, nw, wi, w2, nf, tm, name, final_norm=False, tile0=0, n_tiles=None, unpair_shape=None):
    m_rows, d = x.shape
    km = mrg.shape[1]
    hidden = w2.shape[0]
    n_tiles = m_rows // tm if n_tiles is None else n_tiles
    unpair = 0
    if unpair_shape is None:
        out_spec = pl.BlockSpec((tm, d), lambda i: (i, 0))
        out_shape = jax.ShapeDtypeStruct((n_tiles * tm, d), F32)
    else:
        pairs, seq_len = unpair_shape
        chunks = seq_len // CHUNK
        k_tile = tm // (LANES * CHUNK)
        assert pairs * chunks == n_tiles * k_tile
        if chunks % k_tile == 0:
            per_pair = chunks // k_tile
            unpair = k_tile
            out_spec = pl.BlockSpec((1, LANES, k_tile, CHUNK, d),
                                    lambda i: (i // per_pair, 0, i % per_pair, 0, 0))
        else:
            assert k_tile % chunks == 0
            unpair = chunks
            out_spec = pl.BlockSpec((k_tile // chunks, LANES, chunks, CHUNK, d), lambda i: (i, 0, 0, 0, 0))
        out_shape = jax.ShapeDtypeStruct((pairs, LANES, chunks, CHUNK, d), F32)
    return pl.pallas_call(
        functools.partial(_post_kernel, hidden=hidden, h_chunk=256, final_norm=final_norm,
                          unpair=unpair),
        grid=(n_tiles,),
        in_specs=[pl.BlockSpec((tm, d), lambda i: (tile0 + i, 0)),
                  pl.BlockSpec((tm, km), lambda i: (tile0 + i, 0)),
                  _const_spec((km, d)),
                  _const_spec((1, d)),
                  _const_spec((d, 2 * hidden)),
                  _const_spec((hidden, d)),
                  _const_spec((1, d))],
        out_specs=out_spec,
        out_shape=out_shape,
        scratch_shapes=[pltpu.VMEM((tm, hidden), BF16)],
        compiler_params=pltpu.CompilerParams(dimension_semantics=("arbitrary",),
                                             vmem_limit_bytes=VMEM_LIMIT),
        name=name,
    )(x, mrg, wo, nw, wi, w2, nf)


def _step_tables(prompt_pairs, chunks_prompt, sample_pairs, chunks_sample, n_blocks):
    n_slots = prompt_pairs + sample_pairs + 2
    spare = n_slots - 2
    n_prompt = prompt_pairs * chunks_prompt
    n_sample = sample_pairs * chunks_sample
    rb, slot, fs, insl = [n_prompt + n_sample], [spare], [1], [0]
    for b in range(prompt_pairs):
        for c in range(chunks_prompt):
            rb.append(b * chunks_prompt + c); slot.append(b); fs.append(2 if c == 0 else 0); insl.append(0)
    for s in range(sample_pairs):
        for c in range(chunks_sample):
            rb.append(n_prompt + s * chunks_sample + c); slot.append(prompt_pairs + s)
            fs.append(3 if c == 0 else 0); insl.append(s)
    for blk in range(n_prompt + n_sample + 1, n_blocks):
        rb.append(blk); slot.append(spare + 1); fs.append(1); insl.append(0)
    to = lambda v: jnp.asarray(np.asarray(v, np.int32))
    return to(rb), to(slot), to(fs), to(insl), n_slots


def _hgrn_tables(c):
    t = np.arange(c)[:, None]
    j = np.arange(c)[None, :]
    mats, masks = [], []
    m = c // 2
    while m >= 1:
        blk = t // (2 * m)
        bd = blk * 2 * m + m - 1
        second = (t % (2 * m)) >= m
        mat = np.where(second, (j > bd) & (j <= t), (j > t) & (j <= bd)).astype(np.float32)
        mats.append(mat)
        masks.append((blk == (j // (2 * m))).astype(np.float32))
        m //= 2
    mats.append((j <= t).astype(np.float32))
    masks.append((j == t).astype(np.float32))
    return (jnp.asarray(np.concatenate(mats, 0)), jnp.asarray(np.stack(masks, 0)),
            len(mats) - 1)


def _even_kernel(rb_ref, slot_ref, fs_ref, insl_ref,
                 pa_ref, qb_ref, kb_ref, vb_ref, ob_ref, gt_ref,
                 lbl_ref, hn_ref, mnw_ref, gbias_ref,
                 lvl_ref, lmask_ref, tri_ref, eye_ref,
                 hs_in, mc_in, mn_in, mm_in,
                 mrg_ref, hs_out, mc_out, mn_out, mm_out,
                 meta_hs, meta_mc, meta_mn, meta_mm,
                 *, lb_index, n_levels):
    c = CHUNK
    j = pl.program_id(0)
    fs = fs_ref[j]

    @pl.when(fs == 1)
    def _():
        hs_out[...] = jnp.zeros_like(hs_out)
        mc_out[...] = jnp.zeros_like(mc_out)
        mn_out[...] = jnp.zeros_like(mn_out)
        mm_out[...] = jnp.zeros_like(mm_out)

    @pl.when(fs == 2)
    def _():
        hs_out[...] = meta_hs[...]
        mc_out[...] = meta_mc[...]
        mn_out[...] = meta_mn[...]
        mm_out[...] = meta_mm[...]

    @pl.when(fs == 3)
    def _():
        hs_out[...] = hs_in[...]
        mc_out[...] = mc_in[...]
        mn_out[...] = mn_in[...]
        mm_out[...] = mm_in[...]

    lanes = range(LANES)
    rows = [slice(ln * c, (ln + 1) * c) for ln in lanes]
    hk = HA_HEADS * HA_DK
    ua = [(ln, h) for ln in lanes for h in range(HA_HEADS)]
    ub = [(ln, h) for ln in lanes for h in range(HB_HEADS)]
    hsl = [slice(h * HA_DK, (h + 1) * HA_DK) for h in range(HA_HEADS)]

    lbl = lbl_ref[...]
    ex = jnp.exp(lbl - jnp.max(lbl, axis=0, keepdims=True))
    sm = ex / jnp.sum(ex, axis=0, keepdims=True)
    lb = jnp.sum(sm[:lb_index + 1], axis=0, keepdims=True)
    log_f, k_all, q_all, e_lvl, b_all = [], [], [], [], []
    for ln in lanes:
        fa = pa_ref[rows[ln], hk:2 * hk]
        log_f.append(jnp.log(lb + (1.0 - lb) * _sigmoid(fa)))
        k_all.append((1.0 - lb) * _sigmoid(-fa))
        q_all.append(_silu(pa_ref[rows[ln], 0:hk]))
    for ln in lanes:
        b_all.append(_dot_pieces(_nn, lvl_ref[n_levels * c:(n_levels + 1) * c, :], log_f[ln], 2))
    for lvl in range(n_levels):
        e_lvl.append([_dot_pieces(_nn, lvl_ref[lvl * c:(lvl + 1) * c, :], log_f[ln], 2) for ln in lanes])

    lane = lax.broadcasted_iota(jnp.int32, (c, LANE), 1)
    rowg = lax.broadcasted_iota(jnp.int32, (c, LANE), 0)
    is_f = (lane >= HB_HEADS) & (lane < 2 * HB_HEADS)
    pad = rowg < jnp.where(j == 0, c - N_META, 0)
    ig_t, fc, fr, ir = [], [], [], []
    for ln in lanes:
        gcap = GATE_CAP * jnp.tanh((gt_ref[rows[ln], :] + gbias_ref[...]) / GATE_CAP)
        lf_t = jnp.where(is_f & jnp.logical_not(pad), _log_sigmoid(gcap), 0.0)
        ig_t.append(jnp.where(pad, NEG_BIG, gcap))
        fc.append(_dot_pieces(_nn, tri_ref[...], lf_t, 3))
    for ln in lanes:
        fr.append(_dot_pieces(_nt, eye_ref[...], fc[ln], 3))
        ir.append(_dot_pieces(_nt, eye_ref[...], ig_t[ln], 3))

    row = lax.broadcasted_iota(jnp.int32, (c, HA_DK), 0)
    q_h = [q_all[ln][:, hsl[h]] for ln, h in ua]
    k_h = [k_all[ln][:, hsl[h]] for ln, h in ua]
    v16 = [pa_ref[rows[ln], 2 * hk + h * HA_DV:2 * hk + (h + 1) * HA_DV].astype(BF16) for ln, h in ua]
    b_h = [b_all[ln][:, hsl[h]] for ln, h in ua]
    st = [hs_out[0, ln, h] for ln, h in ua]
    o_inter = [_nt((q_h[u] * jnp.exp(b_h[u])).astype(BF16), st[u].astype(BF16)) for u in range(len(ua))]
    for u, (ln, h) in enumerate(ua):
        b_last = b_h[u][c - 1:c, :]
        k_til = k_h[u] * jnp.exp(b_last - b_h[u])
        hs_out[0, ln, h] = st[u] * jnp.exp(b_last) + _tn(v16[u], k_til.astype(BF16))
    scores = [lmask_ref[n_levels] * _nt(q_h[u].astype(BF16), k_h[u].astype(BF16)) for u in range(len(ua))]
    m = c // 2
    for lvl in range(n_levels):
        second = (row & m) != 0
        for u, (ln, h) in enumerate(ua):
            dec = jnp.exp(e_lvl[lvl][ln][:, hsl[h]])
            qs = jnp.where(second, q_h[u] * dec, 0.0).astype(BF16)
            ks = jnp.where(second, 0.0, k_h[u] * dec).astype(BF16)
            part = _nt(qs, ks)
            scores[u] = scores[u] + (part if lvl == 0 else lmask_ref[lvl] * part)
        m //= 2

    r_i = lax.broadcasted_iota(jnp.int32, (c, c), 0)
    c_i = lax.broadcasted_iota(jnp.int32, (c, c), 1)
    causal = c_i <= r_i
    lane1 = lax.broadcasted_iota(jnp.int32, (1, LANE), 1)
    m_row = [mm_out[0, ln] for ln in lanes]
    m_row_new = list(m_row)
    mq = [qb_ref[rows[ln], h * HB_DQK:(h + 1) * HB_DQK] for ln, h in ub]
    mk = [kb_ref[rows[ln], h * HB_DQK:(h + 1) * HB_DQK] * (HB_DQK ** -0.5) for ln, h in ub]
    mv16 = [vb_ref[rows[ln], h * HB_DV:(h + 1) * HB_DV].astype(BF16) for ln, h in ub]
    ct = [mc_out[0, ln, h] for ln, h in ub]
    n_row = [mn_out[0, ln, h] for ln, h in ub]
    qk_raw = [_nt(mq[u].astype(BF16), mk[u].astype(BF16)) for u in range(len(ub))]
    inter = [_nt(mq[u].astype(BF16), ct[u].astype(BF16)) for u in range(len(ub))]
    qkd, m_ts, w_inters = [], [], []
    for u, (ln, h) in enumerate(ub):
        fl = HB_HEADS + h
        fc_h = fc[ln][:, fl:fl + 1]
        fr_h = fr[ln][fl:fl + 1, :]
        igr_h = ir[ln][h:h + 1, :]
        igc_h = ig_t[ln][:, h:h + 1]
        m_prev = m_row[ln][:, fl:fl + 1]
        log_d = jnp.where(causal, fc_h - fr_h + igr_h, -jnp.inf)
        log_inter = fc_h + m_prev
        m_t = jnp.maximum(log_inter, jnp.max(log_d, axis=1, keepdims=True))
        m_ts.append(m_t)
        w_inters.append(jnp.exp(log_inter - m_t))
        qkd.append(qk_raw[u] * jnp.exp(log_d - m_t))
        f_last = fc_h[c - 1:c, :]
        m_new = m_t[c - 1:c, :]
        w_s = jnp.exp(f_last - fc_h + igc_h - m_new)
        decay = jnp.exp(f_last + m_prev - m_new)
        kw = mk[u] * w_s
        mc_out[0, ln, h] = decay * ct[u] + _tn(mv16[u], kw.astype(BF16))
        mn_out[0, ln, h] = decay * n_row[u] + jnp.sum(kw, axis=0, keepdims=True)
        m_row_new[ln] = jnp.where(lane1 == fl, m_new, m_row_new[ln])
    for ln in lanes:
        mm_out[0, ln] = m_row_new[ln]

    o_a = [_nn(scores[u].astype(BF16), v16[u]) + o_inter[u] for u in range(len(ua))]
    num = [_nn(qkd[u].astype(BF16), mv16[u]) + w_inters[u] * inter[u] for u in range(len(ub))]
    for u, (ln, h) in enumerate(ua):
        ga = pa_ref[rows[ln], 3 * hk + h * HA_DV:3 * hk + (h + 1) * HA_DV]
        mrg_ref[rows[ln], h * HA_DV:(h + 1) * HA_DV] = (_rms(o_a[u], hn_ref[:, hsl[h]]) * _silu(ga)).astype(BF16)
    for u, (ln, h) in enumerate(ub):
        den = jnp.sum(qkd[u], axis=1, keepdims=True) \
            + w_inters[u] * jnp.sum(mq[u] * n_row[u], axis=1, keepdims=True)
        hh = num[u] / jnp.maximum(jnp.abs(den), jnp.exp(-m_ts[u]))
        ob = ob_ref[rows[ln], h * HB_DV:(h + 1) * HB_DV]
        col0 = HA_HEADS * HA_DV + h * HB_DV
        mrg_ref[rows[ln], col0:col0 + HB_DV] = (
            _rms(hh, mnw_ref[:, h * HB_DV:(h + 1) * HB_DV]) * _sigmoid(ob)).astype(BF16)

    @pl.when(j == 0)
    def _():
        meta_hs[...] = hs_out[...]
        meta_mc[...] = mc_out[...]
        meta_mn[...] = mn_out[...]
        meta_mm[...] = mm_out[...]


def _even_mixer(p, tables, lb_logits, lb_index, hgrn_norm, mlstm_norm, gbias, hs_in, mc_in, mn_in, mm_in):
    rb, slot, fs, insl, n_slots = tables
    c = CHUNK
    rows = LANES * c
    n_steps = rb.shape[0]
    lvl, lmask, n_levels = _hgrn_tables(c)
    tri = jnp.asarray(np.tril(np.ones((c, c), np.float32)))
    eye = jnp.eye(LANE, dtype=F32)
    hk = HA_HEADS * HA_DK
    wa = 4 * hk
    qk_w = HB_HEADS * HB_DQK
    v_w = HB_HEADS * HB_DV
    g_col = wa + 2 * qk_w + 2 * v_w
    cst = lambda shape: pl.BlockSpec(shape, lambda j, *_: (0,) * len(shape))
    row_blk = lambda w, idx: pl.BlockSpec((rows, w), lambda j, r, *_: (r[j], idx))
    st_in = lambda shape: pl.BlockSpec((1,) + shape, lambda j, r, s, f, i: (i[j],) + (0,) * len(shape))
    st_out = lambda shape: pl.BlockSpec((1,) + shape, lambda j, r, s, f, i: (s[j],) + (0,) * len(shape))
    sh_hs = (LANES, HA_HEADS, HA_DV, HA_DK)
    sh_mc = (LANES, HB_HEADS, HB_DV, HB_DQK)
    sh_mn = (LANES, HB_HEADS, 1, HB_DQK)
    sh_mm = (LANES, 1, LANE)
    grid_spec = pltpu.PrefetchScalarGridSpec(
        num_scalar_prefetch=4,
        grid=(n_steps,),
        in_specs=[row_blk(wa, 0),
                  row_blk(qk_w, wa // qk_w), row_blk(qk_w, wa // qk_w + 1),
                  row_blk(v_w, (wa + 2 * qk_w) // v_w), row_blk(v_w, (wa + 2 * qk_w) // v_w + 1),
                  row_blk(LANE, g_col // LANE),
                  cst(lb_logits.shape), cst((1, hk)), cst((1, v_w)), cst((1, LANE)),
                  cst(lvl.shape), cst(lmask.shape), cst((c, c)), cst((LANE, LANE)),
                  st_in(sh_hs), st_in(sh_mc), st_in(sh_mn), st_in(sh_mm)],
        out_specs=[pl.BlockSpec((rows, hk + v_w), lambda j, r, *_: (r[j], 0)),
                   st_out(sh_hs), st_out(sh_mc), st_out(sh_mn), st_out(sh_mm)],
        scratch_shapes=[pltpu.VMEM((1,) + sh_hs, F32), pltpu.VMEM((1,) + sh_mc, F32),
                        pltpu.VMEM((1,) + sh_mn, F32), pltpu.VMEM((1,) + sh_mm, F32)],
    )
    out_shape = [jax.ShapeDtypeStruct((p.shape[0], hk + v_w), BF16),
                 jax.ShapeDtypeStruct((n_slots,) + sh_hs, F32),
                 jax.ShapeDtypeStruct((n_slots,) + sh_mc, F32),
                 jax.ShapeDtypeStruct((n_slots,) + sh_mn, F32),
                 jax.ShapeDtypeStruct((n_slots,) + sh_mm, F32)]
    return pl.pallas_call(
        functools.partial(_even_kernel, lb_index=lb_index, n_levels=n_levels),
        grid_spec=grid_spec,
        out_shape=out_shape,
        compiler_params=pltpu.CompilerParams(dimension_semantics=("arbitrary",),
                                             vmem_limit_bytes=VMEM_LIMIT),
        name="even_mixer",
    )(rb, slot, fs, insl, p, p, p, p, p, p, lb_logits, hgrn_norm, mlstm_norm, gbias,
      lvl, lmask, tri, eye, hs_in, mc_in, mn_in, mm_in)


def _gdn_kernel(rb_ref, slot_ref, fs_ref, insl_ref,
                qkv_ref, z_ref, gt_ref, cw_ref, alog_ref, dtb_ref, gnw_ref,
                tri_ref, s_in, cv_in,
                mrg_ref, s_out, cv_out,
                xpad, meta_s, meta_cv, *, n_sq, hps):
    c = CHUNK
    hg = pl.program_id(0)
    j = pl.program_id(1)
    fs = fs_ref[j]
    nk = CONV_W - 1
    base = 8 - nk

    @pl.when(fs == 1)
    def _():
        s_out[...] = jnp.zeros_like(s_out)
        xpad[:, base:8, :] = jnp.zeros((LANES, nk, xpad.shape[2]), F32)

    @pl.when(fs == 2)
    def _():
        s_out[...] = meta_s[...]
        xpad[:, base:8, :] = meta_cv[...]

    @pl.when(fs == 3)
    def _():
        s_out[...] = s_in[...]
        xpad[:, base:8, :] = cv_in[0]

    lanes = range(LANES)
    rows = [slice(ln * c, (ln + 1) * c) for ln in lanes]
    lane = lax.broadcasted_iota(jnp.int32, (c, LANE), 1)
    row = lax.broadcasted_iota(jnp.int32, (c, LANE), 0)
    first = lane < c
    s_idx = jnp.where(first, lane, lane - c)
    causal = s_idx <= row
    strict = s_idx < row
    diag = s_idx == row
    eye_c = diag.astype(F32)
    r2 = lax.broadcasted_iota(jnp.int32, (2 * c, LANE), 0)
    l2 = lax.broadcasted_iota(jnp.int32, (2 * c, LANE), 1)
    same_head = (r2 < c) == (l2 < c)
    wq = 4 * HC_DK

    def col(tile, idx):
        return jnp.sum(jnp.where(lane == idx, tile, 0.0), axis=1, keepdims=True)

    def block_diag(m16):
        return jnp.where(same_head, jnp.concatenate([m16, m16], axis=0), jnp.zeros_like(m16[:1, :1]))

    act, beta_t, gc = [], [], []
    for ln in lanes:
        xpad[ln, 8:8 + c, :] = qkv_ref[rows[ln], :]
        xp = xpad[ln]
        y = cw_ref[nk:CONV_W, :] * xp[8:8 + c]
        for t in range(nk):
            y = y + cw_ref[t:t + 1, :] * pltpu.roll(xp, nk - t, 0)[8:8 + c]
        carry = xp[base + c:8 + c]
        cv_out[0, ln] = carry
        xpad[ln, base:8, :] = carry
        act.append(_silu(y))
        gt = gt_ref[rows[ln], :]
        beta_t.append(_sigmoid(gt))
        g_t = -jnp.exp(alog_ref[...]) * _softplus(gt + dtb_ref[...])
        gc.append(_dot_pieces(_nn, tri_ref[...], g_t, 3))

    units = [(ln, p) for ln in lanes for p in range(hps)]
    nu = range(len(units))
    qn, kn, qk, kk = [], [], [], []
    for ln, p in units:
        q = act[ln][:, p * wq:p * wq + HC_DK]
        k = act[ln][:, p * wq + HC_DK:p * wq + 2 * HC_DK]
        qn.append(q * lax.rsqrt(jnp.sum(q * q, axis=-1, keepdims=True) + NORM_EPS) * (HC_DK ** -0.5))
        kn.append(k * lax.rsqrt(jnp.sum(k * k, axis=-1, keepdims=True) + NORM_EPS))
    for u in nu:
        kn16 = kn[u].astype(BF16)
        k2 = jnp.concatenate([kn16, kn16], axis=0)
        qk.append(_nt(qn[u].astype(BF16), k2))
        kk.append(_nt(kn16, k2))

    beta, gcol, decay, inv, pw = [], [], [], [], []
    for u, (ln, p) in enumerate(units):
        hv = 2 * (hg * hps + p)
        b_a, b_b = col(beta_t[ln], hv), col(beta_t[ln], hv + 1)
        g_a, g_b = col(gc[ln], HC_V_HEADS + hv), col(gc[ln], HC_V_HEADS + hv + 1)
        beta.append((b_a, b_b))
        gcol.append((g_a, g_b))
        g_c = jnp.where(first, g_a, g_b)
        g_r = jnp.sum(jnp.where(diag, g_c, 0.0), axis=0, keepdims=True)
        decay.append(jnp.exp(jnp.where(causal, g_c - g_r, -jnp.inf)))
        low = jnp.where(strict, jnp.where(first, b_a, b_b) * kk[u] * decay[u], 0.0)
        inv.append(eye_c - low)
        pw.append(low.astype(BF16))
    pw = [_nn(x, block_diag(x)).astype(BF16) for x in pw]
    for i in range(n_sq):
        if i < n_sq - 1:
            prod = [_nn(jnp.concatenate([x.astype(BF16), y], axis=0), block_diag(y)) for x, y in zip(inv, pw)]
            inv = [x + r[:c] for x, r in zip(inv, prod)]
            pw = [r[c:].astype(BF16) for r in prod]
        else:
            inv = [x + _nn(x.astype(BF16), block_diag(y)) for x, y in zip(inv, pw)]

    sol, eg = [], []
    for u, (ln, p) in enumerate(units):
        (b_a, b_b), (g_a, g_b) = beta[u], gcol[u]
        e_a, e_b = jnp.exp(g_a), jnp.exp(g_b)
        eg.append((e_a, e_b))
        v_a = act[ln][:, p * wq + 2 * HC_DK:p * wq + 3 * HC_DK]
        v_b = act[ln][:, p * wq + 3 * HC_DK:p * wq + 4 * HC_DK]
        rhs = jnp.concatenate([jnp.concatenate([v_a * b_a, kn[u] * (b_a * e_a)], axis=1),
                               jnp.concatenate([v_b * b_b, kn[u] * (b_b * e_b)], axis=1)],
                              axis=0).astype(BF16)
        sol.append((_nn(jnp.where(first, inv[u], 0.0).astype(BF16), rhs),
                    _nn(jnp.where(first, 0.0, inv[u]).astype(BF16), rhs)))
    s_old = [[s_out[0, ln, 2 * p + jj] for jj in range(2)] for ln, p in units]
    ws = []
    for u in nu:
        for jj in range(2):
            lhs = jnp.concatenate([sol[u][jj][:, HC_DV:], qn[u] * eg[u][jj]], axis=0)
            ws.append(_nn(lhs.astype(BF16), s_old[u][jj].astype(BF16)))
    for u, (ln, p) in enumerate(units):
        v_new = [sol[u][jj][:, :HC_DV] - ws[2 * u + jj][:c] for jj in range(2)]
        v_st = jnp.concatenate(v_new, axis=0).astype(BF16)
        attn = qk[u] * decay[u]
        o = (ws[2 * u][c:] + _nn(jnp.where(first, attn, 0.0).astype(BF16), v_st),
             ws[2 * u + 1][c:] + _nn(jnp.where(first, 0.0, attn).astype(BF16), v_st))
        for jj in range(2):
            n = 2 * p + jj
            g_h = gcol[u][jj]
            g_last = g_h[c - 1:c, :]
            k_til = kn[u] * jnp.exp(g_last - g_h)
            s_out[0, ln, n] = jnp.exp(g_last) * s_old[u][jj] + _tn(k_til.astype(BF16), v_new[jj].astype(BF16))
            z = z_ref[rows[ln], n * HC_DV:(n + 1) * HC_DV]
            mrg_ref[rows[ln], n * HC_DV:(n + 1) * HC_DV] = (_rms(o[jj], gnw_ref[...]) * _silu(z)).astype(BF16)

    @pl.when(j == 0)
    def _():
        meta_s[...] = s_out[...]
        meta_cv[...] = xpad[:, base:8, :]


def _gdn_mixer(p, tables, conv_w, alog_row, dtb_row, gnw, s_in, cv_in, hps=8):
    rb, slot, fs, insl, n_slots = tables
    c = CHUNK
    rows = LANES * c
    n_steps = rb.shape[0]
    n_hg = HC_QK_HEADS // hps
    wq = hps * 4 * HC_DK
    wz = hps * 2 * HC_DV
    qkv_w = HC_QK_HEADS * 4 * HC_DK
    z_w = HC_V_HEADS * HC_DV
    n_sq = int(np.log2(c)) - 1
    tri = jnp.asarray(np.tril(np.ones((c, c), np.float32)))
    nk = CONV_W - 1
    sh_s = (1, LANES, 2 * hps, HC_DK, HC_DV)
    sh_cv = (1, LANES, nk, wq)
    cst = lambda shape: pl.BlockSpec(shape, lambda h, j, *_: (0,) * len(shape))
    grid_spec = pltpu.PrefetchScalarGridSpec(
        num_scalar_prefetch=4,
        grid=(n_hg, n_steps),
        in_specs=[pl.BlockSpec((rows, wq), lambda h, j, r, *_: (r[j], h)),
                  pl.BlockSpec((rows, wz), lambda h, j, r, *_: (r[j], qkv_w // wz + h)),
                  pl.BlockSpec((rows, LANE), lambda h, j, r, *_: (r[j], (qkv_w + z_w) // LANE)),
                  pl.BlockSpec((CONV_W, wq), lambda h, j, *_: (0, h)),
                  cst((1, LANE)), cst((1, LANE)), cst((1, HC_DV)),
                  cst((c, c)),
                  pl.BlockSpec(sh_s, lambda h, j, r, s, f, i: (i[j], 0, h, 0, 0)),
                  pl.BlockSpec(sh_cv, lambda h, j, r, s, f, i: (i[j], 0, 0, h))],
        out_specs=[pl.BlockSpec((rows, wz), lambda h, j, r, *_: (r[j], h)),
                   pl.BlockSpec(sh_s, lambda h, j, r, s, f, i: (s[j], 0, h, 0, 0)),
                   pl.BlockSpec(sh_cv, lambda h, j, r, s, f, i: (s[j], 0, 0, h))],
        scratch_shapes=[pltpu.VMEM((LANES, 8 + c, wq), F32),
                        pltpu.VMEM(sh_s, F32),
                        pltpu.VMEM((LANES, nk, wq), F32)],
    )
    out_shape = [jax.ShapeDtypeStruct((p.shape[0], z_w), BF16),
                 jax.ShapeDtypeStruct((n_slots, LANES, HC_V_HEADS, HC_DK, HC_DV), F32),
                 jax.ShapeDtypeStruct((n_slots, LANES, nk, qkv_w), F32)]
    return pl.pallas_call(
        functools.partial(_gdn_kernel, n_sq=n_sq, hps=hps),
        grid_spec=grid_spec,
        out_shape=out_shape,
        compiler_params=pltpu.CompilerParams(dimension_semantics=("arbitrary", "arbitrary"),
                                             vmem_limit_bytes=VMEM_LIMIT),
        name="gdn_mixer",
    )(rb, slot, fs, insl, p, p, p, conv_w, alog_row, dtb_row, gnw, tri, s_in, cv_in)


def _pair_cols(a):
    nq = HC_QK_HEADS * HC_DK
    lead = a.shape[:-1]
    q = a[..., :nq].reshape(lead + (HC_QK_HEADS, HC_DK))
    k = a[..., nq:2 * nq].reshape(lead + (HC_QK_HEADS, HC_DK))
    v = a[..., 2 * nq:].reshape(lead + (HC_QK_HEADS, 2 * HC_DV))
    return jnp.concatenate([q, k, v], axis=-1).reshape(lead + (-1,))


def _unpair_cols(a):
    lead = a.shape[:-1]
    g = a.reshape(lead + (HC_QK_HEADS, 2 * HC_DK + 2 * HC_DV))
    parts = [g[..., :HC_DK], g[..., HC_DK:2 * HC_DK], g[..., 2 * HC_DK:]]
    return jnp.concatenate([x.reshape(lead + (-1,)) for x in parts], axis=-1)


def _pair_rows(a):
    n, seq, d = a.shape
    return a.reshape(n // LANES, LANES, seq // CHUNK, CHUNK, d).transpose(0, 2, 1, 3, 4).reshape(n * seq, d)


def _pad_cols(w, width):
    return jnp.pad(w, ((0, 0), (0, width - w.shape[1])))


def kernel(x_prompt, x_sample, state_hgrn_S, state_mlstm_C, state_mlstm_n, state_mlstm_m, state_gdn_S,
           state_gdn_conv, meta_tokens, norm_mix, norm_ffn, norm_final, even_w_in, even_w_out,
           hgrn_lb_logits, hgrn_norm, mlstm_b_i, mlstm_b_f, mlstm_norm, odd_w_in, odd_conv_w, gdn_a_log,
           gdn_dt_bias, gdn_norm, odd_w_out, ffn_w_in, ffn_w_out):
    n_b, t_len, d = x_prompt.shape
    n_s, l_s, _ = x_sample.shape
    c = CHUNK
    blk = LANES * c
    assert t_len % c == 0 and l_s % c == 0 and meta_tokens.shape[0] == N_META
    assert n_b % LANES == 0 and n_s % LANES == 0, "sequences are processed in pairs"
    assert norm_mix.shape[0] == 2, "one even (HGRN2+mLSTM) and one odd (DeltaNet) layer"
    dt = x_prompt.dtype
    rows_p, rows_s = n_b * t_len, n_s * l_s
    tm = min(DENSE_TM, int(np.gcd(rows_p, rows_s)))
    m_rows = -(-(rows_p + rows_s + blk) // tm) * tm

    meta_job = jnp.concatenate([jnp.zeros((c - N_META, d), dt), meta_tokens.astype(dt)], axis=0)
    x = jnp.concatenate([_pair_rows(x_prompt), _pair_rows(x_sample), meta_job, meta_job,
                         jnp.zeros((m_rows - rows_p - rows_s - blk, d), dt)], axis=0)
    tables = _step_tables(n_b // LANES, t_len // c, n_s // LANES, l_s // c, m_rows // blk)
    row = lambda v: v.reshape(1, -1).astype(F32)
    pair = lambda a: a.reshape((a.shape[0] // LANES, LANES) + a.shape[1:])

    e = 0
    n_main = 4 * HA_HEADS * HA_DK + 2 * HB_HEADS * HB_DQK + 2 * HB_HEADS * HB_DV
    w_even = jnp.concatenate([even_w_in[e][:, :n_main], _pad_cols(even_w_in[e][:, n_main:], LANE)],
                             axis=1).astype(BF16)
    p_even = _proj(x, row(norm_mix[0]), w_even, tm, "proj_even")
    gbias = _pad_cols(jnp.concatenate([mlstm_b_i[e], mlstm_b_f[e]]).reshape(1, -1).astype(F32), LANE)
    hs_in = pair(jnp.swapaxes(state_hgrn_S[e].astype(F32), -1, -2))
    mc_in = pair(jnp.swapaxes(state_mlstm_C[e].astype(F32), -1, -2))
    mn_in = pair(state_mlstm_n[e].astype(F32)[:, :, None, :])
    mm_in = pair(jnp.pad(state_mlstm_m[e].astype(F32), ((0, 0), (HB_HEADS, LANE - 2 * HB_HEADS)))[:, None, :])
    mrg0, hs_o, mc_o, mn_o, mm_o = _even_mixer(
        p_even, tables, hgrn_lb_logits.astype(F32), e, row(hgrn_norm[e]), row(mlstm_norm[e]), gbias,
        hs_in, mc_in, mn_in, mm_in)
    x = _post(x, mrg0, even_w_out[e].astype(BF16), row(norm_ffn[0]), ffn_w_in[0].astype(BF16),
              ffn_w_out[0].astype(BF16), row(norm_final), tm, "post_even")

    o = 0
    n_qkv = 2 * HC_QK_HEADS * HC_DK + HC_V_HEADS * HC_DV
    n_z = HC_V_HEADS * HC_DV
    w_odd = jnp.concatenate([_pair_cols(odd_w_in[o][:, :n_qkv]), odd_w_in[o][:, n_qkv:n_qkv + n_z],
                             _pad_cols(odd_w_in[o][:, n_qkv + n_z:], LANE)], axis=1).astype(BF16)
    p_odd = _proj(x, row(norm_mix[1]), w_odd, tm, "proj_odd")
    lane_pad = lambda v: jnp.pad(v.reshape(1, -1).astype(F32),
                                 ((0, 0), (HC_V_HEADS, LANE - 2 * HC_V_HEADS)))
    mrg1, gs_o, cv_o = _gdn_mixer(
        p_odd, tables, _pair_cols(odd_conv_w[o].astype(F32)), lane_pad(gdn_a_log[o]), lane_pad(gdn_dt_bias[o]),
        row(gdn_norm[o]), pair(state_gdn_S[o].astype(F32)), pair(_pair_cols(state_gdn_conv[o].astype(F32))))
    post_odd = functools.partial(_post, x, mrg1, odd_w_out[o].astype(BF16), row(norm_ffn[1]),
                                 ffn_w_in[1].astype(BF16), ffn_w_out[1].astype(BF16), row(norm_final), tm,
                                 final_norm=True)
    y_p = post_odd("post_odd_prompt", tile0=0, n_tiles=rows_p // tm, unpair_shape=(n_b // LANES, t_len))
    y_s = post_odd("post_odd_sample", tile0=rows_p // tm, n_tiles=rows_s // tm,
                   unpair_shape=(n_s // LANES, l_s))

    y_prompt = y_p.reshape(n_b, t_len, d)
    y_sample = y_s.reshape(n_s, l_s, d)
    hs_o = jnp.swapaxes(hs_o, -1, -2)
    mc_o = jnp.swapaxes(mc_o, -1, -2)
    mn_o = mn_o[:, :, :, 0, :]
    mm_o = mm_o[:, :, 0, HB_HEADS:2 * HB_HEADS]
    cv_o = _unpair_cols(cv_o)
    np_, ns_ = n_b // LANES, n_s // LANES
    stk = lambda a, lo, hi: a[lo:hi].reshape((-1,) + a.shape[2:])[None].astype(dt)
    outs = [y_prompt, y_sample]
    for lo, hi in ((0, np_), (np_, np_ + ns_)):
        outs += [stk(hs_o, lo, hi), stk(mc_o, lo, hi), stk(mn_o, lo, hi), stk(mm_o, lo, hi),
                 stk(gs_o, lo, hi), stk(cv_o, lo, hi)]
    return tuple(outs)
```

```python
import functools

import numpy as np
import jax
import jax.numpy as jnp
from jax import lax
from jax.experimental import pallas as pl
from jax.experimental.pallas import tpu as pltpu

F32 = jnp.float32
BF16 = jnp.bfloat16

CHUNK = 64
LANES = 2
N_META = 16
NORM_EPS = 1e-6
GATE_CAP = 15.0
NEG_BIG = -1e30

HA_HEADS, HA_DK, HA_DV = 4, 128, 128
HB_HEADS, HB_DQK, HB_DV = 4, 64, 128
HC_QK_HEADS, HC_V_HEADS, HC_DK, HC_DV = 8, 16, 128, 128
CONV_W = 4
LANE = 128
DENSE_TM = 512
VMEM_LIMIT = 56 * 1024 * 1024


def _nn(a, b):
    return lax.dot_general(a, b, (((1,), (0,)), ((), ())), preferred_element_type=F32)


def _nt(a, b):
    return lax.dot_general(a, b, (((1,), (1,)), ((), ())), preferred_element_type=F32)


def _tn(a, b):
    return lax.dot_general(a, b, (((0,), (0,)), ((), ())), preferred_element_type=F32)


def _dot_pieces(dot, exact, x, n):
    e16 = exact.astype(BF16)
    acc = None
    for _ in range(n):
        piece = x.astype(BF16)
        part = dot(e16, piece)
        acc = part if acc is None else acc + part
        x = x - piece.astype(F32)
    return acc


def _sigmoid(x):
    return 1.0 / (1.0 + jnp.exp(-x))


def _silu(x):
    return x * _sigmoid(x)


def _softplus(x):
    return jnp.maximum(x, 0.0) + jnp.log1p(jnp.exp(-jnp.abs(x)))


def _log_sigmoid(x):
    return jnp.minimum(x, 0.0) - jnp.log1p(jnp.exp(-jnp.abs(x)))


def _rms(x, w):
    ms = jnp.mean(x * x, axis=-1, keepdims=True)
    return x * lax.rsqrt(ms + NORM_EPS) * w


def _pair_block(pairs, seq_len, tm):
    chunks = seq_len // CHUNK
    k_tile = tm // (LANES * CHUNK)
    if chunks % k_tile == 0:
        per_pair = chunks // k_tile
        return (1, LANES, k_tile, CHUNK), (lambda t: (t // per_pair, 0, t % per_pair, 0, 0)), k_tile
    assert k_tile % chunks == 0
    return (k_tile // chunks, LANES, chunks, CHUNK), (lambda t: (t, 0, 0, 0, 0)), chunks


def _tile_units(n_rows):
    return [((k * LANES + ln) * CHUNK, k, ln) for k in range(n_rows // (LANES * CHUNK)) for ln in range(LANES)]


def _load_x_tile(i, xp_ref, xs_ref, xm_ref, xt_ref, geom):
    n_pt, n_st, kc_p, kc_s = geom

    def fill(src, kc):
        for r0, k, ln in _tile_units(xt_ref.shape[0]):
            xt_ref[r0:r0 + CHUNK, :] = src[k // kc, ln, k % kc]

    @pl.when(i < n_pt)
    def _():
        fill(xp_ref, kc_p)

    @pl.when((i >= n_pt) & (i < n_pt + n_st))
    def _():
        fill(xs_ref, kc_s)

    @pl.when(i >= n_pt + n_st)
    def _():
        xt_ref[...] = xm_ref[...]


def _x_specs(xsrc, tm):
    xp5, xs5, xm = xsrc
    d = xp5.shape[-1]
    blk_p, idx_p, kc_p = _pair_block(xp5.shape[0], xp5.shape[2] * CHUNK, tm)
    blk_s, idx_s, kc_s = _pair_block(xs5.shape[0], xs5.shape[2] * CHUNK, tm)
    n_pt = xp5.shape[0] * xp5.shape[2] * LANES * CHUNK // tm
    n_st = xs5.shape[0] * xs5.shape[2] * LANES * CHUNK // tm
    specs = [pl.BlockSpec(blk_p + (d,), lambda i: idx_p(jnp.minimum(i, n_pt - 1))),
             pl.BlockSpec(blk_s + (d,), lambda i: idx_s(jnp.clip(i - n_pt, 0, n_st - 1))),
             pl.BlockSpec((tm, d), lambda i: (0, 0))]
    return specs, (n_pt, n_st, kc_p, kc_s), (n_pt + n_st + 1) * tm, d


def _proj_kernel(*refs, n_chunk, segs, geom):
    if geom is None:
        x_ref, nw_ref, w_ref, o_ref = refs
    else:
        xp_ref, xs_ref, xm_ref, nw_ref, w_ref, o_ref, x_ref = refs
        _load_x_tile(pl.program_id(0), xp_ref, xs_ref, xm_ref, x_ref, geom)
    h = _rms(x_ref[...], nw_ref[...]).astype(BF16)
    for s0, s1, fn in segs:
        for n0 in range(s0, s1, n_chunk):
            n1 = min(n0 + n_chunk, s1)
            o_ref[:, n0:n1] = fn(_nn(h, w_ref[:, n0:n1]))


def _post_kernel(*refs, hidden, h_chunk, final_norm, unpair, geom):
    if geom is None:
        x_ref, m_ref, wo_ref, nw_ref, wi_ref, w2_ref, nf_ref, o_ref, act_ref = refs
    else:
        xp_ref, xs_ref, xm_ref, m_ref, wo_ref, nw_ref, wi_ref, w2_ref, nf_ref, o_ref, act_ref, x_ref = refs
        _load_x_tile(pl.program_id(0), xp_ref, xs_ref, xm_ref, x_ref, geom)
    x1 = x_ref[...] + _nn(m_ref[...], wo_ref[...])
    h = _rms(x1, nw_ref[...]).astype(BF16)
    for c0 in range(0, hidden, h_chunk):
        c1 = min(c0 + h_chunk, hidden)
        gate = _nn(h, wi_ref[:, c0:c1])
        up = _nn(h, wi_ref[:, hidden + c0:hidden + c1])
        act_ref[:, c0:c1] = (_silu(gate) * up).astype(BF16)
    x2 = x1 + _nn(act_ref[...], w2_ref[...])
    if final_norm:
        x2 = _rms(x2, nf_ref[...])
    if unpair:
        for r0, k, ln in _tile_units(x2.shape[0]):
            o_ref[k // unpair, ln, k % unpair] = x2[r0:r0 + CHUNK]
    else:
        o_ref[...] = x2


def _conv_act(xp, cw, n_rows):
    nk = CONV_W - 1
    y = cw[nk:CONV_W] * xp[8:8 + n_rows]
    for t in range(nk):
        y = y + cw[t:t + 1] * pltpu.roll(xp, nk - t, 0)[8:8 + n_rows]
    a = _silu(y)
    q, k = a[:, :HC_DK], a[:, HC_DK:2 * HC_DK]
    qn = q * lax.rsqrt(jnp.sum(q * q, axis=-1, keepdims=True) + NORM_EPS) * (HC_DK ** -0.5)
    kn = k * lax.rsqrt(jnp.sum(k * k, axis=-1, keepdims=True) + NORM_EPS)
    return jnp.concatenate([qn, kn, a[:, 2 * HC_DK:]], axis=1)


def _proj_conv_kernel(x_ref, nw_ref, w_ref, cw_ref, o_ref, hd_ref, tl_ref, carry_ref, *, n_chunk, n_conv, segs):
    i = pl.program_id(0)
    wq = 4 * HC_DK
    units = _tile_units(x_ref.shape[0])
    k_last = units[-1][1]

    @pl.when(i == 0)
    def _():
        carry_ref[...] = jnp.zeros_like(carry_ref)

    h = _rms(x_ref[...], nw_ref[...]).astype(BF16)

    def project(n0):
        o_ref[:, n0:n0 + wq] = _nn(h, w_ref[:, n0:n0 + wq])

    project(0)
    for n0 in range(0, n_conv, wq):
        if n0 + wq < n_conv:
            project(n0 + wq)
        cols = slice(n0, n0 + wq)
        cw = cw_ref[:, cols]
        new_carry = {}
        for r0, k, ln in reversed(units):
            xr = o_ref[r0:r0 + CHUNK, cols]
            above = r0 - LANES * CHUNK
            prev8 = carry_ref[ln, :, cols] if k == 0 else o_ref[above + CHUNK - 8:above + CHUNK, cols]
            if k == k_last:
                new_carry[ln] = xr[CHUNK - 8:CHUNK]
            hd_ref[k, ln, :, cols] = xr[0:8]
            tl_ref[k, ln, :, cols] = xr[CHUNK - 8:CHUNK]
            o_ref[r0:r0 + CHUNK, cols] = _conv_act(jnp.concatenate([prev8, xr], axis=0), cw, CHUNK)
        for ln, tail in new_carry.items():
            carry_ref[ln, :, cols] = tail
    for s0, s1, fn in segs:
        for n0 in range(s0, s1, n_chunk):
            n1 = min(n0 + n_chunk, s1)
            o_ref[:, n0:n1] = fn(_nn(h, w_ref[:, n0:n1]))


def _const_spec(shape):
    nd = len(shape)
    return pl.BlockSpec(shape, lambda i, _n=nd: (0,) * _n)


def _proj(x, nw, w, tm, segs, name):
    n = w.shape[1]
    assert segs[0][0] == 0 and segs[-1][1] == n and all(a[1] == b[0] for a, b in zip(segs, segs[1:]))
    if isinstance(x, tuple):
        x_specs, geom, m_rows, d = _x_specs(x, tm)
        x_args, scratch = list(x), [pltpu.VMEM((tm, d), F32)]
    else:
        m_rows, d = x.shape
        x_specs, geom, x_args, scratch = [pl.BlockSpec((tm, d), lambda i: (i, 0))], None, [x], []
    return pl.pallas_call(
        functools.partial(_proj_kernel, n_chunk=512, segs=segs, geom=geom),
        grid=(m_rows // tm,),
        in_specs=x_specs + [_const_spec((1, d)), _const_spec((d, n))],
        out_specs=pl.BlockSpec((tm, n), lambda i: (i, 0)),
        out_shape=jax.ShapeDtypeStruct((m_rows, n), F32),
        scratch_shapes=scratch,
        compiler_params=pltpu.CompilerParams(dimension_semantics=("arbitrary",),
                                             vmem_limit_bytes=VMEM_LIMIT),
        name=name,
    )(*x_args, nw, w)


def _proj_conv(x, nw, w, cw, tm, n_conv, segs, name):
    m_rows, d = x.shape
    n = w.shape[1]
    kt = tm // (LANES * CHUNK)
    assert segs[0][0] == n_conv and segs[-1][1] == n
    unit_spec = pl.BlockSpec((kt, LANES, 8, n_conv), lambda i: (i, 0, 0, 0))
    unit_shape = jax.ShapeDtypeStruct((m_rows // (LANES * CHUNK), LANES, 8, n_conv), F32)
    return pl.pallas_call(
        functools.partial(_proj_conv_kernel, n_chunk=512, n_conv=n_conv, segs=segs),
        grid=(m_rows // tm,),
        in_specs=[pl.BlockSpec((tm, d), lambda i: (i, 0)), _const_spec((1, d)), _const_spec((d, n)),
                  _const_spec((CONV_W, n_conv))],
        out_specs=[pl.BlockSpec((tm, n), lambda i: (i, 0)), unit_spec, unit_spec],
        out_shape=[jax.ShapeDtypeStruct((m_rows, n), F32), unit_shape, unit_shape],
        scratch_shapes=[pltpu.VMEM((LANES, 8, n_conv), F32)],
        compiler_params=pltpu.CompilerParams(dimension_semantics=("arbitrary",),
                                             vmem_limit_bytes=VMEM_LIMIT),
        name=name,
    )(x, nw, w, cw)


def _post(x, mrg, wo, nw, wi, w2, nf, tm, name, final_norm=False, tile0=0, n_tiles=None, unpair_shape=None):
    km = mrg.shape[1]
    hidden = w2.shape[0]
    if isinstance(x, tuple):
        assert tile0 == 0 and n_tiles is None
        x_specs, geom, m_rows, d = _x_specs(x, tm)
        x_args, x_scratch = list(x), [pltpu.VMEM((tm, d), F32)]
    else:
        m_rows, d = x.shape
        x_specs, geom, x_args, x_scratch = [pl.BlockSpec((tm, d), lambda i: (tile0 + i, 0))], None, [x], []
    n_tiles = m_rows // tm if n_tiles is None else n_tiles
    unpair = 0
    if unpair_shape is None:
        out_spec = pl.BlockSpec((tm, d), lambda i: (i, 0))
        out_shape = jax.ShapeDtypeStruct((n_tiles * tm, d), F32)
    else:
        pairs, seq_len = unpair_shape
        assert pairs * seq_len * LANES == n_tiles * tm
        blk, idx, unpair = _pair_block(pairs, seq_len, tm)
        out_spec = pl.BlockSpec(blk + (d,), lambda i: idx(i))
        out_shape = jax.ShapeDtypeStruct((pairs, LANES, seq_len // CHUNK, CHUNK, d), F32)
    return pl.pallas_call(
        functools.partial(_post_kernel, hidden=hidden, h_chunk=256, final_norm=final_norm,
                          unpair=unpair, geom=geom),
        grid=(n_tiles,),
        in_specs=x_specs + [pl.BlockSpec((tm, km), lambda i: (tile0 + i, 0)),
                            _const_spec((km, d)),
                            _const_spec((1, d)),
                            _const_spec((d, 2 * hidden)),
                            _const_spec((hidden, d)),
                            _const_spec((1, d))],
        out_specs=out_spec,
        out_shape=out_shape,
        scratch_shapes=[pltpu.VMEM((tm, hidden), BF16)] + x_scratch,
        compiler_params=pltpu.CompilerParams(dimension_semantics=("arbitrary",),
                                             vmem_limit_bytes=VMEM_LIMIT),
        name=name,
    )(*x_args, mrg, wo, nw, wi, w2, nf)


def _step_tables(prompt_pairs, chunks_prompt, sample_pairs, chunks_sample, n_blocks):
    n_slots = prompt_pairs + sample_pairs + 2
    spare = n_slots - 2
    n_prompt = prompt_pairs * chunks_prompt
    n_sample = sample_pairs * chunks_sample
    rb, slot, fs, insl = [n_prompt + n_sample], [spare], [1], [0]
    for b in range(prompt_pairs):
        for c in range(chunks_prompt):
            rb.append(b * chunks_prompt + c); slot.append(b); fs.append(2 if c == 0 else 0); insl.append(0)
    for s in range(sample_pairs):
        for c in range(chunks_sample):
            rb.append(n_prompt + s * chunks_sample + c); slot.append(prompt_pairs + s)
            fs.append(3 if c == 0 else 0); insl.append(s)
    for blk in range(n_prompt + n_sample + 1, n_blocks):
        rb.append(blk); slot.append(spare + 1); fs.append(1); insl.append(0)
    to = lambda v: jnp.asarray(np.asarray(v, np.int32))
    return to(rb), to(slot), to(fs), to(insl), n_slots


def _hgrn_tables(c):
    t = np.arange(c)[:, None]
    j = np.arange(c)[None, :]
    mats, masks = [], []
    m = c // 2
    while m >= 1:
        blk = t // (2 * m)
        bd = blk * 2 * m + m - 1
        second = (t % (2 * m)) >= m
        mat = np.where(second, (j > bd) & (j <= t), (j > t) & (j <= bd)).astype(np.float32)
        mats.append(mat)
        masks.append((blk == (j // (2 * m))).astype(np.float32))
        m //= 2
    mats.append((j <= t).astype(np.float32))
    masks.append((j == t).astype(np.float32))
    return (jnp.asarray(np.concatenate(mats, 0)), jnp.asarray(np.stack(masks, 0)),
            len(mats) - 1)


def _even_kernel(rb_ref, slot_ref, fs_ref, insl_ref,
                 pa_ref, qb_ref, kb_ref, vb_ref, ob_ref, gt_ref,
                 lbl_ref, hn_ref, mnw_ref, gbias_ref,
                 lvl_ref, lmask_ref, tri_ref, eye_ref,
                 hs_in, mc_in, mn_in, mm_in,
                 mrg_ref, hs_out, mc_out, mn_out, mm_out,
                 meta_hs, meta_mc, meta_mn, meta_mm,
                 *, lb_index, n_levels):
    c = CHUNK
    j = pl.program_id(0)
    fs = fs_ref[j]

    @pl.when(fs == 1)
    def _():
        hs_out[...] = jnp.zeros_like(hs_out)
        mc_out[...] = jnp.zeros_like(mc_out)
        mn_out[...] = jnp.zeros_like(mn_out)
        mm_out[...] = jnp.zeros_like(mm_out)

    @pl.when(fs == 2)
    def _():
        hs_out[...] = meta_hs[...]
        mc_out[...] = meta_mc[...]
        mn_out[...] = meta_mn[...]
        mm_out[...] = meta_mm[...]

    @pl.when(fs == 3)
    def _():
        hs_out[...] = hs_in[...]
        mc_out[...] = mc_in[...]
        mn_out[...] = mn_in[...]
        mm_out[...] = mm_in[...]

    lanes = range(LANES)
    rows = [slice(ln * c, (ln + 1) * c) for ln in lanes]
    hk = HA_HEADS * HA_DK
    ua = [(ln, h) for ln in lanes for h in range(HA_HEADS)]
    ub = [(ln, h) for ln in lanes for h in range(HB_HEADS)]
    hsl = [slice(h * HA_DK, (h + 1) * HA_DK) for h in range(HA_HEADS)]

    lbl = lbl_ref[...]
    ex = jnp.exp(lbl - jnp.max(lbl, axis=0, keepdims=True))
    sm = ex / jnp.sum(ex, axis=0, keepdims=True)
    lb = jnp.sum(sm[:lb_index + 1], axis=0, keepdims=True)
    log_f, k_all, q_all, e_lvl, b_all = [], [], [], [], []
    for ln in lanes:
        sf = pa_ref[rows[ln], hk:2 * hk]
        log_f.append(jnp.log(lb + (1.0 - lb) * sf))
        k_all.append((1.0 - lb) * (1.0 - sf))
        q_all.append(pa_ref[rows[ln], 0:hk])
    for ln in lanes:
        b_all.append(_dot_pieces(_nn, lvl_ref[n_levels * c:(n_levels + 1) * c, :], log_f[ln], 2))
    for lvl in range(n_levels):
        e_lvl.append([_dot_pieces(_nn, lvl_ref[lvl * c:(lvl + 1) * c, :], log_f[ln], 2) for ln in lanes])

    lane = lax.broadcasted_iota(jnp.int32, (c, LANE), 1)
    rowg = lax.broadcasted_iota(jnp.int32, (c, LANE), 0)
    is_f = (lane >= HB_HEADS) & (lane < 2 * HB_HEADS)
    pad = rowg < jnp.where(j == 0, c - N_META, 0)
    ig_t, fc, fr, ir = [], [], [], []
    for ln in lanes:
        gcap = GATE_CAP * jnp.tanh((gt_ref[rows[ln], :] + gbias_ref[...]) / GATE_CAP)
        lf_t = jnp.where(is_f & jnp.logical_not(pad), _log_sigmoid(gcap), 0.0)
        ig_t.append(jnp.where(pad, NEG_BIG, gcap))
        fc.append(_dot_pieces(_nn, tri_ref[...], lf_t, 3))
    for ln in lanes:
        fr.append(_dot_pieces(_nt, eye_ref[...], fc[ln], 3))
        ir.append(_dot_pieces(_nt, eye_ref[...], ig_t[ln], 3))

    row = lax.broadcasted_iota(jnp.int32, (c, HA_DK), 0)
    q_h = [q_all[ln][:, hsl[h]] for ln, h in ua]
    k_h = [k_all[ln][:, hsl[h]] for ln, h in ua]
    v16 = [pa_ref[rows[ln], 2 * hk + h * HA_DV:2 * hk + (h + 1) * HA_DV].astype(BF16) for ln, h in ua]
    b_h = [b_all[ln][:, hsl[h]] for ln, h in ua]
    st = [hs_out[0, ln, h] for ln, h in ua]
    o_inter = [_nt((q_h[u] * jnp.exp(b_h[u])).astype(BF16), st[u].astype(BF16)) for u in range(len(ua))]
    for u, (ln, h) in enumerate(ua):
        b_last = b_h[u][c - 1:c, :]
        k_til = k_h[u] * jnp.exp(b_last - b_h[u])
        hs_out[0, ln, h] = st[u] * jnp.exp(b_last) + _tn(v16[u], k_til.astype(BF16))
    scores = [lmask_ref[n_levels] * _nt(q_h[u].astype(BF16), k_h[u].astype(BF16)) for u in range(len(ua))]
    m = c // 2
    for lvl in range(n_levels):
        second = (row & m) != 0
        for u, (ln, h) in enumerate(ua):
            dec = jnp.exp(e_lvl[lvl][ln][:, hsl[h]])
            qs = jnp.where(second, q_h[u] * dec, 0.0).astype(BF16)
            ks = jnp.where(second, 0.0, k_h[u] * dec).astype(BF16)
            part = _nt(qs, ks)
            scores[u] = scores[u] + (part if lvl == 0 else lmask_ref[lvl] * part)
        m //= 2

    r_i = lax.broadcasted_iota(jnp.int32, (c, c), 0)
    c_i = lax.broadcasted_iota(jnp.int32, (c, c), 1)
    causal = c_i <= r_i
    lane1 = lax.broadcasted_iota(jnp.int32, (1, LANE), 1)
    m_row = [mm_out[0, ln] for ln in lanes]
    m_row_new = list(m_row)
    mq = [qb_ref[rows[ln], h * HB_DQK:(h + 1) * HB_DQK] for ln, h in ub]
    mk = [kb_ref[rows[ln], h * HB_DQK:(h + 1) * HB_DQK] * (HB_DQK ** -0.5) for ln, h in ub]
    mv16 = [vb_ref[rows[ln], h * HB_DV:(h + 1) * HB_DV].astype(BF16) for ln, h in ub]
    ct = [mc_out[0, ln, h] for ln, h in ub]
    n_row = [mn_out[0, ln, h] for ln, h in ub]
    qk_raw = [_nt(mq[u].astype(BF16), mk[u].astype(BF16)) for u in range(len(ub))]
    inter = [_nt(mq[u].astype(BF16), ct[u].astype(BF16)) for u in range(len(ub))]
    qkd, m_ts, w_inters = [], [], []
    for u, (ln, h) in enumerate(ub):
        fl = HB_HEADS + h
        fc_h = fc[ln][:, fl:fl + 1]
        fr_h = fr[ln][fl:fl + 1, :]
        igr_h = ir[ln][h:h + 1, :]
        igc_h = ig_t[ln][:, h:h + 1]
        m_prev = m_row[ln][:, fl:fl + 1]
        log_d = jnp.where(causal, fc_h - fr_h + igr_h, -jnp.inf)
        log_inter = fc_h + m_prev
        m_t = jnp.maximum(log_inter, jnp.max(log_d, axis=1, keepdims=True))
        m_ts.append(m_t)
        w_inters.append(jnp.exp(log_inter - m_t))
        qkd.append(qk_raw[u] * jnp.exp(log_d - m_t))
        f_last = fc_h[c - 1:c, :]
        m_new = m_t[c - 1:c, :]
        w_s = jnp.exp(f_last - fc_h + igc_h - m_new)
        decay = jnp.exp(f_last + m_prev - m_new)
        kw = mk[u] * w_s
        mc_out[0, ln, h] = decay * ct[u] + _tn(mv16[u], kw.astype(BF16))
        mn_out[0, ln, h] = decay * n_row[u] + jnp.sum(kw, axis=0, keepdims=True)
        m_row_new[ln] = jnp.where(lane1 == fl, m_new, m_row_new[ln])
    for ln in lanes:
        mm_out[0, ln] = m_row_new[ln]

    o_a = [_nn(scores[u].astype(BF16), v16[u]) + o_inter[u] for u in range(len(ua))]
    num = [_nn(qkd[u].astype(BF16), mv16[u]) + w_inters[u] * inter[u] for u in range(len(ub))]
    for u, (ln, h) in enumerate(ua):
        ga = pa_ref[rows[ln], 3 * hk + h * HA_DV:3 * hk + (h + 1) * HA_DV]
        mrg_ref[rows[ln], h * HA_DV:(h + 1) * HA_DV] = (_rms(o_a[u], hn_ref[:, hsl[h]]) * ga).astype(BF16)
    for u, (ln, h) in enumerate(ub):
        den = jnp.sum(qkd[u], axis=1, keepdims=True) \
            + w_inters[u] * jnp.sum(mq[u] * n_row[u], axis=1, keepdims=True)
        hh = num[u] / jnp.maximum(jnp.abs(den), jnp.exp(-m_ts[u]))
        ob = ob_ref[rows[ln], h * HB_DV:(h + 1) * HB_DV]
        col0 = HA_HEADS * HA_DV + h * HB_DV
        mrg_ref[rows[ln], col0:col0 + HB_DV] = (
            _rms(hh, mnw_ref[:, h * HB_DV:(h + 1) * HB_DV]) * ob).astype(BF16)

    @pl.when(j == 0)
    def _():
        meta_hs[...] = hs_out[...]
        meta_mc[...] = mc_out[...]
        meta_mn[...] = mn_out[...]
        meta_mm[...] = mm_out[...]


def _even_mixer(p, tables, lb_logits, lb_index, hgrn_norm, mlstm_norm, gbias, hs_in, mc_in, mn_in, mm_in):
    rb, slot, fs, insl, n_slots = tables
    c = CHUNK
    rows = LANES * c
    n_steps = rb.shape[0]
    lvl, lmask, n_levels = _hgrn_tables(c)
    tri = jnp.asarray(np.tril(np.ones((c, c), np.float32)))
    eye = jnp.eye(LANE, dtype=F32)
    hk = HA_HEADS * HA_DK
    wa = 4 * hk
    qk_w = HB_HEADS * HB_DQK
    v_w = HB_HEADS * HB_DV
    g_col = wa + 2 * qk_w + 2 * v_w
    cst = lambda shape: pl.BlockSpec(shape, lambda j, *_: (0,) * len(shape))
    row_blk = lambda w, idx: pl.BlockSpec((rows, w), lambda j, r, *_: (r[j], idx))
    st_in = lambda shape: pl.BlockSpec((1,) + shape, lambda j, r, s, f, i: (i[j],) + (0,) * len(shape))
    st_out = lambda shape: pl.BlockSpec((1,) + shape, lambda j, r, s, f, i: (s[j],) + (0,) * len(shape))
    sh_hs = (LANES, HA_HEADS, HA_DV, HA_DK)
    sh_mc = (LANES, HB_HEADS, HB_DV, HB_DQK)
    sh_mn = (LANES, HB_HEADS, 1, HB_DQK)
    sh_mm = (LANES, 1, LANE)
    grid_spec = pltpu.PrefetchScalarGridSpec(
        num_scalar_prefetch=4,
        grid=(n_steps,),
        in_specs=[row_blk(wa, 0),
                  row_blk(qk_w, wa // qk_w), row_blk(qk_w, wa // qk_w + 1),
                  row_blk(v_w, (wa + 2 * qk_w) // v_w), row_blk(v_w, (wa + 2 * qk_w) // v_w + 1),
                  row_blk(LANE, g_col // LANE),
                  cst(lb_logits.shape), cst((1, hk)), cst((1, v_w)), cst((1, LANE)),
                  cst(lvl.shape), cst(lmask.shape), cst((c, c)), cst((LANE, LANE)),
                  st_in(sh_hs), st_in(sh_mc), st_in(sh_mn), st_in(sh_mm)],
        out_specs=[pl.BlockSpec((rows, hk + v_w), lambda j, r, *_: (r[j], 0)),
                   st_out(sh_hs), st_out(sh_mc), st_out(sh_mn), st_out(sh_mm)],
        scratch_shapes=[pltpu.VMEM((1,) + sh_hs, F32), pltpu.VMEM((1,) + sh_mc, F32),
                        pltpu.VMEM((1,) + sh_mn, F32), pltpu.VMEM((1,) + sh_mm, F32)],
    )
    out_shape = [jax.ShapeDtypeStruct((p.shape[0], hk + v_w), BF16),
                 jax.ShapeDtypeStruct((n_slots,) + sh_hs, F32),
                 jax.ShapeDtypeStruct((n_slots,) + sh_mc, F32),
                 jax.ShapeDtypeStruct((n_slots,) + sh_mn, F32),
                 jax.ShapeDtypeStruct((n_slots,) + sh_mm, F32)]
    return pl.pallas_call(
        functools.partial(_even_kernel, lb_index=lb_index, n_levels=n_levels),
        grid_spec=grid_spec,
        out_shape=out_shape,
        compiler_params=pltpu.CompilerParams(dimension_semantics=("arbitrary",),
                                             vmem_limit_bytes=VMEM_LIMIT),
        name="even_mixer",
    )(rb, slot, fs, insl, p, p, p, p, p, p, lb_logits, hgrn_norm, mlstm_norm, gbias,
      lvl, lmask, tri, eye, hs_in, mc_in, mn_in, mm_in)


def _gdn_kernel(rb_ref, slot_ref, fs_ref, insl_ref,
                qkv_ref, z_ref, gt_ref, hd_ref, mt_ref, cw_ref, alog_ref, dtb_ref, gnw_ref,
                tri_ref, s_in, cv_in,
                mrg_ref, s_out,
                meta_s, *, n_sq, hps):
    c = CHUNK
    hg = pl.program_id(0)
    j = pl.program_id(1)
    fs = fs_ref[j]
    nk = CONV_W - 1

    @pl.when(fs == 1)
    def _():
        s_out[...] = jnp.zeros_like(s_out)

    @pl.when(fs == 2)
    def _():
        s_out[...] = meta_s[...]

    @pl.when(fs == 3)
    def _():
        s_out[...] = s_in[...]

    lanes = range(LANES)
    rows = [slice(ln * c, (ln + 1) * c) for ln in lanes]
    lane = lax.broadcasted_iota(jnp.int32, (c, LANE), 1)
    row = lax.broadcasted_iota(jnp.int32, (c, LANE), 0)
    first = lane < c
    s_idx = jnp.where(first, lane, lane - c)
    causal = s_idx <= row
    strict = s_idx < row
    diag = s_idx == row
    eye_c = diag.astype(F32)
    r2 = lax.broadcasted_iota(jnp.int32, (2 * c, LANE), 0)
    l2 = lax.broadcasted_iota(jnp.int32, (2 * c, LANE), 1)
    same_head = (r2 < c) == (l2 < c)
    wq = 4 * HC_DK

    def col(tile, idx):
        return jnp.sum(jnp.where(lane == idx, tile, 0.0), axis=1, keepdims=True)

    def block_diag(m16):
        return jnp.where(same_head, jnp.concatenate([m16, m16], axis=0), jnp.zeros_like(m16[:1, :1]))

    from_meta = (fs == 2).astype(F32)
    from_state = (fs == 3).astype(F32)
    n_redo = jnp.where(fs != 0, nk, 0)
    row8 = lax.broadcasted_iota(jnp.int32, (8, hps * wq), 0)
    act, beta_t, gc = [], [], []
    for ln in lanes:
        state8 = jnp.concatenate([jnp.zeros((8 - nk, hps * wq), F32), cv_in[0, ln]], axis=0)
        xp = jnp.concatenate([from_meta * mt_ref[0, ln] + from_state * state8, hd_ref[0, ln]], axis=0)
        redo = jnp.concatenate([_conv_act(xp[:, p * wq:(p + 1) * wq], cw_ref[:, p * wq:(p + 1) * wq], 8)
                                for p in range(hps)], axis=1)
        a_ln = qkv_ref[rows[ln], :]
        act.append(jnp.concatenate([jnp.where(row8 < n_redo, redo, a_ln[0:8]), a_ln[8:]], axis=0))
        gt = gt_ref[rows[ln], :]
        beta_t.append(_sigmoid(gt))
        g_t = -jnp.exp(alog_ref[...]) * _softplus(gt + dtb_ref[...])
        gc.append(_dot_pieces(_nn, tri_ref[...], g_t, 3))

    units = [(ln, p) for ln in lanes for p in range(hps)]
    nu = range(len(units))
    qn, kn, qk, kk = [], [], [], []
    for ln, p in units:
        qn.append(act[ln][:, p * wq:p * wq + HC_DK])
        kn.append(act[ln][:, p * wq + HC_DK:p * wq + 2 * HC_DK])
    for u in nu:
        kn16 = kn[u].astype(BF16)
        k2 = jnp.concatenate([kn16, kn16], axis=0)
        qk.append(_nt(qn[u].astype(BF16), k2))
        kk.append(_nt(kn16, k2))

    beta, gcol, decay, inv, pw = [], [], [], [], []
    for u, (ln, p) in enumerate(units):
        hv = 2 * (hg * hps + p)
        b_a, b_b = col(beta_t[ln], hv), col(beta_t[ln], hv + 1)
        g_a, g_b = col(gc[ln], HC_V_HEADS + hv), col(gc[ln], HC_V_HEADS + hv + 1)
        beta.append((b_a, b_b))
        gcol.append((g_a, g_b))
        g_c = jnp.where(first, g_a, g_b)
        g_r = jnp.sum(jnp.where(diag, g_c, 0.0), axis=0, keepdims=True)
        decay.append(jnp.exp(jnp.where(causal, g_c - g_r, -jnp.inf)))
        low = jnp.where(strict, jnp.where(first, b_a, b_b) * kk[u] * decay[u], 0.0)
        inv.append(eye_c - low)
        pw.append(low.astype(BF16))
    pw = [_nn(x, block_diag(x)).astype(BF16) for x in pw]
    for i in range(n_sq):
        if i < n_sq - 1:
            prod = [_nn(jnp.concatenate([x.astype(BF16), y], axis=0), block_diag(y)) for x, y in zip(inv, pw)]
            inv = [x + r[:c] for x, r in zip(inv, prod)]
            pw = [r[c:].astype(BF16) for r in prod]
        else:
            inv = [x + _nn(x.astype(BF16), block_diag(y)) for x, y in zip(inv, pw)]

    sol, eg = [], []
    for u, (ln, p) in enumerate(units):
        (b_a, b_b), (g_a, g_b) = beta[u], gcol[u]
        e_a, e_b = jnp.exp(g_a), jnp.exp(g_b)
        eg.append((e_a, e_b))
        v_a = act[ln][:, p * wq + 2 * HC_DK:p * wq + 3 * HC_DK]
        v_b = act[ln][:, p * wq + 3 * HC_DK:p * wq + 4 * HC_DK]
        rhs = jnp.concatenate([jnp.concatenate([v_a * b_a, kn[u] * (b_a * e_a)], axis=1),
                               jnp.concatenate([v_b * b_b, kn[u] * (b_b * e_b)], axis=1)],
                              axis=0).astype(BF16)
        sol.append((_nn(jnp.where(first, inv[u], 0.0).astype(BF16), rhs),
                    _nn(jnp.where(first, 0.0, inv[u]).astype(BF16), rhs)))
    s_old = [[s_out[0, ln, 2 * p + jj] for jj in range(2)] for ln, p in units]
    ws = []
    for u in nu:
        for jj in range(2):
            lhs = jnp.concatenate([sol[u][jj][:, HC_DV:], qn[u] * eg[u][jj]], axis=0)
            ws.append(_nn(lhs.astype(BF16), s_old[u][jj].astype(BF16)))
    for u, (ln, p) in enumerate(units):
        v_new = [sol[u][jj][:, :HC_DV] - ws[2 * u + jj][:c] for jj in range(2)]
        v_st = jnp.concatenate(v_new, axis=0).astype(BF16)
        attn = qk[u] * decay[u]
        o = (ws[2 * u][c:] + _nn(jnp.where(first, attn, 0.0).astype(BF16), v_st),
             ws[2 * u + 1][c:] + _nn(jnp.where(first, 0.0, attn).astype(BF16), v_st))
        for jj in range(2):
            n = 2 * p + jj
            g_h = gcol[u][jj]
            g_last = g_h[c - 1:c, :]
            k_til = kn[u] * jnp.exp(g_last - g_h)
            s_out[0, ln, n] = jnp.exp(g_last) * s_old[u][jj] + _tn(k_til.astype(BF16), v_new[jj].astype(BF16))
            z = z_ref[rows[ln], n * HC_DV:(n + 1) * HC_DV]
            mrg_ref[rows[ln], n * HC_DV:(n + 1) * HC_DV] = (_rms(o[jj], gnw_ref[...]) * z).astype(BF16)

    @pl.when(j == 0)
    def _():
        meta_s[...] = s_out[...]


def _gdn_mixer(p, heads, tails, meta_blk, tables, conv_w, alog_row, dtb_row, gnw, s_in, cv_in, hps=8):
    rb, slot, fs, insl, n_slots = tables
    c = CHUNK
    rows = LANES * c
    n_steps = rb.shape[0]
    n_hg = HC_QK_HEADS // hps
    wq = hps * 4 * HC_DK
    wz = hps * 2 * HC_DV
    qkv_w = HC_QK_HEADS * 4 * HC_DK
    z_w = HC_V_HEADS * HC_DV
    n_sq = int(np.log2(c)) - 1
    tri = jnp.asarray(np.tril(np.ones((c, c), np.float32)))
    nk = CONV_W - 1
    sh_s = (1, LANES, 2 * hps, HC_DK, HC_DV)
    sh_cv = (1, LANES, nk, wq)
    sh_u = (1, LANES, 8, wq)
    cst = lambda shape: pl.BlockSpec(shape, lambda h, j, *_: (0,) * len(shape))
    grid_spec = pltpu.PrefetchScalarGridSpec(
        num_scalar_prefetch=4,
        grid=(n_hg, n_steps),
        in_specs=[pl.BlockSpec((rows, wq), lambda h, j, r, *_: (r[j], h)),
                  pl.BlockSpec((rows, wz), lambda h, j, r, *_: (r[j], qkv_w // wz + h)),
                  pl.BlockSpec((rows, LANE), lambda h, j, r, *_: (r[j], (qkv_w + z_w) // LANE)),
                  pl.BlockSpec(sh_u, lambda h, j, r, *_: (r[j], 0, 0, h)),
                  pl.BlockSpec(sh_u, lambda h, j, *_: (meta_blk, 0, 0, h)),
                  pl.BlockSpec((CONV_W, wq), lambda h, j, *_: (0, h)),
                  cst((1, LANE)), cst((1, LANE)), cst((1, HC_DV)),
                  cst((c, c)),
                  pl.BlockSpec(sh_s, lambda h, j, r, s, f, i: (i[j], 0, h, 0, 0)),
                  pl.BlockSpec(sh_cv, lambda h, j, r, s, f, i: (i[j], 0, 0, h))],
        out_specs=[pl.BlockSpec((rows, wz), lambda h, j, r, *_: (r[j], h)),
                   pl.BlockSpec(sh_s, lambda h, j, r, s, f, i: (s[j], 0, h, 0, 0))],
        scratch_shapes=[pltpu.VMEM(sh_s, F32)],
    )
    out_shape = [jax.ShapeDtypeStruct((p.shape[0], z_w), BF16),
                 jax.ShapeDtypeStruct((n_slots, LANES, HC_V_HEADS, HC_DK, HC_DV), F32)]
    return pl.pallas_call(
        functools.partial(_gdn_kernel, n_sq=n_sq, hps=hps),
        grid_spec=grid_spec,
        out_shape=out_shape,
        compiler_params=pltpu.CompilerParams(dimension_semantics=("arbitrary", "arbitrary"),
                                             vmem_limit_bytes=VMEM_LIMIT),
        name="gdn_mixer",
    )(rb, slot, fs, insl, p, p, p, heads, tails, conv_w, alog_row, dtb_row, gnw, tri, s_in, cv_in)


def _pair_cols(a):
    nq = HC_QK_HEADS * HC_DK
    lead = a.shape[:-1]
    q = a[..., :nq].reshape(lead + (HC_QK_HEADS, HC_DK))
    k = a[..., nq:2 * nq].reshape(lead + (HC_QK_HEADS, HC_DK))
    v = a[..., 2 * nq:].reshape(lead + (HC_QK_HEADS, 2 * HC_DV))
    return jnp.concatenate([q, k, v], axis=-1).reshape(lead + (-1,))


def _unpair_cols(a):
    lead = a.shape[:-1]
    g = a.reshape(lead + (HC_QK_HEADS, 2 * HC_DK + 2 * HC_DV))
    parts = [g[..., :HC_DK], g[..., HC_DK:2 * HC_DK], g[..., 2 * HC_DK:]]
    return jnp.concatenate([x.reshape(lead + (-1,)) for x in parts], axis=-1)


def _pad_cols(w, width):
    return jnp.pad(w, ((0, 0), (0, width - w.shape[1])))


def kernel(x_prompt, x_sample, state_hgrn_S, state_mlstm_C, state_mlstm_n, state_mlstm_m, state_gdn_S,
           state_gdn_conv, meta_tokens, norm_mix, norm_ffn, norm_final, even_w_in, even_w_out,
           hgrn_lb_logits, hgrn_norm, mlstm_b_i, mlstm_b_f, mlstm_norm, odd_w_in, odd_conv_w, gdn_a_log,
           gdn_dt_bias, gdn_norm, odd_w_out, ffn_w_in, ffn_w_out):
    n_b, t_len, d = x_prompt.shape
    n_s, l_s, _ = x_sample.shape
    c = CHUNK
    blk = LANES * c
    assert t_len % c == 0 and l_s % c == 0 and meta_tokens.shape[0] == N_META
    assert n_b % LANES == 0 and n_s % LANES == 0, "sequences are processed in pairs"
    assert norm_mix.shape[0] == 2, "one even (HGRN2+mLSTM) and one odd (DeltaNet) layer"
    dt = x_prompt.dtype
    rows_p, rows_s = n_b * t_len, n_s * l_s
    tm = min(DENSE_TM, int(np.gcd(rows_p, rows_s)))
    assert tm % blk == 0
    m_rows = rows_p + rows_s + tm

    meta_job = jnp.concatenate([jnp.zeros((c - N_META, d), dt), meta_tokens.astype(dt)], axis=0)
    x_meta = jnp.concatenate([meta_job, meta_job, jnp.zeros((tm - blk, d), dt)], axis=0)
    x = (x_prompt.reshape(n_b // LANES, LANES, t_len // c, c, d),
         x_sample.reshape(n_s // LANES, LANES, l_s // c, c, d), x_meta)
    tables = _step_tables(n_b // LANES, t_len // c, n_s // LANES, l_s // c, m_rows // blk)
    row = lambda v: v.reshape(1, -1).astype(F32)
    pair = lambda a: a.reshape((a.shape[0] // LANES, LANES) + a.shape[1:])

    e = 0
    n_main = 4 * HA_HEADS * HA_DK + 2 * HB_HEADS * HB_DQK + 2 * HB_HEADS * HB_DV
    w_even = jnp.concatenate([even_w_in[e][:, :n_main], _pad_cols(even_w_in[e][:, n_main:], LANE)],
                             axis=1).astype(BF16)
    hk, hv = HA_HEADS * HA_DK, HA_HEADS * HA_DV
    mq, mv = HB_HEADS * HB_DQK, HB_HEADS * HB_DV
    ident = lambda v: v
    segs_even = ((0, hk, _silu), (hk, 2 * hk, _sigmoid), (2 * hk, 2 * hk + hv, ident),
                 (2 * hk + hv, 2 * hk + 2 * hv, _silu),
                 (2 * hk + 2 * hv, n_main - mv, ident), (n_main - mv, n_main, _sigmoid),
                 (n_main, n_main + LANE, ident))
    p_even = _proj(x, row(norm_mix[0]), w_even, tm, segs_even, "proj_even")
    gbias = _pad_cols(jnp.concatenate([mlstm_b_i[e], mlstm_b_f[e]]).reshape(1, -1).astype(F32), LANE)
    hs_in = pair(jnp.swapaxes(state_hgrn_S[e].astype(F32), -1, -2))
    mc_in = pair(jnp.swapaxes(state_mlstm_C[e].astype(F32), -1, -2))
    mn_in = pair(state_mlstm_n[e].astype(F32)[:, :, None, :])
    mm_in = pair(jnp.pad(state_mlstm_m[e].astype(F32), ((0, 0), (HB_HEADS, LANE - 2 * HB_HEADS)))[:, None, :])
    mrg0, hs_o, mc_o, mn_o, mm_o = _even_mixer(
        p_even, tables, hgrn_lb_logits.astype(F32), e, row(hgrn_norm[e]), row(mlstm_norm[e]), gbias,
        hs_in, mc_in, mn_in, mm_in)
    x = _post(x, mrg0, even_w_out[e].astype(BF16), row(norm_ffn[0]), ffn_w_in[0].astype(BF16),
              ffn_w_out[0].astype(BF16), row(norm_final), tm, "post_even")

    o = 0
    n_qkv = 2 * HC_QK_HEADS * HC_DK + HC_V_HEADS * HC_DV
    n_z = HC_V_HEADS * HC_DV
    w_odd = jnp.concatenate([_pair_cols(odd_w_in[o][:, :n_qkv]), odd_w_in[o][:, n_qkv:n_qkv + n_z],
                             _pad_cols(odd_w_in[o][:, n_qkv + n_z:], LANE)], axis=1).astype(BF16)
    segs_odd = ((n_qkv, n_qkv + n_z, _silu), (n_qkv + n_z, n_qkv + n_z + LANE, ident))
    conv_w = _pair_cols(odd_conv_w[o].astype(F32))
    p_odd, heads, tails = _proj_conv(x, row(norm_mix[1]), w_odd, conv_w, tm, n_qkv, segs_odd, "proj_odd")
    lane_pad = lambda v: jnp.pad(v.reshape(1, -1).astype(F32),
                                 ((0, 0), (HC_V_HEADS, LANE - 2 * HC_V_HEADS)))
    np_, ns_ = n_b // LANES, n_s // LANES
    cp_, cs_ = t_len // c, l_s // c
    mrg1, gs_o = _gdn_mixer(
        p_odd, heads, tails, np_ * cp_ + ns_ * cs_, tables, conv_w, lane_pad(gdn_a_log[o]),
        lane_pad(gdn_dt_bias[o]), row(gdn_norm[o]), pair(state_gdn_S[o].astype(F32)),
        pair(_pair_cols(state_gdn_conv[o].astype(F32))))
    last_blk = np.asarray([b * cp_ + cp_ - 1 for b in range(np_)]
                          + [np_ * cp_ + q * cs_ + cs_ - 1 for q in range(ns_)], np.int32)
    cv_o = tails[last_blk][:, :, 8 - (CONV_W - 1):, :]
    post_odd = functools.partial(_post, x, mrg1, odd_w_out[o].astype(BF16), row(norm_ffn[1]),
                                 ffn_w_in[1].astype(BF16), ffn_w_out[1].astype(BF16), row(norm_final), tm,
                                 final_norm=True)
    y_p = post_odd("post_odd_prompt", tile0=0, n_tiles=rows_p // tm, unpair_shape=(n_b // LANES, t_len))
    y_s = post_odd("post_odd_sample", tile0=rows_p // tm, n_tiles=rows_s // tm,
                   unpair_shape=(n_s // LANES, l_s))

    y_prompt = y_p.reshape(n_b, t_len, d)
    y_sample = y_s.reshape(n_s, l_s, d)
    hs_o = jnp.swapaxes(hs_o, -1, -2)
    mc_o = jnp.swapaxes(mc_o, -1, -2)
    mn_o = mn_o[:, :, :, 0, :]
    mm_o = mm_o[:, :, 0, HB_HEADS:2 * HB_HEADS]
    cv_o = _unpair_cols(cv_o)
    stk = lambda a, lo, hi: a[lo:hi].reshape((-1,) + a.shape[2:])[None].astype(dt)
    outs = [y_prompt, y_sample]
    for lo, hi in ((0, np_), (np_, np_ + ns_)):
        outs += [stk(hs_o, lo, hi), stk(mc_o, lo, hi), stk(mn_o, lo, hi), stk(mm_o, lo, hi),
                 stk(gs_o, lo, hi), stk(cv_o, lo, hi)]
    return tuple(outs)
```

```python
import functools

import numpy as np
import jax
import jax.numpy as jnp
from jax import lax
from jax.experimental import pallas as pl
from jax.experimental.pallas import tpu as pltpu

F32 = jnp.float32
BF16 = jnp.bfloat16

CHUNK = 64
LANES = 2
N_META = 16
NORM_EPS = 1e-6
GATE_CAP = 15.0
NEG_BIG = -1e30

HA_HEADS, HA_DK, HA_DV = 4, 128, 128
HB_HEADS, HB_DQK, HB_DV = 4, 64, 128
HC_QK_HEADS, HC_V_HEADS, HC_DK, HC_DV = 8, 16, 128, 128
CONV_W = 4
LANE = 128
DENSE_TM = 512
VMEM_LIMIT = 56 * 1024 * 1024


def _nn(a, b):
    return lax.dot_general(a, b, (((1,), (0,)), ((), ())), preferred_element_type=F32)


def _nt(a, b):
    return lax.dot_general(a, b, (((1,), (1,)), ((), ())), preferred_element_type=F32)


def _tn(a, b):
    return lax.dot_general(a, b, (((0,), (0,)), ((), ())), preferred_element_type=F32)


def _dot_pieces(dot, exact, x, n):
    e16 = exact.astype(BF16)
    acc = None
    for _ in range(n):
        piece = x.astype(BF16)
        part = dot(e16, piece)
        acc = part if acc is None else acc + part
        x = x - piece.astype(F32)
    return acc


def _sigmoid(x):
    return 1.0 / (1.0 + jnp.exp(-x))


def _silu(x):
    return x * _sigmoid(x)


def _softplus(x):
    return jnp.maximum(x, 0.0) + jnp.log1p(jnp.exp(-jnp.abs(x)))


def _log_sigmoid(x):
    return jnp.minimum(x, 0.0) - jnp.log1p(jnp.exp(-jnp.abs(x)))


def _rms(x, w):
    ms = jnp.mean(x * x, axis=-1, keepdims=True)
    return x * lax.rsqrt(ms + NORM_EPS) * w


def _pair_block(pairs, seq_len, tm):
    chunks = seq_len // CHUNK
    k_tile = tm // (LANES * CHUNK)
    if chunks % k_tile == 0:
        per_pair = chunks // k_tile
        return (1, LANES, k_tile, CHUNK), (lambda t: (t // per_pair, 0, t % per_pair, 0, 0)), k_tile
    assert k_tile % chunks == 0
    return (k_tile // chunks, LANES, chunks, CHUNK), (lambda t: (t, 0, 0, 0, 0)), chunks


def _tile_units(n_rows):
    return [((k * LANES + ln) * CHUNK, k, ln) for k in range(n_rows // (LANES * CHUNK)) for ln in range(LANES)]


def _load_x_tile(i, xp_ref, xs_ref, xm_ref, xt_ref, geom):
    n_pt, n_st, kc_p, kc_s = geom

    def fill(src, kc):
        for r0, k, ln in _tile_units(xt_ref.shape[0]):
            xt_ref[r0:r0 + CHUNK, :] = src[k // kc, ln, k % kc]

    @pl.when(i < n_pt)
    def _():
        fill(xp_ref, kc_p)

    @pl.when((i >= n_pt) & (i < n_pt + n_st))
    def _():
        fill(xs_ref, kc_s)

    @pl.when(i >= n_pt + n_st)
    def _():
        xt_ref[...] = xm_ref[...]


def _x_specs(xsrc, tm):
    xp5, xs5, xm = xsrc
    d = xp5.shape[-1]
    blk_p, idx_p, kc_p = _pair_block(xp5.shape[0], xp5.shape[2] * CHUNK, tm)
    blk_s, idx_s, kc_s = _pair_block(xs5.shape[0], xs5.shape[2] * CHUNK, tm)
    n_pt = xp5.shape[0] * xp5.shape[2] * LANES * CHUNK // tm
    n_st = xs5.shape[0] * xs5.shape[2] * LANES * CHUNK // tm
    specs = [pl.BlockSpec(blk_p + (d,), lambda i: idx_p(jnp.minimum(i, n_pt - 1))),
             pl.BlockSpec(blk_s + (d,), lambda i: idx_s(jnp.clip(i - n_pt, 0, n_st - 1))),
             pl.BlockSpec((tm, d), lambda i: (0, 0))]
    return specs, (n_pt, n_st, kc_p, kc_s), (n_pt + n_st + 1) * tm, d


def _proj_kernel(*refs, n_chunk, segs, geom):
    if geom is None:
        x_ref, nw_ref, w_ref, o_ref = refs
    else:
        xp_ref, xs_ref, xm_ref, nw_ref, w_ref, o_ref, x_ref = refs
        _load_x_tile(pl.program_id(0), xp_ref, xs_ref, xm_ref, x_ref, geom)
    h = _rms(x_ref[...], nw_ref[...]).astype(BF16)
    for s0, s1, fn in segs:
        for n0 in range(s0, s1, n_chunk):
            n1 = min(n0 + n_chunk, s1)
            o_ref[:, n0:n1] = fn(_nn(h, w_ref[:, n0:n1]))


def _post_kernel(*refs, hidden, h_chunk, final_norm, unpair, geom):
    if geom is None:
        x_ref, m_ref, wo_ref, nw_ref, wi_ref, w2_ref, nf_ref, o_ref, act_ref = refs
    else:
        xp_ref, xs_ref, xm_ref, m_ref, wo_ref, nw_ref, wi_ref, w2_ref, nf_ref, o_ref, act_ref, x_ref = refs
        _load_x_tile(pl.program_id(0), xp_ref, xs_ref, xm_ref, x_ref, geom)
    x1 = x_ref[...] + _nn(m_ref[...], wo_ref[...])
    h = _rms(x1, nw_ref[...]).astype(BF16)
    for c0 in range(0, hidden, h_chunk):
        c1 = min(c0 + h_chunk, hidden)
        gate = _nn(h, wi_ref[:, c0:c1])
        up = _nn(h, wi_ref[:, hidden + c0:hidden + c1])
        act_ref[:, c0:c1] = (_silu(gate) * up).astype(BF16)
    x2 = x1 + _nn(act_ref[...], w2_ref[...])
    if final_norm:
        x2 = _rms(x2, nf_ref[...])
    if unpair:
        for r0, k, ln in _tile_units(x2.shape[0]):
            o_ref[k // unpair, ln, k % unpair] = x2[r0:r0 + CHUNK]
    else:
        o_ref[...] = x2


def _const_spec(shape):
    nd = len(shape)
    return pl.BlockSpec(shape, lambda i, _n=nd: (0,) * _n)


def _proj(x, nw, w, tm, segs, name):
    n = w.shape[1]
    assert segs[0][0] == 0 and segs[-1][1] == n and all(a[1] == b[0] for a, b in zip(segs, segs[1:]))
    if isinstance(x, tuple):
        x_specs, geom, m_rows, d = _x_specs(x, tm)
        x_args, scratch = list(x), [pltpu.VMEM((tm, d), F32)]
    else:
        m_rows, d = x.shape
        x_specs, geom, x_args, scratch = [pl.BlockSpec((tm, d), lambda i: (i, 0))], None, [x], []
    return pl.pallas_call(
        functools.partial(_proj_kernel, n_chunk=512, segs=segs, geom=geom),
        grid=(m_rows // tm,),
        in_specs=x_specs + [_const_spec((1, d)), _const_spec((d, n))],
        out_specs=pl.BlockSpec((tm, n), lambda i: (i, 0)),
        out_shape=jax.ShapeDtypeStruct((m_rows, n), F32),
        scratch_shapes=scratch,
        compiler_params=pltpu.CompilerParams(dimension_semantics=("arbitrary",),
                                             vmem_limit_bytes=VMEM_LIMIT),
        name=name,
    )(*x_args, nw, w)


def _post(x, mrg, wo, nw, wi, w2, nf, tm, name, final_norm=False, tile0=0, n_tiles=None, unpair_shape=None):
    (wo, l_o), (wi, l_i), (w2, l_2) = wo, wi, w2
    km = mrg.shape[1]
    hidden = w2.shape[1]
    layer_spec = lambda w, l: pl.BlockSpec((None,) + w.shape[1:], lambda i: (l, 0, 0))
    if isinstance(x, tuple):
        assert tile0 == 0 and n_tiles is None
        x_specs, geom, m_rows, d = _x_specs(x, tm)
        x_args, x_scratch = list(x), [pltpu.VMEM((tm, d), F32)]
    else:
        m_rows, d = x.shape
        x_specs, geom, x_args, x_scratch = [pl.BlockSpec((tm, d), lambda i: (tile0 + i, 0))], None, [x], []
    n_tiles = m_rows // tm if n_tiles is None else n_tiles
    unpair = 0
    if unpair_shape is None:
        out_spec = pl.BlockSpec((tm, d), lambda i: (i, 0))
        out_shape = jax.ShapeDtypeStruct((n_tiles * tm, d), F32)
    else:
        pairs, seq_len = unpair_shape
        assert pairs * seq_len * LANES == n_tiles * tm
        blk, idx, unpair = _pair_block(pairs, seq_len, tm)
        out_spec = pl.BlockSpec(blk + (d,), lambda i: idx(i))
        out_shape = jax.ShapeDtypeStruct((pairs, LANES, seq_len // CHUNK, CHUNK, d), F32)
    return pl.pallas_call(
        functools.partial(_post_kernel, hidden=hidden, h_chunk=256, final_norm=final_norm,
                          unpair=unpair, geom=geom),
        grid=(n_tiles,),
        in_specs=x_specs + [pl.BlockSpec((tm, km), lambda i: (tile0 + i, 0)),
                            layer_spec(wo, l_o),
                            _const_spec((1, d)),
                            layer_spec(wi, l_i),
                            layer_spec(w2, l_2),
                            _const_spec((1, d))],
        out_specs=out_spec,
        out_shape=out_shape,
        scratch_shapes=[pltpu.VMEM((tm, hidden), BF16)] + x_scratch,
        compiler_params=pltpu.CompilerParams(dimension_semantics=("arbitrary",),
                                             vmem_limit_bytes=VMEM_LIMIT),
        name=name,
    )(*x_args, mrg, wo, nw, wi, w2, nf)


def _step_tables(prompt_pairs, chunks_prompt, sample_pairs, chunks_sample, n_blocks):
    n_prompt = prompt_pairs * chunks_prompt
    n_sample = sample_pairs * chunks_sample
    rb, fs, insl, emit, pidx, sidx = [n_prompt + n_sample], [1], [0], [0], [0], [0]
    for b in range(prompt_pairs):
        for c in range(chunks_prompt):
            rb.append(b * chunks_prompt + c); fs.append(2 if c == 0 else 0); insl.append(0)
            emit.append(1 if c == chunks_prompt - 1 else 0); pidx.append(b); sidx.append(0)
    for q in range(sample_pairs):
        for c in range(chunks_sample):
            rb.append(n_prompt + q * chunks_sample + c); fs.append(3 if c == 0 else 0); insl.append(q)
            emit.append(2 if c == chunks_sample - 1 else 0); pidx.append(prompt_pairs - 1); sidx.append(q)
    for blk in range(n_prompt + n_sample + 1, n_blocks):
        rb.append(blk); fs.append(1); insl.append(0)
        emit.append(0); pidx.append(prompt_pairs - 1); sidx.append(sample_pairs - 1)
    to = lambda v: jnp.asarray(np.asarray(v, np.int32))
    return tuple(to(v) for v in (rb, fs, insl, emit, pidx, sidx))


def _hgrn_tables(c):
    t = np.arange(c)[:, None]
    j = np.arange(c)[None, :]
    mats, masks = [], []
    m = c // 2
    while m >= 1:
        blk = t // (2 * m)
        bd = blk * 2 * m + m - 1
        second = (t % (2 * m)) >= m
        mat = np.where(second, (j > bd) & (j <= t), (j > t) & (j <= bd)).astype(np.float32)
        mats.append(mat)
        masks.append((blk == (j // (2 * m))).astype(np.float32))
        m //= 2
    mats.append((j <= t).astype(np.float32))
    masks.append((j == t).astype(np.float32))
    return (jnp.asarray(np.concatenate(mats, 0)), jnp.asarray(np.stack(masks, 0)),
            len(mats) - 1)


def _even_kernel(rb_ref, fs_ref, insl_ref, emit_ref, pidx_ref, sidx_ref,
                 pa_ref, qb_ref, kb_ref, vb_ref, ob_ref, gt_ref,
                 lbl_ref, hn_ref, mnw_ref, gbias_ref,
                 lvl_ref, lmask_ref, tri_ref, eye_ref,
                 hs_in, mc_in, mn_in, mm_in,
                 mrg_ref, hs_p, mc_p, mn_p, mm_p, hs_s, mc_s, mn_s, mm_s,
                 hs_out, mc_out, mn_out, mm_out,
                 meta_hs, meta_mc, meta_mn, meta_mm,
                 *, lb_index, n_levels):
    c = CHUNK
    j = pl.program_id(0)
    fs = fs_ref[j]
    emit = emit_ref[j]

    @pl.when(fs == 1)
    def _():
        hs_out[...] = jnp.zeros_like(hs_out)
        mc_out[...] = jnp.zeros_like(mc_out)
        mn_out[...] = jnp.zeros_like(mn_out)
        mm_out[...] = jnp.zeros_like(mm_out)

    @pl.when(fs == 2)
    def _():
        hs_out[...] = meta_hs[...]
        mc_out[...] = meta_mc[...]
        mn_out[...] = meta_mn[...]
        mm_out[...] = meta_mm[...]

    @pl.when(fs == 3)
    def _():
        for ln in range(LANES):
            for h in range(HA_HEADS):
                hs_out[0, ln, h] = hs_in[0, ln, h].T
            mm_out[0, ln] = jnp.concatenate(
                [jnp.zeros((1, HB_HEADS), F32), mm_in[0, ln], jnp.zeros((1, LANE - 2 * HB_HEADS), F32)], axis=1)
        mc_out[...] = mc_in[...]
        mn_out[...] = mn_in[...]

    lanes = range(LANES)
    rows = [slice(ln * c, (ln + 1) * c) for ln in lanes]
    hk = HA_HEADS * HA_DK
    ua = [(ln, h) for ln in lanes for h in range(HA_HEADS)]
    ub = [(ln, h) for ln in lanes for h in range(HB_HEADS)]
    hsl = [slice(h * HA_DK, (h + 1) * HA_DK) for h in range(HA_HEADS)]

    lbl = lbl_ref[...]
    ex = jnp.exp(lbl - jnp.max(lbl, axis=0, keepdims=True))
    sm = ex / jnp.sum(ex, axis=0, keepdims=True)
    lb = jnp.sum(sm[:lb_index + 1], axis=0, keepdims=True)
    log_f, k_all, q_all, e_lvl, b_all = [], [], [], [], []
    for ln in lanes:
        sf = pa_ref[rows[ln], hk:2 * hk]
        log_f.append(jnp.log(lb + (1.0 - lb) * sf))
        k_all.append((1.0 - lb) * (1.0 - sf))
        q_all.append(pa_ref[rows[ln], 0:hk])
    for ln in lanes:
        b_all.append(_dot_pieces(_nn, lvl_ref[n_levels * c:(n_levels + 1) * c, :], log_f[ln], 2))
    for lvl in range(n_levels):
        e_lvl.append([_dot_pieces(_nn, lvl_ref[lvl * c:(lvl + 1) * c, :], log_f[ln], 2) for ln in lanes])

    lane = lax.broadcasted_iota(jnp.int32, (c, LANE), 1)
    rowg = lax.broadcasted_iota(jnp.int32, (c, LANE), 0)
    is_f = (lane >= HB_HEADS) & (lane < 2 * HB_HEADS)
    pad = rowg < jnp.where(j == 0, c - N_META, 0)
    ig_t, fc, fr, ir = [], [], [], []
    for ln in lanes:
        gcap = GATE_CAP * jnp.tanh((gt_ref[rows[ln], :] + gbias_ref[...]) / GATE_CAP)
        lf_t = jnp.where(is_f & jnp.logical_not(pad), _log_sigmoid(gcap), 0.0)
        ig_t.append(jnp.where(pad, NEG_BIG, gcap))
        fc.append(_dot_pieces(_nn, tri_ref[...], lf_t, 3))
    for ln in lanes:
        fr.append(_dot_pieces(_nt, eye_ref[...], fc[ln], 3))
        ir.append(_dot_pieces(_nt, eye_ref[...], ig_t[ln], 3))

    row = lax.broadcasted_iota(jnp.int32, (c, HA_DK), 0)
    q_h = [q_all[ln][:, hsl[h]] for ln, h in ua]
    k_h = [k_all[ln][:, hsl[h]] for ln, h in ua]
    v16 = [pa_ref[rows[ln], 2 * hk + h * HA_DV:2 * hk + (h + 1) * HA_DV].astype(BF16) for ln, h in ua]
    b_h = [b_all[ln][:, hsl[h]] for ln, h in ua]
    st = [hs_out[0, ln, h] for ln, h in ua]
    o_inter = [_nt((q_h[u] * jnp.exp(b_h[u])).astype(BF16), st[u].astype(BF16)) for u in range(len(ua))]
    for u, (ln, h) in enumerate(ua):
        b_last = b_h[u][c - 1:c, :]
        k_til = k_h[u] * jnp.exp(b_last - b_h[u])
        hs_out[0, ln, h] = st[u] * jnp.exp(b_last) + _tn(v16[u], k_til.astype(BF16))
    scores = [lmask_ref[n_levels] * _nt(q_h[u].astype(BF16), k_h[u].astype(BF16)) for u in range(len(ua))]
    m = c // 2
    for lvl in range(n_levels):
        second = (row & m) != 0
        for u, (ln, h) in enumerate(ua):
            dec = jnp.exp(e_lvl[lvl][ln][:, hsl[h]])
            qs = jnp.where(second, q_h[u] * dec, 0.0).astype(BF16)
            ks = jnp.where(second, 0.0, k_h[u] * dec).astype(BF16)
            part = _nt(qs, ks)
            scores[u] = scores[u] + (part if lvl == 0 else lmask_ref[lvl] * part)
        m //= 2

    r_i = lax.broadcasted_iota(jnp.int32, (c, c), 0)
    c_i = lax.broadcasted_iota(jnp.int32, (c, c), 1)
    causal = c_i <= r_i
    lane1 = lax.broadcasted_iota(jnp.int32, (1, LANE), 1)
    m_row = [mm_out[0, ln] for ln in lanes]
    m_row_new = list(m_row)
    mq = [qb_ref[rows[ln], h * HB_DQK:(h + 1) * HB_DQK] for ln, h in ub]
    mk = [kb_ref[rows[ln], h * HB_DQK:(h + 1) * HB_DQK] * (HB_DQK ** -0.5) for ln, h in ub]
    mv16 = [vb_ref[rows[ln], h * HB_DV:(h + 1) * HB_DV].astype(BF16) for ln, h in ub]
    ct = [mc_out[0, ln, h] for ln, h in ub]
    n_row = [mn_out[0, ln, h] for ln, h in ub]
    qk_raw = [_nt(mq[u].astype(BF16), mk[u].astype(BF16)) for u in range(len(ub))]
    inter = [_nn(mq[u].astype(BF16), ct[u].astype(BF16)) for u in range(len(ub))]
    qkd, m_ts, w_inters = [], [], []
    for u, (ln, h) in enumerate(ub):
        fl = HB_HEADS + h
        fc_h = fc[ln][:, fl:fl + 1]
        fr_h = fr[ln][fl:fl + 1, :]
        igr_h = ir[ln][h:h + 1, :]
        igc_h = ig_t[ln][:, h:h + 1]
        m_prev = m_row[ln][:, fl:fl + 1]
        log_d = jnp.where(causal, fc_h - fr_h + igr_h, -jnp.inf)
        log_inter = fc_h + m_prev
        m_t = jnp.maximum(log_inter, jnp.max(log_d, axis=1, keepdims=True))
        m_ts.append(m_t)
        w_inters.append(jnp.exp(log_inter - m_t))
        qkd.append(qk_raw[u] * jnp.exp(log_d - m_t))
        f_last = fc_h[c - 1:c, :]
        m_new = m_t[c - 1:c, :]
        w_s = jnp.exp(f_last - fc_h + igc_h - m_new)
        decay = jnp.exp(f_last + m_prev - m_new)
        kw = mk[u] * w_s
        mc_out[0, ln, h] = decay * ct[u] + _tn(kw.astype(BF16), mv16[u])
        mn_out[0, ln, h] = decay * n_row[u] + jnp.sum(kw, axis=0, keepdims=True)
        m_row_new[ln] = jnp.where(lane1 == fl, m_new, m_row_new[ln])
    for ln in lanes:
        mm_out[0, ln] = m_row_new[ln]

    o_a = [_nn(scores[u].astype(BF16), v16[u]) + o_inter[u] for u in range(len(ua))]
    num = [_nn(qkd[u].astype(BF16), mv16[u]) + w_inters[u] * inter[u] for u in range(len(ub))]
    for u, (ln, h) in enumerate(ua):
        ga = pa_ref[rows[ln], 3 * hk + h * HA_DV:3 * hk + (h + 1) * HA_DV]
        mrg_ref[rows[ln], h * HA_DV:(h + 1) * HA_DV] = (_rms(o_a[u], hn_ref[:, hsl[h]]) * ga).astype(BF16)
    for u, (ln, h) in enumerate(ub):
        den = jnp.sum(qkd[u], axis=1, keepdims=True) \
            + w_inters[u] * jnp.sum(mq[u] * n_row[u], axis=1, keepdims=True)
        hh = num[u] / jnp.maximum(jnp.abs(den), jnp.exp(-m_ts[u]))
        ob = ob_ref[rows[ln], h * HB_DV:(h + 1) * HB_DV]
        col0 = HA_HEADS * HA_DV + h * HB_DV
        mrg_ref[rows[ln], col0:col0 + HB_DV] = (
            _rms(hh, mnw_ref[:, h * HB_DV:(h + 1) * HB_DV]) * ob).astype(BF16)

    def write_states(hs_o, mc_o, mn_o, mm_o):
        for ln in range(LANES):
            for h in range(HA_HEADS):
                hs_o[0, ln, h] = hs_out[0, ln, h].T
            mm_o[0, ln] = mm_out[0, ln][:, HB_HEADS:2 * HB_HEADS]
        mc_o[...] = mc_out[...]
        mn_o[...] = mn_out[...]

    @pl.when(emit == 1)
    def _():
        write_states(hs_p, mc_p, mn_p, mm_p)

    @pl.when(emit == 2)
    def _():
        write_states(hs_s, mc_s, mn_s, mm_s)

    @pl.when(j == 0)
    def _():
        meta_hs[...] = hs_out[...]
        meta_mc[...] = mc_out[...]
        meta_mn[...] = mn_out[...]
        meta_mm[...] = mm_out[...]


def _even_mixer(p, tables, n_pp, n_sp, lb_logits, lb_index, hgrn_norm, mlstm_norm, gbias,
                hs_in, mc_in, mn_in, mm_in):
    rb, fs, insl, emit, pidx, sidx = tables
    c = CHUNK
    rows = LANES * c
    n_steps = rb.shape[0]
    lvl, lmask, n_levels = _hgrn_tables(c)
    tri = jnp.asarray(np.tril(np.ones((c, c), np.float32)))
    eye = jnp.eye(LANE, dtype=F32)
    hk = HA_HEADS * HA_DK
    wa = 4 * hk
    qk_w = HB_HEADS * HB_DQK
    v_w = HB_HEADS * HB_DV
    g_col = wa + 2 * qk_w + 2 * v_w
    cst = lambda shape: pl.BlockSpec(shape, lambda j, *_: (0,) * len(shape))
    row_blk = lambda w, idx: pl.BlockSpec((rows, w), lambda j, r, *_: (r[j], idx))
    zeros = lambda shape: (0,) * len(shape)
    st_in = lambda shape: pl.BlockSpec((1,) + shape, lambda j, r, f, i, *_: (i[j],) + zeros(shape))
    st_p = lambda shape: pl.BlockSpec((1,) + shape, lambda j, r, f, i, e, pi, si: (pi[j],) + zeros(shape))
    st_s = lambda shape: pl.BlockSpec((1,) + shape, lambda j, r, f, i, e, pi, si: (si[j],) + zeros(shape))
    sh_hs = (LANES, HA_HEADS, HA_DK, HA_DV)
    sh_mc = (LANES, HB_HEADS, HB_DQK, HB_DV)
    sh_mn = (LANES, HB_HEADS, 1, HB_DQK)
    sh_mm = (LANES, 1, HB_HEADS)
    io_shapes = (sh_hs, sh_mc, sh_mn, sh_mm)
    carry_shapes = ((LANES, HA_HEADS, HA_DV, HA_DK), sh_mc, sh_mn, (LANES, 1, LANE))
    grid_spec = pltpu.PrefetchScalarGridSpec(
        num_scalar_prefetch=6,
        grid=(n_steps,),
        in_specs=[row_blk(wa, 0),
                  row_blk(qk_w, wa // qk_w), row_blk(qk_w, wa // qk_w + 1),
                  row_blk(v_w, (wa + 2 * qk_w) // v_w), row_blk(v_w, (wa + 2 * qk_w) // v_w + 1),
                  row_blk(LANE, g_col // LANE),
                  cst(lb_logits.shape), cst((1, hk)), cst((1, v_w)), cst((1, LANE)),
                  cst(lvl.shape), cst(lmask.shape), cst((c, c)), cst((LANE, LANE))]
                 + [st_in(sh) for sh in io_shapes],
        out_specs=[pl.BlockSpec((rows, hk + v_w), lambda j, r, *_: (r[j], 0))]
                  + [st_p(sh) for sh in io_shapes] + [st_s(sh) for sh in io_shapes],
        scratch_shapes=[pltpu.VMEM((1,) + sh, F32) for sh in carry_shapes + carry_shapes],
    )
    out_shape = ([jax.ShapeDtypeStruct((p.shape[0], hk + v_w), BF16)]
                 + [jax.ShapeDtypeStruct((n_pp,) + sh, F32) for sh in io_shapes]
                 + [jax.ShapeDtypeStruct((n_sp,) + sh, F32) for sh in io_shapes])
    return pl.pallas_call(
        functools.partial(_even_kernel, lb_index=lb_index, n_levels=n_levels),
        grid_spec=grid_spec,
        out_shape=out_shape,
        compiler_params=pltpu.CompilerParams(dimension_semantics=("arbitrary",),
                                             vmem_limit_bytes=VMEM_LIMIT),
        name="even_mixer",
    )(rb, fs, insl, emit, pidx, sidx, p, p, p, p, p, p, lb_logits, hgrn_norm, mlstm_norm, gbias,
      lvl, lmask, tri, eye, hs_in, mc_in, mn_in, mm_in)


def _gdn_kernel(rb_ref, fs_ref, insl_ref, emit_ref, pidx_ref, sidx_ref,
                qkv_ref, z_ref, gt_ref, cw_ref, alog_ref, dtb_ref, gnw_ref,
                tri_ref, s_in, cv_in,
                mrg_ref, s_p, cv_p, s_s, cv_s,
                s_out, xpad, meta_s, meta_cv, *, n_sq, hps):
    c = CHUNK
    hg = pl.program_id(0)
    j = pl.program_id(1)
    fs = fs_ref[j]
    emit = emit_ref[j]
    nk = CONV_W - 1
    base = 8 - nk

    @pl.when(fs == 1)
    def _():
        s_out[...] = jnp.zeros_like(s_out)
        xpad[:, base:8, :] = jnp.zeros((LANES, nk, xpad.shape[2]), F32)

    @pl.when(fs == 2)
    def _():
        s_out[...] = meta_s[...]
        xpad[:, base:8, :] = meta_cv[...]

    @pl.when(fs == 3)
    def _():
        s_out[...] = s_in[...]
        xpad[:, base:8, :] = cv_in[0]

    lanes = range(LANES)
    rows = [slice(ln * c, (ln + 1) * c) for ln in lanes]
    lane = lax.broadcasted_iota(jnp.int32, (c, LANE), 1)
    row = lax.broadcasted_iota(jnp.int32, (c, LANE), 0)
    first = lane < c
    s_idx = jnp.where(first, lane, lane - c)
    causal = s_idx <= row
    strict = s_idx < row
    diag = s_idx == row
    eye_c = diag.astype(F32)
    r2 = lax.broadcasted_iota(jnp.int32, (2 * c, LANE), 0)
    l2 = lax.broadcasted_iota(jnp.int32, (2 * c, LANE), 1)
    same_head = (r2 < c) == (l2 < c)
    nq = HC_QK_HEADS * HC_DK

    def col(tile, idx):
        return jnp.sum(jnp.where(lane == idx, tile, 0.0), axis=1, keepdims=True)

    def block_diag(m16):
        return jnp.where(same_head, jnp.concatenate([m16, m16], axis=0), jnp.zeros_like(m16[:1, :1]))

    act, beta_t, gc = [], [], []
    for ln in lanes:
        xpad[ln, 8:8 + c, :] = qkv_ref[rows[ln], :]
        xp = xpad[ln]
        y = cw_ref[nk:CONV_W, :] * xp[8:8 + c]
        for t in range(nk):
            y = y + cw_ref[t:t + 1, :] * pltpu.roll(xp, nk - t, 0)[8:8 + c]
        xpad[ln, base:8, :] = xp[base + c:8 + c]
        act.append(_silu(y))
        gt = gt_ref[rows[ln], :]
        beta_t.append(_sigmoid(gt))
        g_t = -jnp.exp(alog_ref[...]) * _softplus(gt + dtb_ref[...])
        gc.append(_dot_pieces(_nn, tri_ref[...], g_t, 3))

    units = [(ln, p) for ln in lanes for p in range(hps)]
    nu = range(len(units))
    qn, kn, qk, kk = [], [], [], []
    for ln, p in units:
        q = act[ln][:, p * HC_DK:(p + 1) * HC_DK]
        k = act[ln][:, nq + p * HC_DK:nq + (p + 1) * HC_DK]
        qn.append(q * lax.rsqrt(jnp.sum(q * q, axis=-1, keepdims=True) + NORM_EPS) * (HC_DK ** -0.5))
        kn.append(k * lax.rsqrt(jnp.sum(k * k, axis=-1, keepdims=True) + NORM_EPS))
    for u in nu:
        kn16 = kn[u].astype(BF16)
        k2 = jnp.concatenate([kn16, kn16], axis=0)
        qk.append(_nt(qn[u].astype(BF16), k2))
        kk.append(_nt(kn16, k2))

    beta, gcol, decay, inv, pw = [], [], [], [], []
    for u, (ln, p) in enumerate(units):
        hv = 2 * (hg * hps + p)
        b_a, b_b = col(beta_t[ln], hv), col(beta_t[ln], hv + 1)
        g_a, g_b = col(gc[ln], HC_V_HEADS + hv), col(gc[ln], HC_V_HEADS + hv + 1)
        beta.append((b_a, b_b))
        gcol.append((g_a, g_b))
        g_c = jnp.where(first, g_a, g_b)
        g_r = jnp.sum(jnp.where(diag, g_c, 0.0), axis=0, keepdims=True)
        decay.append(jnp.exp(jnp.where(causal, g_c - g_r, -jnp.inf)))
        low = jnp.where(strict, jnp.where(first, b_a, b_b) * kk[u] * decay[u], 0.0)
        inv.append(eye_c - low)
        pw.append(low.astype(BF16))
    pw = [_nn(x, block_diag(x)).astype(BF16) for x in pw]
    for i in range(n_sq):
        if i < n_sq - 1:
            prod = [_nn(jnp.concatenate([x.astype(BF16), y], axis=0), block_diag(y)) for x, y in zip(inv, pw)]
            inv = [x + r[:c] for x, r in zip(inv, prod)]
            pw = [r[c:].astype(BF16) for r in prod]
        else:
            inv = [x + _nn(x.astype(BF16), block_diag(y)) for x, y in zip(inv, pw)]

    sol, eg = [], []
    for u, (ln, p) in enumerate(units):
        (b_a, b_b), (g_a, g_b) = beta[u], gcol[u]
        e_a, e_b = jnp.exp(g_a), jnp.exp(g_b)
        eg.append((e_a, e_b))
        v_a = act[ln][:, 2 * nq + 2 * p * HC_DV:2 * nq + (2 * p + 1) * HC_DV]
        v_b = act[ln][:, 2 * nq + (2 * p + 1) * HC_DV:2 * nq + (2 * p + 2) * HC_DV]
        rhs = jnp.concatenate([jnp.concatenate([v_a * b_a, kn[u] * (b_a * e_a)], axis=1),
                               jnp.concatenate([v_b * b_b, kn[u] * (b_b * e_b)], axis=1)],
                              axis=0).astype(BF16)
        sol.append((_nn(jnp.where(first, inv[u], 0.0).astype(BF16), rhs),
                    _nn(jnp.where(first, 0.0, inv[u]).astype(BF16), rhs)))
    s_old = [[s_out[0, ln, 2 * p + jj] for jj in range(2)] for ln, p in units]
    ws = []
    for u in nu:
        for jj in range(2):
            lhs = jnp.concatenate([sol[u][jj][:, HC_DV:], qn[u] * eg[u][jj]], axis=0)
            ws.append(_nn(lhs.astype(BF16), s_old[u][jj].astype(BF16)))
    for u, (ln, p) in enumerate(units):
        v_new = [sol[u][jj][:, :HC_DV] - ws[2 * u + jj][:c] for jj in range(2)]
        v_st = jnp.concatenate(v_new, axis=0).astype(BF16)
        attn = qk[u] * decay[u]
        o = (ws[2 * u][c:] + _nn(jnp.where(first, attn, 0.0).astype(BF16), v_st),
             ws[2 * u + 1][c:] + _nn(jnp.where(first, 0.0, attn).astype(BF16), v_st))
        for jj in range(2):
            n = 2 * p + jj
            g_h = gcol[u][jj]
            g_last = g_h[c - 1:c, :]
            k_til = kn[u] * jnp.exp(g_last - g_h)
            s_out[0, ln, n] = jnp.exp(g_last) * s_old[u][jj] + _tn(k_til.astype(BF16), v_new[jj].astype(BF16))
            z = z_ref[rows[ln], n * HC_DV:(n + 1) * HC_DV]
            mrg_ref[rows[ln], n * HC_DV:(n + 1) * HC_DV] = (_rms(o[jj], gnw_ref[...]) * z).astype(BF16)

    @pl.when(emit == 1)
    def _():
        s_p[...] = s_out[...]
        cv_p[0] = xpad[:, base:8, :]

    @pl.when(emit == 2)
    def _():
        s_s[...] = s_out[...]
        cv_s[0] = xpad[:, base:8, :]

    @pl.when(j == 0)
    def _():
        meta_s[...] = s_out[...]
        meta_cv[...] = xpad[:, base:8, :]


def _gdn_mixer(p, tables, n_pp, n_sp, conv_w, alog_row, dtb_row, gnw, s_in, cv_in):
    rb, fs, insl, emit, pidx, sidx = tables
    hps = HC_QK_HEADS
    c = CHUNK
    rows = LANES * c
    n_steps = rb.shape[0]
    n_hg = HC_QK_HEADS // hps
    wq = hps * 4 * HC_DK
    wz = hps * 2 * HC_DV
    qkv_w = HC_QK_HEADS * 4 * HC_DK
    z_w = HC_V_HEADS * HC_DV
    n_sq = int(np.log2(c)) - 1
    tri = jnp.asarray(np.tril(np.ones((c, c), np.float32)))
    nk = CONV_W - 1
    sh_s = (1, LANES, 2 * hps, HC_DK, HC_DV)
    sh_cv = (1, LANES, nk, wq)
    cst = lambda shape: pl.BlockSpec(shape, lambda h, j, *_: (0,) * len(shape))
    s_spec = lambda k: pl.BlockSpec(sh_s, lambda h, j, *t: (t[k][j], 0, h, 0, 0))
    cv_spec = lambda k: pl.BlockSpec(sh_cv, lambda h, j, *t: (t[k][j], 0, 0, h))
    grid_spec = pltpu.PrefetchScalarGridSpec(
        num_scalar_prefetch=6,
        grid=(n_hg, n_steps),
        in_specs=[pl.BlockSpec((rows, wq), lambda h, j, r, *_: (r[j], h)),
                  pl.BlockSpec((rows, wz), lambda h, j, r, *_: (r[j], qkv_w // wz + h)),
                  pl.BlockSpec((rows, LANE), lambda h, j, r, *_: (r[j], (qkv_w + z_w) // LANE)),
                  pl.BlockSpec((CONV_W, wq), lambda h, j, *_: (0, h)),
                  cst((1, LANE)), cst((1, LANE)), cst((1, HC_DV)),
                  cst((c, c)),
                  s_spec(2), cv_spec(2)],
        out_specs=[pl.BlockSpec((rows, wz), lambda h, j, r, *_: (r[j], h)),
                   s_spec(4), cv_spec(4), s_spec(5), cv_spec(5)],
        scratch_shapes=[pltpu.VMEM(sh_s, F32),
                        pltpu.VMEM((LANES, 8 + c, wq), F32),
                        pltpu.VMEM(sh_s, F32),
                        pltpu.VMEM((LANES, nk, wq), F32)],
    )
    st = lambda n: [jax.ShapeDtypeStruct((n, LANES, HC_V_HEADS, HC_DK, HC_DV), F32),
                    jax.ShapeDtypeStruct((n, LANES, nk, qkv_w), F32)]
    return pl.pallas_call(
        functools.partial(_gdn_kernel, n_sq=n_sq, hps=hps),
        grid_spec=grid_spec,
        out_shape=[jax.ShapeDtypeStruct((p.shape[0], z_w), BF16)] + st(n_pp) + st(n_sp),
        compiler_params=pltpu.CompilerParams(dimension_semantics=("arbitrary", "arbitrary"),
                                             vmem_limit_bytes=VMEM_LIMIT),
        name="gdn_mixer",
    )(rb, fs, insl, emit, pidx, sidx, p, p, p, conv_w, alog_row, dtb_row, gnw, tri, s_in, cv_in)


def _pad_cols(w, width):
    return jnp.pad(w, ((0, 0), (0, width - w.shape[1])))


def kernel(x_prompt, x_sample, state_hgrn_S, state_mlstm_C, state_mlstm_n, state_mlstm_m, state_gdn_S,
           state_gdn_conv, meta_tokens, norm_mix, norm_ffn, norm_final, even_w_in, even_w_out,
           hgrn_lb_logits, hgrn_norm, mlstm_b_i, mlstm_b_f, mlstm_norm, odd_w_in, odd_conv_w, gdn_a_log,
           gdn_dt_bias, gdn_norm, odd_w_out, ffn_w_in, ffn_w_out):
    n_b, t_len, d = x_prompt.shape
    n_s, l_s, _ = x_sample.shape
    c = CHUNK
    blk = LANES * c
    assert t_len % c == 0 and l_s % c == 0 and meta_tokens.shape[0] == N_META
    assert n_b % LANES == 0 and n_s % LANES == 0, "sequences are processed in pairs"
    assert norm_mix.shape[0] == 2, "one even (HGRN2+mLSTM) and one odd (DeltaNet) layer"
    dt = x_prompt.dtype
    rows_p, rows_s = n_b * t_len, n_s * l_s
    tm = min(DENSE_TM, int(np.gcd(rows_p, rows_s)))
    assert tm % blk == 0
    m_rows = rows_p + rows_s + tm

    meta_job = jnp.concatenate([jnp.zeros((c - N_META, d), dt), meta_tokens.astype(dt)], axis=0)
    x_meta = jnp.concatenate([meta_job, meta_job, jnp.zeros((tm - blk, d), dt)], axis=0)
    x = (x_prompt.reshape(n_b // LANES, LANES, t_len // c, c, d),
         x_sample.reshape(n_s // LANES, LANES, l_s // c, c, d), x_meta)
    tables = _step_tables(n_b // LANES, t_len // c, n_s // LANES, l_s // c, m_rows // blk)
    row = lambda v: v.reshape(1, -1).astype(F32)
    pair = lambda a: a.reshape((a.shape[0] // LANES, LANES) + a.shape[1:])

    e = 0
    n_main = 4 * HA_HEADS * HA_DK + 2 * HB_HEADS * HB_DQK + 2 * HB_HEADS * HB_DV
    w_even = _pad_cols(even_w_in[e].astype(BF16), n_main + LANE)
    hk, hv = HA_HEADS * HA_DK, HA_HEADS * HA_DV
    mq, mv = HB_HEADS * HB_DQK, HB_HEADS * HB_DV
    ident = lambda v: v
    segs_even = ((0, hk, _silu), (hk, 2 * hk, _sigmoid), (2 * hk, 2 * hk + hv, ident),
                 (2 * hk + hv, 2 * hk + 2 * hv, _silu),
                 (2 * hk + 2 * hv, n_main - mv, ident), (n_main - mv, n_main, _sigmoid),
                 (n_main, n_main + LANE, ident))
    p_even = _proj(x, row(norm_mix[0]), w_even, tm, segs_even, "proj_even")
    gbias = _pad_cols(jnp.concatenate([mlstm_b_i[e], mlstm_b_f[e]]).reshape(1, -1).astype(F32), LANE)
    np_, ns_ = n_b // LANES, n_s // LANES
    mrg0, *even_states = _even_mixer(
        p_even, tables, np_, ns_, hgrn_lb_logits.astype(F32), e, row(hgrn_norm[e]), row(mlstm_norm[e]), gbias,
        pair(state_hgrn_S[e].astype(F32)), pair(state_mlstm_C[e].astype(F32)),
        pair(state_mlstm_n[e].astype(F32)[:, :, None, :]), pair(state_mlstm_m[e].astype(F32)[:, None, :]))
    ffn_wi, ffn_w2 = ffn_w_in.astype(BF16), ffn_w_out.astype(BF16)
    x = _post(x, mrg0, (even_w_out.astype(BF16), e), row(norm_ffn[0]), (ffn_wi, 0), (ffn_w2, 0),
              row(norm_final), tm, "post_even")

    o = 0
    n_qkv = 2 * HC_QK_HEADS * HC_DK + HC_V_HEADS * HC_DV
    n_z = HC_V_HEADS * HC_DV
    w_odd = _pad_cols(odd_w_in[o].astype(BF16), n_qkv + n_z + LANE)
    segs_odd = ((0, n_qkv, ident), (n_qkv, n_qkv + n_z, _silu), (n_qkv + n_z, n_qkv + n_z + LANE, ident))
    p_odd = _proj(x, row(norm_mix[1]), w_odd, tm, segs_odd, "proj_odd")
    lane_pad = lambda v: jnp.pad(v.reshape(1, -1).astype(F32),
                                 ((0, 0), (HC_V_HEADS, LANE - 2 * HC_V_HEADS)))
    mrg1, *gdn_states = _gdn_mixer(
        p_odd, tables, np_, ns_, odd_conv_w[o].astype(F32), lane_pad(gdn_a_log[o]),
        lane_pad(gdn_dt_bias[o]), row(gdn_norm[o]), pair(state_gdn_S[o].astype(F32)),
        pair(state_gdn_conv[o].astype(F32)))
    post_odd = functools.partial(_post, x, mrg1, (odd_w_out.astype(BF16), o), row(norm_ffn[1]),
                                 (ffn_wi, 1), (ffn_w2, 1), row(norm_final), tm, final_norm=True)
    y_p = post_odd("post_odd_prompt", tile0=0, n_tiles=rows_p // tm, unpair_shape=(n_b // LANES, t_len))
    y_s = post_odd("post_odd_sample", tile0=rows_p // tm, n_tiles=rows_s // tm,
                   unpair_shape=(n_s // LANES, l_s))

    y_prompt = y_p.reshape(n_b, t_len, d)
    y_sample = y_s.reshape(n_s, l_s, d)
    unpair = lambda v: v.reshape((-1,) + v.shape[2:])[None].astype(dt)
    outs = [y_prompt, y_sample]
    for k in range(2):
        hs_o, mc_o, mn_o, mm_o = even_states[4 * k:4 * k + 4]
        gs_o, cv_o = gdn_states[2 * k:2 * k + 2]
        outs += [unpair(hs_o), unpair(mc_o), unpair(mn_o[:, :, :, 0, :]), unpair(mm_o[:, :, 0, :]),
                 unpair(gs_o), unpair(cv_o)]
    return tuple(outs)
```

```python
import functools

import numpy as np
import jax
import jax.numpy as jnp
from jax import lax
from jax.experimental import pallas as pl
from jax.experimental.pallas import tpu as pltpu

F32 = jnp.float32
BF16 = jnp.bfloat16

CHUNK = 64
LANES = 2
N_META = 16
NORM_EPS = 1e-6
GATE_CAP = 15.0
NEG_BIG = -1e30

HA_HEADS, HA_DK, HA_DV = 4, 128, 128
HB_HEADS, HB_DQK, HB_DV = 4, 64, 128
HC_QK_HEADS, HC_V_HEADS, HC_DK, HC_DV = 8, 16, 128, 128
CONV_W = 4
LANE = 128
DENSE_TM = 512
VMEM_LIMIT = 56 * 1024 * 1024


def _nn(a, b):
    return lax.dot_general(a, b, (((1,), (0,)), ((), ())), preferred_element_type=F32)


def _nt(a, b):
    return lax.dot_general(a, b, (((1,), (1,)), ((), ())), preferred_element_type=F32)


def _tn(a, b):
    return lax.dot_general(a, b, (((0,), (0,)), ((), ())), preferred_element_type=F32)


def _dot_pieces(dot, exact, x, n):
    e16 = exact.astype(BF16)
    acc = None
    for _ in range(n):
        piece = x.astype(BF16)
        part = dot(e16, piece)
        acc = part if acc is None else acc + part
        x = x - piece.astype(F32)
    return acc


def _sigmoid(x):
    return 1.0 / (1.0 + jnp.exp(-x))


def _silu(x):
    return x * _sigmoid(x)


def _softplus(x):
    return jnp.maximum(x, 0.0) + jnp.log1p(jnp.exp(-jnp.abs(x)))


def _log_sigmoid(x):
    return jnp.minimum(x, 0.0) - jnp.log1p(jnp.exp(-jnp.abs(x)))


def _rms(x, w):
    ms = jnp.mean(x * x, axis=-1, keepdims=True)
    return x * lax.rsqrt(ms + NORM_EPS) * w


def _pair_block(pairs, seq_len, tm):
    chunks = seq_len // CHUNK
    k_tile = tm // (LANES * CHUNK)
    if chunks % k_tile == 0:
        per_pair = chunks // k_tile
        return (1, LANES, k_tile, CHUNK), (lambda t: (t // per_pair, 0, t % per_pair, 0, 0)), k_tile
    assert k_tile % chunks == 0
    return (k_tile // chunks, LANES, chunks, CHUNK), (lambda t: (t, 0, 0, 0, 0)), chunks


def _tile_units(n_rows):
    return [((k * LANES + ln) * CHUNK, k, ln) for k in range(n_rows // (LANES * CHUNK)) for ln in range(LANES)]


def _load_x_tile(i, xp_ref, xs_ref, xm_ref, xt_ref, geom):
    n_pt, n_st, kc_p, kc_s = geom

    def fill(src, kc):
        for r0, k, ln in _tile_units(xt_ref.shape[0]):
            xt_ref[r0:r0 + CHUNK, :] = src[k // kc, ln, k % kc]

    @pl.when(i < n_pt)
    def _():
        fill(xp_ref, kc_p)

    @pl.when((i >= n_pt) & (i < n_pt + n_st))
    def _():
        fill(xs_ref, kc_s)

    @pl.when(i >= n_pt + n_st)
    def _():
        xt_ref[...] = xm_ref[...]


def _x_specs(xsrc, tm):
    xp5, xs5, xm = xsrc
    d = xp5.shape[-1]
    blk_p, idx_p, kc_p = _pair_block(xp5.shape[0], xp5.shape[2] * CHUNK, tm)
    blk_s, idx_s, kc_s = _pair_block(xs5.shape[0], xs5.shape[2] * CHUNK, tm)
    n_pt = xp5.shape[0] * xp5.shape[2] * LANES * CHUNK // tm
    n_st = xs5.shape[0] * xs5.shape[2] * LANES * CHUNK // tm
    specs = [pl.BlockSpec(blk_p + (d,), lambda i: idx_p(jnp.minimum(i, n_pt - 1))),
             pl.BlockSpec(blk_s + (d,), lambda i: idx_s(jnp.clip(i - n_pt, 0, n_st - 1))),
             pl.BlockSpec((tm, d), lambda i: (0, 0))]
    return specs, (n_pt, n_st, kc_p, kc_s), (n_pt + n_st + 1) * tm, d


def _proj_kernel(*refs, n_chunk, segs, geom):
    if geom is None:
        x_ref, nw_ref, w_ref, o_ref = refs
    else:
        xp_ref, xs_ref, xm_ref, nw_ref, w_ref, o_ref, x_ref = refs
        _load_x_tile(pl.program_id(0), xp_ref, xs_ref, xm_ref, x_ref, geom)
    h = _rms(x_ref[...], nw_ref[...]).astype(BF16)
    for s0, s1, fn in segs:
        for n0 in range(s0, s1, n_chunk):
            n1 = min(n0 + n_chunk, s1)
            o_ref[:, n0:n1] = fn(_nn(h, w_ref[:, n0:n1]))


def _post_kernel(*refs, hidden, h_chunk, final_norm, unpair, geom):
    if geom is None:
        x_ref, m_ref, wo_ref, nw_ref, wi_ref, w2_ref, nf_ref, o_ref, act_ref = refs
    else:
        xp_ref, xs_ref, xm_ref, m_ref, wo_ref, nw_ref, wi_ref, w2_ref, nf_ref, o_ref, act_ref, x_ref = refs
        _load_x_tile(pl.program_id(0), xp_ref, xs_ref, xm_ref, x_ref, geom)
    x1 = x_ref[...] + _nn(m_ref[...], wo_ref[...])
    h = _rms(x1, nw_ref[...]).astype(BF16)
    for c0 in range(0, hidden, h_chunk):
        c1 = min(c0 + h_chunk, hidden)
        gate = _nn(h, wi_ref[:, c0:c1])
        up = _nn(h, wi_ref[:, hidden + c0:hidden + c1])
        act_ref[:, c0:c1] = (_silu(gate) * up).astype(BF16)
    x2 = x1 + _nn(act_ref[...], w2_ref[...])
    if final_norm:
        x2 = _rms(x2, nf_ref[...])
    if unpair:
        for r0, k, ln in _tile_units(x2.shape[0]):
            o_ref[k // unpair, ln, k % unpair] = x2[r0:r0 + CHUNK]
    else:
        o_ref[...] = x2


def _const_spec(shape):
    nd = len(shape)
    return pl.BlockSpec(shape, lambda i, _n=nd: (0,) * _n)


def _proj(x, nw, w, tm, segs, name):
    n = w.shape[1]
    assert segs[0][0] == 0 and segs[-1][1] == n and all(a[1] == b[0] for a, b in zip(segs, segs[1:]))
    if isinstance(x, tuple):
        x_specs, geom, m_rows, d = _x_specs(x, tm)
        x_args, scratch = list(x), [pltpu.VMEM((tm, d), F32)]
    else:
        m_rows, d = x.shape
        x_specs, geom, x_args, scratch = [pl.BlockSpec((tm, d), lambda i: (i, 0))], None, [x], []
    return pl.pallas_call(
        functools.partial(_proj_kernel, n_chunk=512, segs=segs, geom=geom),
        grid=(m_rows // tm,),
        in_specs=x_specs + [_const_spec((1, d)), _const_spec((d, n))],
        out_specs=pl.BlockSpec((tm, n), lambda i: (i, 0)),
        out_shape=jax.ShapeDtypeStruct((m_rows, n), F32),
        scratch_shapes=scratch,
        compiler_params=pltpu.CompilerParams(dimension_semantics=("arbitrary",),
                                             vmem_limit_bytes=VMEM_LIMIT),
        name=name,
    )(*x_args, nw, w)


def _post(x, mrg, wo, nw, wi, w2, nf, tm, name, final_norm=False, tile0=0, n_tiles=None, unpair_shape=None):
    (wo, l_o), (wi, l_i), (w2, l_2) = wo, wi, w2
    km = mrg.shape[1]
    hidden = w2.shape[1]
    layer_spec = lambda w, l: pl.BlockSpec((None,) + w.shape[1:], lambda i: (l, 0, 0))
    if isinstance(x, tuple):
        assert tile0 == 0 and n_tiles is None
        x_specs, geom, m_rows, d = _x_specs(x, tm)
        x_args, x_scratch = list(x), [pltpu.VMEM((tm, d), F32)]
    else:
        m_rows, d = x.shape
        x_specs, geom, x_args, x_scratch = [pl.BlockSpec((tm, d), lambda i: (tile0 + i, 0))], None, [x], []
    n_tiles = m_rows // tm if n_tiles is None else n_tiles
    unpair = 0
    if unpair_shape is None:
        out_spec = pl.BlockSpec((tm, d), lambda i: (i, 0))
        out_shape = jax.ShapeDtypeStruct((n_tiles * tm, d), F32)
    else:
        pairs, seq_len = unpair_shape
        assert pairs * seq_len * LANES == n_tiles * tm
        blk, idx, unpair = _pair_block(pairs, seq_len, tm)
        out_spec = pl.BlockSpec(blk + (d,), lambda i: idx(i))
        out_shape = jax.ShapeDtypeStruct((pairs, LANES, seq_len // CHUNK, CHUNK, d), F32)
    return pl.pallas_call(
        functools.partial(_post_kernel, hidden=hidden, h_chunk=256, final_norm=final_norm,
                          unpair=unpair, geom=geom),
        grid=(n_tiles,),
        in_specs=x_specs + [pl.BlockSpec((tm, km), lambda i: (tile0 + i, 0)),
                            layer_spec(wo, l_o),
                            _const_spec((1, d)),
                            layer_spec(wi, l_i),
                            layer_spec(w2, l_2),
                            _const_spec((1, d))],
        out_specs=out_spec,
        out_shape=out_shape,
        scratch_shapes=[pltpu.VMEM((tm, hidden), BF16)] + x_scratch,
        compiler_params=pltpu.CompilerParams(dimension_semantics=("arbitrary",),
                                             vmem_limit_bytes=VMEM_LIMIT),
        name=name,
    )(*x_args, mrg, wo, nw, wi, w2, nf)


def _step_tables(prompt_pairs, chunks_prompt, sample_pairs, chunks_sample, n_blocks):
    n_prompt = prompt_pairs * chunks_prompt
    n_sample = sample_pairs * chunks_sample
    rb, fs, insl, emit, pidx, sidx = [n_prompt + n_sample], [1], [0], [0], [0], [0]
    for b in range(prompt_pairs):
        for c in range(chunks_prompt):
            rb.append(b * chunks_prompt + c); fs.append(2 if c == 0 else 0); insl.append(0)
            emit.append(1 if c == chunks_prompt - 1 else 0); pidx.append(b); sidx.append(0)
    for q in range(sample_pairs):
        for c in range(chunks_sample):
            rb.append(n_prompt + q * chunks_sample + c); fs.append(3 if c == 0 else 0); insl.append(q)
            emit.append(2 if c == chunks_sample - 1 else 0); pidx.append(prompt_pairs - 1); sidx.append(q)
    for blk in range(n_prompt + n_sample + 1, n_blocks):
        rb.append(blk); fs.append(1); insl.append(0)
        emit.append(0); pidx.append(prompt_pairs - 1); sidx.append(sample_pairs - 1)
    to = lambda v: jnp.asarray(np.asarray(v, np.int32))
    return tuple(to(v) for v in (rb, fs, insl, emit, pidx, sidx))


def _hgrn_tables(c):
    t = np.arange(c)[:, None]
    j = np.arange(c)[None, :]
    mats, masks = [], []
    m = c // 2
    while m >= 1:
        blk = t // (2 * m)
        bd = blk * 2 * m + m - 1
        second = (t % (2 * m)) >= m
        mat = np.where(second, (j > bd) & (j <= t), (j > t) & (j <= bd)).astype(np.float32)
        mats.append(mat)
        masks.append((blk == (j // (2 * m))).astype(np.float32))
        m //= 2
    mats.append((j <= t).astype(np.float32))
    masks.append((j == t).astype(np.float32))
    return np.concatenate(mats, 0), np.concatenate(masks, 0), len(mats) - 1


def _even_kernel(rb_ref, fs_ref, insl_ref, emit_ref, pidx_ref, sidx_ref,
                 p_ref, crow_ref, mats_ref,
                 hs_in, mc_in, mn_in, mm_in,
                 mrg_ref, hs_p, mc_p, mn_p, mm_p, hs_s, mc_s, mn_s, mm_s,
                 hs_out, mc_out, mn_out, mm_out,
                 meta_hs, meta_mc, meta_mn, meta_mm,
                 *, lb_index, n_levels, n_lb):
    c = CHUNK
    wa, wqk, wv = 4 * HA_HEADS * HA_DK, HB_HEADS * HB_DQK, HB_HEADS * HB_DV
    pa_ref = p_ref.at[:, 0:wa]
    qb_ref = p_ref.at[:, wa:wa + wqk]
    kb_ref = p_ref.at[:, wa + wqk:wa + 2 * wqk]
    vb_ref = p_ref.at[:, wa + 2 * wqk:wa + 2 * wqk + wv]
    ob_ref = p_ref.at[:, wa + 2 * wqk + wv:wa + 2 * wqk + 2 * wv]
    gt_ref = p_ref.at[:, wa + 2 * wqk + 2 * wv:wa + 2 * wqk + 2 * wv + LANE]
    lbl_ref = crow_ref.at[0:n_lb, :]
    hn_ref = crow_ref.at[n_lb:n_lb + 1, :]
    mnw_ref = crow_ref.at[n_lb + 1:n_lb + 2, :]
    gbias_ref = crow_ref.at[n_lb + 2:n_lb + 3, 0:LANE]
    n_mat = (n_levels + 1) * c
    lvl_mat = lambda l: mats_ref[l * c:(l + 1) * c, 0:c]
    lmask = lambda l: mats_ref[n_mat + l * c:n_mat + (l + 1) * c, 0:c]
    tri = mats_ref[2 * n_mat:2 * n_mat + c, 0:c]
    eye = mats_ref[2 * n_mat + c:2 * n_mat + c + LANE, :]
    j = pl.program_id(0)
    fs = fs_ref[j]
    emit = emit_ref[j]

    @pl.when(fs == 1)
    def _():
        hs_out[...] = jnp.zeros_like(hs_out)
        mc_out[...] = jnp.zeros_like(mc_out)
        mn_out[...] = jnp.zeros_like(mn_out)
        mm_out[...] = jnp.zeros_like(mm_out)

    @pl.when(fs == 2)
    def _():
        hs_out[...] = meta_hs[...]
        mc_out[...] = meta_mc[...]
        mn_out[...] = meta_mn[...]
        mm_out[...] = meta_mm[...]

    @pl.when(fs == 3)
    def _():
        for ln in range(LANES):
            for h in range(HA_HEADS):
                hs_out[0, ln, h] = hs_in[0, ln, h].T
            mm_out[0, ln] = jnp.concatenate(
                [jnp.zeros((1, HB_HEADS), F32), mm_in[0, ln], jnp.zeros((1, LANE - 2 * HB_HEADS), F32)], axis=1)
        mc_out[...] = mc_in[...]
        mn_out[...] = mn_in[...]

    lanes = range(LANES)
    rows = [slice(ln * c, (ln + 1) * c) for ln in lanes]
    hk = HA_HEADS * HA_DK
    ua = [(ln, h) for ln in lanes for h in range(HA_HEADS)]
    ub = [(ln, h) for ln in lanes for h in range(HB_HEADS)]
    hsl = [slice(h * HA_DK, (h + 1) * HA_DK) for h in range(HA_HEADS)]

    lbl = lbl_ref[...]
    ex = jnp.exp(lbl - jnp.max(lbl, axis=0, keepdims=True))
    sm = ex / jnp.sum(ex, axis=0, keepdims=True)
    lb = jnp.sum(sm[:lb_index + 1], axis=0, keepdims=True)
    log_f, k_all, q_all, e_lvl, b_all = [], [], [], [], []
    for ln in lanes:
        sf = pa_ref[rows[ln], hk:2 * hk]
        log_f.append(jnp.log(lb + (1.0 - lb) * sf))
        k_all.append((1.0 - lb) * (1.0 - sf))
        q_all.append(pa_ref[rows[ln], 0:hk])
    for ln in lanes:
        b_all.append(_dot_pieces(_nn, lvl_mat(n_levels), log_f[ln], 2))
    for lvl in range(n_levels):
        e_lvl.append([_dot_pieces(_nn, lvl_mat(lvl), log_f[ln], 2) for ln in lanes])

    lane = lax.broadcasted_iota(jnp.int32, (c, LANE), 1)
    rowg = lax.broadcasted_iota(jnp.int32, (c, LANE), 0)
    is_f = (lane >= HB_HEADS) & (lane < 2 * HB_HEADS)
    pad = rowg < jnp.where(j == 0, c - N_META, 0)
    ig_t, fc, fr, ir = [], [], [], []
    for ln in lanes:
        gcap = GATE_CAP * jnp.tanh((gt_ref[rows[ln], :] + gbias_ref[...]) / GATE_CAP)
        lf_t = jnp.where(is_f & jnp.logical_not(pad), _log_sigmoid(gcap), 0.0)
        ig_t.append(jnp.where(pad, NEG_BIG, gcap))
        fc.append(_dot_pieces(_nn, tri, lf_t, 3))
    for ln in lanes:
        fr.append(_dot_pieces(_nt, eye, fc[ln], 3))
        ir.append(_dot_pieces(_nt, eye, ig_t[ln], 3))

    row = lax.broadcasted_iota(jnp.int32, (c, HA_DK), 0)
    q_h = [q_all[ln][:, hsl[h]] for ln, h in ua]
    k_h = [k_all[ln][:, hsl[h]] for ln, h in ua]
    v16 = [pa_ref[rows[ln], 2 * hk + h * HA_DV:2 * hk + (h + 1) * HA_DV].astype(BF16) for ln, h in ua]
    b_h = [b_all[ln][:, hsl[h]] for ln, h in ua]
    st = [hs_out[0, ln, h] for ln, h in ua]
    o_inter = [_nt((q_h[u] * jnp.exp(b_h[u])).astype(BF16), st[u].astype(BF16)) for u in range(len(ua))]
    for u, (ln, h) in enumerate(ua):
        b_last = b_h[u][c - 1:c, :]
        k_til = k_h[u] * jnp.exp(b_last - b_h[u])
        hs_out[0, ln, h] = st[u] * jnp.exp(b_last) + _tn(v16[u], k_til.astype(BF16))
    scores = [lmask(n_levels) * _nt(q_h[u].astype(BF16), k_h[u].astype(BF16)) for u in range(len(ua))]
    m = c // 2
    for lvl in range(n_levels):
        second = (row & m) != 0
        for u, (ln, h) in enumerate(ua):
            dec = jnp.exp(e_lvl[lvl][ln][:, hsl[h]])
            qs = jnp.where(second, q_h[u] * dec, 0.0).astype(BF16)
            ks = jnp.where(second, 0.0, k_h[u] * dec).astype(BF16)
            part = _nt(qs, ks)
            scores[u] = scores[u] + (part if lvl == 0 else lmask(lvl) * part)
        m //= 2

    r_i = lax.broadcasted_iota(jnp.int32, (c, c), 0)
    c_i = lax.broadcasted_iota(jnp.int32, (c, c), 1)
    causal = c_i <= r_i
    lane1 = lax.broadcasted_iota(jnp.int32, (1, LANE), 1)
    m_row = [mm_out[0, ln] for ln in lanes]
    m_row_new = list(m_row)
    mq = [qb_ref[rows[ln], h * HB_DQK:(h + 1) * HB_DQK] for ln, h in ub]
    mk = [kb_ref[rows[ln], h * HB_DQK:(h + 1) * HB_DQK] * (HB_DQK ** -0.5) for ln, h in ub]
    mv16 = [vb_ref[rows[ln], h * HB_DV:(h + 1) * HB_DV].astype(BF16) for ln, h in ub]
    ct = [mc_out[0, ln, h] for ln, h in ub]
    n_row = [mn_out[0, ln, h] for ln, h in ub]
    qk_raw = [_nt(mq[u].astype(BF16), mk[u].astype(BF16)) for u in range(len(ub))]
    inter = [_nn(mq[u].astype(BF16), ct[u].astype(BF16)) for u in range(len(ub))]
    qkd, m_ts, w_inters = [], [], []
    for u, (ln, h) in enumerate(ub):
        fl = HB_HEADS + h
        fc_h = fc[ln][:, fl:fl + 1]
        fr_h = fr[ln][fl:fl + 1, :]
        igr_h = ir[ln][h:h + 1, :]
        igc_h = ig_t[ln][:, h:h + 1]
        m_prev = m_row[ln][:, fl:fl + 1]
        log_d = jnp.where(causal, fc_h - fr_h + igr_h, -jnp.inf)
        log_inter = fc_h + m_prev
        m_t = jnp.maximum(log_inter, jnp.max(log_d, axis=1, keepdims=True))
        m_ts.append(m_t)
        w_inters.append(jnp.exp(log_inter - m_t))
        qkd.append(qk_raw[u] * jnp.exp(log_d - m_t))
        f_last = fc_h[c - 1:c, :]
        m_new = m_t[c - 1:c, :]
        w_s = jnp.exp(f_last - fc_h + igc_h - m_new)
        decay = jnp.exp(f_last + m_prev - m_new)
        kw = mk[u] * w_s
        mc_out[0, ln, h] = decay * ct[u] + _tn(kw.astype(BF16), mv16[u])
        mn_out[0, ln, h] = decay * n_row[u] + jnp.sum(kw, axis=0, keepdims=True)
        m_row_new[ln] = jnp.where(lane1 == fl, m_new, m_row_new[ln])
    for ln in lanes:
        mm_out[0, ln] = m_row_new[ln]

    o_a = [_nn(scores[u].astype(BF16), v16[u]) + o_inter[u] for u in range(len(ua))]
    num = [_nn(qkd[u].astype(BF16), mv16[u]) + w_inters[u] * inter[u] for u in range(len(ub))]
    for u, (ln, h) in enumerate(ua):
        ga = pa_ref[rows[ln], 3 * hk + h * HA_DV:3 * hk + (h + 1) * HA_DV]
        mrg_ref[rows[ln], h * HA_DV:(h + 1) * HA_DV] = (_rms(o_a[u], hn_ref[:, hsl[h]]) * ga).astype(BF16)
    for u, (ln, h) in enumerate(ub):
        den = jnp.sum(qkd[u], axis=1, keepdims=True) \
            + w_inters[u] * jnp.sum(mq[u] * n_row[u], axis=1, keepdims=True)
        hh = num[u] / jnp.maximum(jnp.abs(den), jnp.exp(-m_ts[u]))
        ob = ob_ref[rows[ln], h * HB_DV:(h + 1) * HB_DV]
        col0 = HA_HEADS * HA_DV + h * HB_DV
        mrg_ref[rows[ln], col0:col0 + HB_DV] = (
            _rms(hh, mnw_ref[:, h * HB_DV:(h + 1) * HB_DV]) * ob).astype(BF16)

    def write_states(hs_o, mc_o, mn_o, mm_o):
        for ln in range(LANES):
            for h in range(HA_HEADS):
                hs_o[0, ln, h] = hs_out[0, ln, h].T
            mm_o[0, ln] = mm_out[0, ln][:, HB_HEADS:2 * HB_HEADS]
        mc_o[...] = mc_out[...]
        mn_o[...] = mn_out[...]

    @pl.when(emit == 1)
    def _():
        write_states(hs_p, mc_p, mn_p, mm_p)

    @pl.when(emit == 2)
    def _():
        write_states(hs_s, mc_s, mn_s, mm_s)

    @pl.when(j == 0)
    def _():
        meta_hs[...] = hs_out[...]
        meta_mc[...] = mc_out[...]
        meta_mn[...] = mn_out[...]
        meta_mm[...] = mm_out[...]


def _even_mixer(p, tables, n_pp, n_sp, lb_logits, lb_index, hgrn_norm, mlstm_norm, gbias,
                hs_in, mc_in, mn_in, mm_in):
    rb, fs, insl, emit, pidx, sidx = tables
    c = CHUNK
    rows = LANES * c
    n_steps = rb.shape[0]
    hk = HA_HEADS * HA_DK
    v_w = HB_HEADS * HB_DV
    n_lb = lb_logits.shape[0]
    crow = jnp.concatenate([lb_logits, hgrn_norm, mlstm_norm, _pad_cols(gbias, hk)], axis=0)
    crow = jnp.pad(crow, ((0, -crow.shape[0] % 8), (0, 0)))
    lvl, lmask, n_levels = _hgrn_tables(c)
    lane_pad = lambda m: np.pad(m, ((0, 0), (0, LANE - m.shape[1])))
    mats = jnp.asarray(np.concatenate([lane_pad(lvl), lane_pad(lmask), lane_pad(np.tril(np.ones((c, c), np.float32))),
                                       np.eye(LANE, dtype=np.float32)], axis=0))
    cst = lambda shape: pl.BlockSpec(shape, lambda j, *_: (0,) * len(shape))
    zeros = lambda shape: (0,) * len(shape)
    st_in = lambda shape: pl.BlockSpec((1,) + shape, lambda j, r, f, i, *_: (i[j],) + zeros(shape))
    st_p = lambda shape: pl.BlockSpec((1,) + shape, lambda j, r, f, i, e, pi, si: (pi[j],) + zeros(shape))
    st_s = lambda shape: pl.BlockSpec((1,) + shape, lambda j, r, f, i, e, pi, si: (si[j],) + zeros(shape))
    sh_hs = (LANES, HA_HEADS, HA_DK, HA_DV)
    sh_mc = (LANES, HB_HEADS, HB_DQK, HB_DV)
    sh_mn = (LANES, HB_HEADS, 1, HB_DQK)
    sh_mm = (LANES, 1, HB_HEADS)
    io_shapes = (sh_hs, sh_mc, sh_mn, sh_mm)
    carry_shapes = ((LANES, HA_HEADS, HA_DV, HA_DK), sh_mc, sh_mn, (LANES, 1, LANE))
    grid_spec = pltpu.PrefetchScalarGridSpec(
        num_scalar_prefetch=6,
        grid=(n_steps,),
        in_specs=[pl.BlockSpec((rows, p.shape[1]), lambda j, r, *_: (r[j], 0)), cst(crow.shape), cst(mats.shape)]
                 + [st_in(sh) for sh in io_shapes],
        out_specs=[pl.BlockSpec((rows, hk + v_w), lambda j, r, *_: (r[j], 0))]
                  + [st_p(sh) for sh in io_shapes] + [st_s(sh) for sh in io_shapes],
        scratch_shapes=[pltpu.VMEM((1,) + sh, F32) for sh in carry_shapes + carry_shapes],
    )
    out_shape = ([jax.ShapeDtypeStruct((p.shape[0], hk + v_w), BF16)]
                 + [jax.ShapeDtypeStruct((n_pp,) + sh, F32) for sh in io_shapes]
                 + [jax.ShapeDtypeStruct((n_sp,) + sh, F32) for sh in io_shapes])
    return pl.pallas_call(
        functools.partial(_even_kernel, lb_index=lb_index, n_levels=n_levels, n_lb=n_lb),
        grid_spec=grid_spec,
        out_shape=out_shape,
        compiler_params=pltpu.CompilerParams(dimension_semantics=("arbitrary",),
                                             vmem_limit_bytes=VMEM_LIMIT),
        name="even_mixer",
    )(rb, fs, insl, emit, pidx, sidx, p, crow, mats, hs_in, mc_in, mn_in, mm_in)


def _gdn_kernel(rb_ref, fs_ref, insl_ref, emit_ref, pidx_ref, sidx_ref,
                p_ref, cw_ref, misc_ref, s_in, cv_in,
                mrg_ref, s_p, cv_p, s_s, cv_s,
                s_out, xpad, meta_s, meta_cv, *, n_sq, hps):
    c = CHUNK
    n_qkv, n_z = 2 * HC_QK_HEADS * HC_DK + HC_V_HEADS * HC_DV, HC_V_HEADS * HC_DV
    qkv_ref = p_ref.at[:, 0:n_qkv]
    z_ref = p_ref.at[:, n_qkv:n_qkv + n_z]
    gt_ref = p_ref.at[:, n_qkv + n_z:n_qkv + n_z + LANE]
    alog_ref, dtb_ref, gnw_ref = misc_ref.at[0:1, :], misc_ref.at[1:2, :], misc_ref.at[2:3, :]
    tri = misc_ref[8:8 + c, 0:c]
    hg = pl.program_id(0)
    j = pl.program_id(1)
    fs = fs_ref[j]
    emit = emit_ref[j]
    nk = CONV_W - 1
    base = 8 - nk

    @pl.when(fs == 1)
    def _():
        s_out[...] = jnp.zeros_like(s_out)
        xpad[:, base:8, :] = jnp.zeros((LANES, nk, xpad.shape[2]), F32)

    @pl.when(fs == 2)
    def _():
        s_out[...] = meta_s[...]
        xpad[:, base:8, :] = meta_cv[...]

    @pl.when(fs == 3)
    def _():
        s_out[...] = s_in[...]
        xpad[:, base:8, :] = cv_in[0]

    lanes = range(LANES)
    rows = [slice(ln * c, (ln + 1) * c) for ln in lanes]
    lane = lax.broadcasted_iota(jnp.int32, (c, LANE), 1)
    row = lax.broadcasted_iota(jnp.int32, (c, LANE), 0)
    first = lane < c
    s_idx = jnp.where(first, lane, lane - c)
    causal = s_idx <= row
    strict = s_idx < row
    diag = s_idx == row
    eye_c = diag.astype(F32)
    r2 = lax.broadcasted_iota(jnp.int32, (2 * c, LANE), 0)
    l2 = lax.broadcasted_iota(jnp.int32, (2 * c, LANE), 1)
    same_head = (r2 < c) == (l2 < c)
    nq = HC_QK_HEADS * HC_DK

    def col(tile, idx):
        return jnp.sum(jnp.where(lane == idx, tile, 0.0), axis=1, keepdims=True)

    def block_diag(m16):
        return jnp.where(same_head, jnp.concatenate([m16, m16], axis=0), jnp.zeros_like(m16[:1, :1]))

    act, beta_t, gc = [], [], []
    for ln in lanes:
        xpad[ln, 8:8 + c, :] = qkv_ref[rows[ln], :]
        xp = xpad[ln]
        y = cw_ref[nk:CONV_W, :] * xp[8:8 + c]
        for t in range(nk):
            y = y + cw_ref[t:t + 1, :] * pltpu.roll(xp, nk - t, 0)[8:8 + c]
        xpad[ln, base:8, :] = xp[base + c:8 + c]
        act.append(_silu(y))
        gt = gt_ref[rows[ln], :]
        beta_t.append(_sigmoid(gt))
        g_t = -jnp.exp(alog_ref[...]) * _softplus(gt + dtb_ref[...])
        gc.append(_dot_pieces(_nn, tri, g_t, 3))

    units = [(ln, p) for ln in lanes for p in range(hps)]
    nu = range(len(units))
    qn, kn, qk, kk = [], [], [], []
    for ln, p in units:
        q = act[ln][:, p * HC_DK:(p + 1) * HC_DK]
        k = act[ln][:, nq + p * HC_DK:nq + (p + 1) * HC_DK]
        qn.append(q * lax.rsqrt(jnp.sum(q * q, axis=-1, keepdims=True) + NORM_EPS) * (HC_DK ** -0.5))
        kn.append(k * lax.rsqrt(jnp.sum(k * k, axis=-1, keepdims=True) + NORM_EPS))
    for u in nu:
        kn16 = kn[u].astype(BF16)
        k2 = jnp.concatenate([kn16, kn16], axis=0)
        qk.append(_nt(qn[u].astype(BF16), k2))
        kk.append(_nt(kn16, k2))

    beta, gcol, decay, inv, pw = [], [], [], [], []
    for u, (ln, p) in enumerate(units):
        hv = 2 * (hg * hps + p)
        b_a, b_b = col(beta_t[ln], hv), col(beta_t[ln], hv + 1)
        g_a, g_b = col(gc[ln], HC_V_HEADS + hv), col(gc[ln], HC_V_HEADS + hv + 1)
        beta.append((b_a, b_b))
        gcol.append((g_a, g_b))
        g_c = jnp.where(first, g_a, g_b)
        g_r = jnp.sum(jnp.where(diag, g_c, 0.0), axis=0, keepdims=True)
        decay.append(jnp.exp(jnp.where(causal, g_c - g_r, -jnp.inf)))
        low = jnp.where(strict, jnp.where(first, b_a, b_b) * kk[u] * decay[u], 0.0)
        inv.append(eye_c - low)
        pw.append(low.astype(BF16))
    pw = [_nn(x, block_diag(x)).astype(BF16) for x in pw]
    for i in range(n_sq):
        if i < n_sq - 1:
            prod = [_nn(jnp.concatenate([x.astype(BF16), y], axis=0), block_diag(y)) for x, y in zip(inv, pw)]
            inv = [x + r[:c] for x, r in zip(inv, prod)]
            pw = [r[c:].astype(BF16) for r in prod]
        else:
            inv = [x + _nn(x.astype(BF16), block_diag(y)) for x, y in zip(inv, pw)]

    sol, eg = [], []
    for u, (ln, p) in enumerate(units):
        (b_a, b_b), (g_a, g_b) = beta[u], gcol[u]
        e_a, e_b = jnp.exp(g_a), jnp.exp(g_b)
        eg.append((e_a, e_b))
        v_a = act[ln][:, 2 * nq + 2 * p * HC_DV:2 * nq + (2 * p + 1) * HC_DV]
        v_b = act[ln][:, 2 * nq + (2 * p + 1) * HC_DV:2 * nq + (2 * p + 2) * HC_DV]
        rhs = jnp.concatenate([jnp.concatenate([v_a * b_a, kn[u] * (b_a * e_a)], axis=1),
                               jnp.concatenate([v_b * b_b, kn[u] * (b_b * e_b)], axis=1)],
                              axis=0).astype(BF16)
        sol.append((_nn(jnp.where(first, inv[u], 0.0).astype(BF16), rhs),
                    _nn(jnp.where(first, 0.0, inv[u]).astype(BF16), rhs)))
    s_old = [[s_out[0, ln, 2 * p + jj] for jj in range(2)] for ln, p in units]
    ws = []
    for u in nu:
        for jj in range(2):
            lhs = jnp.concatenate([sol[u][jj][:, HC_DV:], qn[u] * eg[u][jj]], axis=0)
            ws.append(_nn(lhs.astype(BF16), s_old[u][jj].astype(BF16)))
    for u, (ln, p) in enumerate(units):
        v_new = [sol[u][jj][:, :HC_DV] - ws[2 * u + jj][:c] for jj in range(2)]
        v_st = jnp.concatenate(v_new, axis=0).astype(BF16)
        attn = qk[u] * decay[u]
        o = (ws[2 * u][c:] + _nn(jnp.where(first, attn, 0.0).astype(BF16), v_st),
             ws[2 * u + 1][c:] + _nn(jnp.where(first, 0.0, attn).astype(BF16), v_st))
        for jj in range(2):
            n = 2 * p + jj
            g_h = gcol[u][jj]
            g_last = g_h[c - 1:c, :]
            k_til = kn[u] * jnp.exp(g_last - g_h)
            s_out[0, ln, n] = jnp.exp(g_last) * s_old[u][jj] + _tn(k_til.astype(BF16), v_new[jj].astype(BF16))
            z = z_ref[rows[ln], n * HC_DV:(n + 1) * HC_DV]
            mrg_ref[rows[ln], n * HC_DV:(n + 1) * HC_DV] = (_rms(o[jj], gnw_ref[...]) * z).astype(BF16)

    @pl.when(emit == 1)
    def _():
        s_p[...] = s_out[...]
        cv_p[0] = xpad[:, base:8, :]

    @pl.when(emit == 2)
    def _():
        s_s[...] = s_out[...]
        cv_s[0] = xpad[:, base:8, :]

    @pl.when(j == 0)
    def _():
        meta_s[...] = s_out[...]
        meta_cv[...] = xpad[:, base:8, :]


def _gdn_mixer(p, tables, n_pp, n_sp, conv_w, alog_row, dtb_row, gnw, s_in, cv_in):
    rb, fs, insl, emit, pidx, sidx = tables
    hps = HC_QK_HEADS
    c = CHUNK
    rows = LANES * c
    n_steps = rb.shape[0]
    qkv_w = 2 * HC_QK_HEADS * HC_DK + HC_V_HEADS * HC_DV
    z_w = HC_V_HEADS * HC_DV
    n_sq = int(np.log2(c)) - 1
    nk = CONV_W - 1
    tri = np.pad(np.tril(np.ones((c, c), np.float32)), ((0, 0), (0, LANE - c)))
    misc = jnp.concatenate([alog_row, dtb_row, gnw, jnp.zeros((5, LANE), F32), jnp.asarray(tri)], axis=0)
    sh_s = (1, LANES, 2 * hps, HC_DK, HC_DV)
    sh_cv = (1, LANES, nk, qkv_w)
    cst = lambda shape: pl.BlockSpec(shape, lambda h, j, *_: (0,) * len(shape))
    s_spec = lambda k: pl.BlockSpec(sh_s, lambda h, j, *t: (t[k][j], 0, 0, 0, 0))
    cv_spec = lambda k: pl.BlockSpec(sh_cv, lambda h, j, *t: (t[k][j], 0, 0, 0))
    grid_spec = pltpu.PrefetchScalarGridSpec(
        num_scalar_prefetch=6,
        grid=(1, n_steps),
        in_specs=[pl.BlockSpec((rows, p.shape[1]), lambda h, j, r, *_: (r[j], 0)),
                  cst((CONV_W, qkv_w)), cst(misc.shape),
                  s_spec(2), cv_spec(2)],
        out_specs=[pl.BlockSpec((rows, z_w), lambda h, j, r, *_: (r[j], 0)),
                   s_spec(4), cv_spec(4), s_spec(5), cv_spec(5)],
        scratch_shapes=[pltpu.VMEM(sh_s, F32),
                        pltpu.VMEM((LANES, 8 + c, qkv_w), F32),
                        pltpu.VMEM(sh_s, F32),
                        pltpu.VMEM((LANES, nk, qkv_w), F32)],
    )
    st = lambda n: [jax.ShapeDtypeStruct((n, LANES, HC_V_HEADS, HC_DK, HC_DV), F32),
                    jax.ShapeDtypeStruct((n, LANES, nk, qkv_w), F32)]
    return pl.pallas_call(
        functools.partial(_gdn_kernel, n_sq=n_sq, hps=hps),
        grid_spec=grid_spec,
        out_shape=[jax.ShapeDtypeStruct((p.shape[0], z_w), BF16)] + st(n_pp) + st(n_sp),
        compiler_params=pltpu.CompilerParams(dimension_semantics=("arbitrary", "arbitrary"),
                                             vmem_limit_bytes=VMEM_LIMIT),
        name="gdn_mixer",
    )(rb, fs, insl, emit, pidx, sidx, p, conv_w, misc, s_in, cv_in)


def _pad_cols(w, width):
    return jnp.pad(w, ((0, 0), (0, width - w.shape[1])))


def kernel(x_prompt, x_sample, state_hgrn_S, state_mlstm_C, state_mlstm_n, state_mlstm_m, state_gdn_S,
           state_gdn_conv, meta_tokens, norm_mix, norm_ffn, norm_final, even_w_in, even_w_out,
           hgrn_lb_logits, hgrn_norm, mlstm_b_i, mlstm_b_f, mlstm_norm, odd_w_in, odd_conv_w, gdn_a_log,
           gdn_dt_bias, gdn_norm, odd_w_out, ffn_w_in, ffn_w_out):
    n_b, t_len, d = x_prompt.shape
    n_s, l_s, _ = x_sample.shape
    c = CHUNK
    blk = LANES * c
    assert t_len % c == 0 and l_s % c == 0 and meta_tokens.shape[0] == N_META
    assert n_b % LANES == 0 and n_s % LANES == 0, "sequences are processed in pairs"
    assert norm_mix.shape[0] == 2, "one even (HGRN2+mLSTM) and one odd (DeltaNet) layer"
    dt = x_prompt.dtype
    rows_p, rows_s = n_b * t_len, n_s * l_s
    tm = min(DENSE_TM, int(np.gcd(rows_p, rows_s)))
    assert tm % blk == 0
    m_rows = rows_p + rows_s + tm

    meta_job = jnp.concatenate([jnp.zeros((c - N_META, d), dt), meta_tokens.astype(dt)], axis=0)
    x_meta = jnp.concatenate([meta_job, meta_job, jnp.zeros((tm - blk, d), dt)], axis=0)
    x = (x_prompt.reshape(n_b // LANES, LANES, t_len // c, c, d),
         x_sample.reshape(n_s // LANES, LANES, l_s // c, c, d), x_meta)
    tables = _step_tables(n_b // LANES, t_len // c, n_s // LANES, l_s // c, m_rows // blk)
    row = lambda v: v.reshape(1, -1).astype(F32)
    pair = lambda a: a.reshape((a.shape[0] // LANES, LANES) + a.shape[1:])

    e = 0
    n_main = 4 * HA_HEADS * HA_DK + 2 * HB_HEADS * HB_DQK + 2 * HB_HEADS * HB_DV
    w_even = _pad_cols(even_w_in[e].astype(BF16), n_main + LANE)
    hk, hv = HA_HEADS * HA_DK, HA_HEADS * HA_DV
    mq, mv = HB_HEADS * HB_DQK, HB_HEADS * HB_DV
    ident = lambda v: v
    segs_even = ((0, hk, _silu), (hk, 2 * hk, _sigmoid), (2 * hk, 2 * hk + hv, ident),
                 (2 * hk + hv, 2 * hk + 2 * hv, _silu),
                 (2 * hk + 2 * hv, n_main - mv, ident), (n_main - mv, n_main, _sigmoid),
                 (n_main, n_main + LANE, ident))
    p_even = _proj(x, row(norm_mix[0]), w_even, tm, segs_even, "proj_even")
    gbias = _pad_cols(jnp.concatenate([mlstm_b_i[e], mlstm_b_f[e]]).reshape(1, -1).astype(F32), LANE)
    np_, ns_ = n_b // LANES, n_s // LANES
    mrg0, *even_states = _even_mixer(
        p_even, tables, np_, ns_, hgrn_lb_logits.astype(F32), e, row(hgrn_norm[e]), row(mlstm_norm[e]), gbias,
        pair(state_hgrn_S[e].astype(F32)), pair(state_mlstm_C[e].astype(F32)),
        pair(state_mlstm_n[e].astype(F32)[:, :, None, :]), pair(state_mlstm_m[e].astype(F32)[:, None, :]))
    ffn_wi, ffn_w2 = ffn_w_in.astype(BF16), ffn_w_out.astype(BF16)
    x = _post(x, mrg0, (even_w_out.astype(BF16), e), row(norm_ffn[0]), (ffn_wi, 0), (ffn_w2, 0),
              row(norm_final), tm, "post_even")

    o = 0
    n_qkv = 2 * HC_QK_HEADS * HC_DK + HC_V_HEADS * HC_DV
    n_z = HC_V_HEADS * HC_DV
    w_odd = _pad_cols(odd_w_in[o].astype(BF16), n_qkv + n_z + LANE)
    segs_odd = ((0, n_qkv, ident), (n_qkv, n_qkv + n_z, _silu), (n_qkv + n_z, n_qkv + n_z + LANE, ident))
    p_odd = _proj(x, row(norm_mix[1]), w_odd, tm, segs_odd, "proj_odd")
    lane_pad = lambda v: jnp.pad(v.reshape(1, -1).astype(F32),
                                 ((0, 0), (HC_V_HEADS, LANE - 2 * HC_V_HEADS)))
    mrg1, *gdn_states = _gdn_mixer(
        p_odd, tables, np_, ns_, odd_conv_w[o].astype(F32), lane_pad(gdn_a_log[o]),
        lane_pad(gdn_dt_bias[o]), row(gdn_norm[o]), pair(state_gdn_S[o].astype(F32)),
        pair(state_gdn_conv[o].astype(F32)))
    post_odd = functools.partial(_post, x, mrg1, (odd_w_out.astype(BF16), o), row(norm_ffn[1]),
                                 (ffn_wi, 1), (ffn_w2, 1), row(norm_final), tm, final_norm=True)
    y_p = post_odd("post_odd_prompt", tile0=0, n_tiles=rows_p // tm, unpair_shape=(n_b // LANES, t_len))
    y_s = post_odd("post_odd_sample", tile0=rows_p // tm, n_tiles=rows_s // tm,
                   unpair_shape=(n_s // LANES, l_s))

    y_prompt = y_p.reshape(n_b, t_len, d)
    y_sample = y_s.reshape(n_s, l_s, d)
    unpair = lambda v: v.reshape((-1,) + v.shape[2:])[None].astype(dt)
    outs = [y_prompt, y_sample]
    for k in range(2):
        hs_o, mc_o, mn_o, mm_o = even_states[4 * k:4 * k + 4]
        gs_o, cv_o = gdn_states[2 * k:2 * k + 2]
        outs += [unpair(hs_o), unpair(mc_o), unpair(mn_o[:, :, :, 0, :]), unpair(mm_o[:, :, 0, :]),
                 unpair(gs_o), unpair(cv_o)]
    return tuple(outs)
```

```python
import functools

import numpy as np
import jax
import jax.numpy as jnp
from jax import lax
from jax.experimental import pallas as pl
from jax.experimental.pallas import tpu as pltpu

F32 = jnp.float32
BF16 = jnp.bfloat16

CHUNK = 64
LANES = 2
N_META = 16
NORM_EPS = 1e-6
GATE_CAP = 15.0
NEG_BIG = -1e30

HA_HEADS, HA_DK, HA_DV = 4, 128, 128
HB_HEADS, HB_DQK, HB_DV = 4, 64, 128
HC_QK_HEADS, HC_V_HEADS, HC_DK, HC_DV = 8, 16, 128, 128
CONV_W = 4
LANE = 128
DENSE_TM = 512
VMEM_LIMIT = 56 * 1024 * 1024


def _nn(a, b):
    return lax.dot_general(a, b, (((1,), (0,)), ((), ())), preferred_element_type=F32)


def _nt(a, b):
    return lax.dot_general(a, b, (((1,), (1,)), ((), ())), preferred_element_type=F32)


def _tn(a, b):
    return lax.dot_general(a, b, (((0,), (0,)), ((), ())), preferred_element_type=F32)


def _dot_pieces(dot, exact, x, n):
    e16 = exact.astype(BF16)
    acc = None
    for _ in range(n):
        piece = x.astype(BF16)
        part = dot(e16, piece)
        acc = part if acc is None else acc + part
        x = x - piece.astype(F32)
    return acc


def _sigmoid(x):
    return 1.0 / (1.0 + jnp.exp(-x))


def _silu(x):
    return x * _sigmoid(x)


def _softplus(x):
    return jnp.maximum(x, 0.0) + jnp.log1p(jnp.exp(-jnp.abs(x)))


def _log_sigmoid(x):
    return jnp.minimum(x, 0.0) - jnp.log1p(jnp.exp(-jnp.abs(x)))


def _rms(x, w):
    ms = jnp.mean(x * x, axis=-1, keepdims=True)
    return x * lax.rsqrt(ms + NORM_EPS) * w


def _pair_block(pairs, seq_len, tm):
    chunks = seq_len // CHUNK
    k_tile = tm // (LANES * CHUNK)
    if chunks % k_tile == 0:
        per_pair = chunks // k_tile
        return (1, LANES, k_tile, CHUNK), (lambda t: (t // per_pair, 0, t % per_pair, 0, 0)), k_tile
    assert k_tile % chunks == 0
    return (k_tile // chunks, LANES, chunks, CHUNK), (lambda t: (t, 0, 0, 0, 0)), chunks


def _tile_units(n_rows):
    return [((k * LANES + ln) * CHUNK, k, ln) for k in range(n_rows // (LANES * CHUNK)) for ln in range(LANES)]


def _load_x_tile(i, xp_ref, xs_ref, xm_ref, xt_ref, geom):
    n_pt, n_st, kc_p, kc_s = geom

    def fill(src, kc):
        for r0, k, ln in _tile_units(xt_ref.shape[0]):
            xt_ref[r0:r0 + CHUNK, :] = src[k // kc, ln, k % kc]

    @pl.when(i < n_pt)
    def _():
        fill(xp_ref, kc_p)

    @pl.when((i >= n_pt) & (i < n_pt + n_st))
    def _():
        fill(xs_ref, kc_s)

    @pl.when(i >= n_pt + n_st)
    def _():
        xt_ref[...] = xm_ref[...]


def _x_specs(xsrc, tm):
    xp5, xs5, xm = xsrc
    d = xp5.shape[-1]
    blk_p, idx_p, kc_p = _pair_block(xp5.shape[0], xp5.shape[2] * CHUNK, tm)
    blk_s, idx_s, kc_s = _pair_block(xs5.shape[0], xs5.shape[2] * CHUNK, tm)
    n_pt = xp5.shape[0] * xp5.shape[2] * LANES * CHUNK // tm
    n_st = xs5.shape[0] * xs5.shape[2] * LANES * CHUNK // tm
    specs = [pl.BlockSpec(blk_p + (d,), lambda i: idx_p(jnp.minimum(i, n_pt - 1))),
             pl.BlockSpec(blk_s + (d,), lambda i: idx_s(jnp.clip(i - n_pt, 0, n_st - 1))),
             pl.BlockSpec((tm, d), lambda i: (0, 0))]
    return specs, (n_pt, n_st, kc_p, kc_s), (n_pt + n_st + 1) * tm, d


def _proj_kernel(*refs, n_chunk, segs, geom):
    if geom is None:
        x_ref, nw_ref, w_ref, o_ref = refs
    else:
        xp_ref, xs_ref, xm_ref, nw_ref, w_ref, o_ref, x_ref = refs
        _load_x_tile(pl.program_id(0), xp_ref, xs_ref, xm_ref, x_ref, geom)
    h = _rms(x_ref[...], nw_ref[...]).astype(BF16)
    for s0, s1, fn in segs:
        for n0 in range(s0, s1, n_chunk):
            n1 = min(n0 + n_chunk, s1)
            o_ref[:, n0:n1] = fn(_nn(h, w_ref[:, n0:n1]))


def _post_kernel(*refs, hidden, h_chunk, final_norm, unpair, geom):
    if geom is None:
        x_ref, m_ref, wo_ref, nw_ref, wi_ref, w2_ref, nf_ref, o_ref, act_ref = refs
    else:
        xp_ref, xs_ref, xm_ref, m_ref, wo_ref, nw_ref, wi_ref, w2_ref, nf_ref, o_ref, act_ref, x_ref = refs
        _load_x_tile(pl.program_id(0), xp_ref, xs_ref, xm_ref, x_ref, geom)
    x1 = x_ref[...] + _nn(m_ref[...], wo_ref[...])
    h = _rms(x1, nw_ref[...]).astype(BF16)
    for c0 in range(0, hidden, h_chunk):
        c1 = min(c0 + h_chunk, hidden)
        gate = _nn(h, wi_ref[:, c0:c1])
        up = _nn(h, wi_ref[:, hidden + c0:hidden + c1])
        act_ref[:, c0:c1] = (_silu(gate) * up).astype(BF16)
    x2 = x1 + _nn(act_ref[...], w2_ref[...])
    if final_norm:
        x2 = _rms(x2, nf_ref[...])
    if unpair:
        for r0, k, ln in _tile_units(x2.shape[0]):
            o_ref[k // unpair, ln, k % unpair] = x2[r0:r0 + CHUNK]
    else:
        o_ref[...] = x2


def _const_spec(shape):
    nd = len(shape)
    return pl.BlockSpec(shape, lambda i, _n=nd: (0,) * _n)


def _cast_pad_kernel(w_ref, o_ref):
    n, n_full = w_ref.shape[1], (w_ref.shape[1] // LANE) * LANE
    o_ref[:, :n_full] = w_ref[:, :n_full].astype(BF16)
    tail = jnp.concatenate([w_ref[:, n_full:n], jnp.zeros((w_ref.shape[0], o_ref.shape[1] - n), F32)], axis=1)
    o_ref[:, n_full:] = tail.astype(BF16)


def _cast_pad(w, name):
    d, n = w.shape
    return pl.pallas_call(
        _cast_pad_kernel,
        out_shape=jax.ShapeDtypeStruct((d, -(-n // LANE) * LANE), BF16),
        compiler_params=pltpu.CompilerParams(vmem_limit_bytes=VMEM_LIMIT),
        name=name,
    )(w)


def _proj(x, nw, w, tm, segs, name):
    n = w.shape[1]
    assert segs[0][0] == 0 and segs[-1][1] == n and all(a[1] == b[0] for a, b in zip(segs, segs[1:]))
    if isinstance(x, tuple):
        x_specs, geom, m_rows, d = _x_specs(x, tm)
        x_args, scratch = list(x), [pltpu.VMEM((tm, d), F32)]
    else:
        m_rows, d = x.shape
        x_specs, geom, x_args, scratch = [pl.BlockSpec((tm, d), lambda i: (i, 0))], None, [x], []
    return pl.pallas_call(
        functools.partial(_proj_kernel, n_chunk=512, segs=segs, geom=geom),
        grid=(m_rows // tm,),
        in_specs=x_specs + [_const_spec((1, d)), _const_spec((d, n))],
        out_specs=pl.BlockSpec((tm, n), lambda i: (i, 0)),
        out_shape=jax.ShapeDtypeStruct((m_rows, n), F32),
        scratch_shapes=scratch,
        compiler_params=pltpu.CompilerParams(dimension_semantics=("arbitrary",),
                                             vmem_limit_bytes=VMEM_LIMIT),
        name=name,
    )(*x_args, nw, w)


def _post(x, mrg, wo, nw, wi, w2, nf, tm, name, final_norm=False, tile0=0, n_tiles=None, unpair_shape=None):
    (wo, l_o), (wi, l_i), (w2, l_2) = wo, wi, w2
    km = mrg.shape[1]
    hidden = w2.shape[1]
    layer_spec = lambda w, l: pl.BlockSpec((None,) + w.shape[1:], lambda i: (l, 0, 0))
    if isinstance(x, tuple):
        assert tile0 == 0 and n_tiles is None
        x_specs, geom, m_rows, d = _x_specs(x, tm)
        x_args, x_scratch = list(x), [pltpu.VMEM((tm, d), F32)]
    else:
        m_rows, d = x.shape
        x_specs, geom, x_args, x_scratch = [pl.BlockSpec((tm, d), lambda i: (tile0 + i, 0))], None, [x], []
    n_tiles = m_rows // tm if n_tiles is None else n_tiles
    unpair = 0
    if unpair_shape is None:
        out_spec = pl.BlockSpec((tm, d), lambda i: (i, 0))
        out_shape = jax.ShapeDtypeStruct((n_tiles * tm, d), F32)
    else:
        pairs, seq_len = unpair_shape
        assert pairs * seq_len * LANES == n_tiles * tm
        blk, idx, unpair = _pair_block(pairs, seq_len, tm)
        out_spec = pl.BlockSpec(blk + (d,), lambda i: idx(i))
        out_shape = jax.ShapeDtypeStruct((pairs, LANES, seq_len // CHUNK, CHUNK, d), F32)
    return pl.pallas_call(
        functools.partial(_post_kernel, hidden=hidden, h_chunk=256, final_norm=final_norm,
                          unpair=unpair, geom=geom),
        grid=(n_tiles,),
        in_specs=x_specs + [pl.BlockSpec((tm, km), lambda i: (tile0 + i, 0)),
                            layer_spec(wo, l_o),
                            _const_spec((1, d)),
                            layer_spec(wi, l_i),
                            layer_spec(w2, l_2),
                            _const_spec((1, d))],
        out_specs=out_spec,
        out_shape=out_shape,
        scratch_shapes=[pltpu.VMEM((tm, hidden), BF16)] + x_scratch,
        compiler_params=pltpu.CompilerParams(dimension_semantics=("arbitrary",),
                                             vmem_limit_bytes=VMEM_LIMIT),
        name=name,
    )(*x_args, mrg, wo, nw, wi, w2, nf)


def _step_tables(prompt_pairs, chunks_prompt, sample_pairs, chunks_sample, n_blocks):
    n_prompt = prompt_pairs * chunks_prompt
    n_sample = sample_pairs * chunks_sample
    rb, fs, insl, emit, pidx, sidx = [n_prompt + n_sample], [1], [0], [0], [0], [0]
    for b in range(prompt_pairs):
        for c in range(chunks_prompt):
            rb.append(b * chunks_prompt + c); fs.append(2 if c == 0 else 0); insl.append(0)
            emit.append(1 if c == chunks_prompt - 1 else 0); pidx.append(b); sidx.append(0)
    for q in range(sample_pairs):
        for c in range(chunks_sample):
            rb.append(n_prompt + q * chunks_sample + c); fs.append(3 if c == 0 else 0); insl.append(q)
            emit.append(2 if c == chunks_sample - 1 else 0); pidx.append(prompt_pairs - 1); sidx.append(q)
    for blk in range(n_prompt + n_sample + 1, n_blocks):
        rb.append(blk); fs.append(1); insl.append(0)
        emit.append(0); pidx.append(prompt_pairs - 1); sidx.append(sample_pairs - 1)
    to = lambda v: jnp.asarray(np.asarray(v, np.int32))
    return tuple(to(v) for v in (rb, fs, insl, emit, pidx, sidx))


def _hgrn_tables(c):
    t = np.arange(c)[:, None]
    j = np.arange(c)[None, :]
    mats, masks = [], []
    m = c // 2
    while m >= 1:
        blk = t // (2 * m)
        bd = blk * 2 * m + m - 1
        second = (t % (2 * m)) >= m
        mat = np.where(second, (j > bd) & (j <= t), (j > t) & (j <= bd)).astype(np.float32)
        mats.append(mat)
        masks.append((blk == (j // (2 * m))).astype(np.float32))
        m //= 2
    mats.append((j <= t).astype(np.float32))
    masks.append((j == t).astype(np.float32))
    return np.concatenate(mats, 0), np.concatenate(masks, 0), len(mats) - 1


def _even_kernel(rb_ref, fs_ref, insl_ref, emit_ref, pidx_ref, sidx_ref,
                 p_ref, crow_ref, mats_ref,
                 hs_in, mc_in, mn_in, mm_in,
                 mrg_ref, hs_p, mc_p, mn_p, mm_p, hs_s, mc_s, mn_s, mm_s,
                 hs_out, mc_out, mn_out, mm_out,
                 meta_hs, meta_mc, meta_mn, meta_mm,
                 *, lb_index, n_levels, n_lb):
    c = CHUNK
    wa, wqk, wv = 4 * HA_HEADS * HA_DK, HB_HEADS * HB_DQK, HB_HEADS * HB_DV
    pa_ref = p_ref.at[:, 0:wa]
    qb_ref = p_ref.at[:, wa:wa + wqk]
    kb_ref = p_ref.at[:, wa + wqk:wa + 2 * wqk]
    vb_ref = p_ref.at[:, wa + 2 * wqk:wa + 2 * wqk + wv]
    ob_ref = p_ref.at[:, wa + 2 * wqk + wv:wa + 2 * wqk + 2 * wv]
    gt_ref = p_ref.at[:, wa + 2 * wqk + 2 * wv:wa + 2 * wqk + 2 * wv + LANE]
    lbl_ref = crow_ref.at[0:n_lb, :]
    hn_ref = crow_ref.at[n_lb:n_lb + 1, :]
    mnw_ref = crow_ref.at[n_lb + 1:n_lb + 2, :]
    gbias_ref = crow_ref.at[n_lb + 2:n_lb + 3, 0:LANE]
    n_mat = (n_levels + 1) * c
    lvl_mat = lambda l: mats_ref[l * c:(l + 1) * c, 0:c]
    lmask = lambda l: mats_ref[n_mat + l * c:n_mat + (l + 1) * c, 0:c]
    tri = mats_ref[2 * n_mat:2 * n_mat + c, 0:c]
    eye = mats_ref[2 * n_mat + c:2 * n_mat + c + LANE, :]
    j = pl.program_id(0)
    fs = fs_ref[j]
    emit = emit_ref[j]

    @pl.when(fs == 1)
    def _():
        hs_out[...] = jnp.zeros_like(hs_out)
        mc_out[...] = jnp.zeros_like(mc_out)
        mn_out[...] = jnp.zeros_like(mn_out)
        mm_out[...] = jnp.zeros_like(mm_out)

    @pl.when(fs == 2)
    def _():
        hs_out[...] = meta_hs[...]
        mc_out[...] = meta_mc[...]
        mn_out[...] = meta_mn[...]
        mm_out[...] = meta_mm[...]

    @pl.when(fs == 3)
    def _():
        for ln in range(LANES):
            for h in range(HA_HEADS):
                hs_out[0, ln, h] = hs_in[0, ln, h].T
            mm_out[0, ln] = jnp.concatenate(
                [jnp.zeros((1, HB_HEADS), F32), mm_in[0, ln], jnp.zeros((1, LANE - 2 * HB_HEADS), F32)], axis=1)
        mc_out[...] = mc_in[...]
        mn_out[...] = mn_in[...]

    lanes = range(LANES)
    rows = [slice(ln * c, (ln + 1) * c) for ln in lanes]
    hk = HA_HEADS * HA_DK
    ua = [(ln, h) for ln in lanes for h in range(HA_HEADS)]
    ub = [(ln, h) for ln in lanes for h in range(HB_HEADS)]
    hsl = [slice(h * HA_DK, (h + 1) * HA_DK) for h in range(HA_HEADS)]

    lbl = lbl_ref[...]
    ex = jnp.exp(lbl - jnp.max(lbl, axis=0, keepdims=True))
    sm = ex / jnp.sum(ex, axis=0, keepdims=True)
    lb = jnp.sum(sm[:lb_index + 1], axis=0, keepdims=True)
    log_f, k_all, q_all, e_lvl, b_all = [], [], [], [], []
    for ln in lanes:
        sf = pa_ref[rows[ln], hk:2 * hk]
        log_f.append(jnp.log(lb + (1.0 - lb) * sf))
        k_all.append((1.0 - lb) * (1.0 - sf))
        q_all.append(pa_ref[rows[ln], 0:hk])
    for ln in lanes:
        b_all.append(_dot_pieces(_nn, lvl_mat(n_levels), log_f[ln], 2))
    for lvl in range(n_levels):
        e_lvl.append([_dot_pieces(_nn, lvl_mat(lvl), log_f[ln], 2) for ln in lanes])

    lane = lax.broadcasted_iota(jnp.int32, (c, LANE), 1)
    rowg = lax.broadcasted_iota(jnp.int32, (c, LANE), 0)
    is_f = (lane >= HB_HEADS) & (lane < 2 * HB_HEADS)
    pad = rowg < jnp.where(j == 0, c - N_META, 0)
    ig_t, fc, fr, ir = [], [], [], []
    for ln in lanes:
        gcap = GATE_CAP * jnp.tanh((gt_ref[rows[ln], :] + gbias_ref[...]) / GATE_CAP)
        lf_t = jnp.where(is_f & jnp.logical_not(pad), _log_sigmoid(gcap), 0.0)
        ig_t.append(jnp.where(pad, NEG_BIG, gcap))
        fc.append(_dot_pieces(_nn, tri, lf_t, 3))
    for ln in lanes:
        fr.append(_dot_pieces(_nt, eye, fc[ln], 3))
        ir.append(_dot_pieces(_nt, eye, ig_t[ln], 3))

    row = lax.broadcasted_iota(jnp.int32, (c, HA_DK), 0)
    q_h = [q_all[ln][:, hsl[h]] for ln, h in ua]
    k_h = [k_all[ln][:, hsl[h]] for ln, h in ua]
    v16 = [pa_ref[rows[ln], 2 * hk + h * HA_DV:2 * hk + (h + 1) * HA_DV].astype(BF16) for ln, h in ua]
    b_h = [b_all[ln][:, hsl[h]] for ln, h in ua]
    st = [hs_out[0, ln, h] for ln, h in ua]
    o_inter = [_nt((q_h[u] * jnp.exp(b_h[u])).astype(BF16), st[u].astype(BF16)) for u in range(len(ua))]
    for u, (ln, h) in enumerate(ua):
        b_last = b_h[u][c - 1:c, :]
        k_til = k_h[u] * jnp.exp(b_last - b_h[u])
        hs_out[0, ln, h] = st[u] * jnp.exp(b_last) + _tn(v16[u], k_til.astype(BF16))
    scores = [lmask(n_levels) * _nt(q_h[u].astype(BF16), k_h[u].astype(BF16)) for u in range(len(ua))]
    m = c // 2
    for lvl in range(n_levels):
        second = (row & m) != 0
        for u, (ln, h) in enumerate(ua):
            dec = jnp.exp(e_lvl[lvl][ln][:, hsl[h]])
            qs = jnp.where(second, q_h[u] * dec, 0.0).astype(BF16)
            ks = jnp.where(second, 0.0, k_h[u] * dec).astype(BF16)
            part = _nt(qs, ks)
            scores[u] = scores[u] + (part if lvl == 0 else lmask(lvl) * part)
        m //= 2

    r_i = lax.broadcasted_iota(jnp.int32, (c, c), 0)
    c_i = lax.broadcasted_iota(jnp.int32, (c, c), 1)
    causal = c_i <= r_i
    lane1 = lax.broadcasted_iota(jnp.int32, (1, LANE), 1)
    m_row = [mm_out[0, ln] for ln in lanes]
    m_row_new = list(m_row)
    mq = [qb_ref[rows[ln], h * HB_DQK:(h + 1) * HB_DQK] for ln, h in ub]
    mk = [kb_ref[rows[ln], h * HB_DQK:(h + 1) * HB_DQK] * (HB_DQK ** -0.5) for ln, h in ub]
    mv16 = [vb_ref[rows[ln], h * HB_DV:(h + 1) * HB_DV].astype(BF16) for ln, h in ub]
    ct = [mc_out[0, ln, h] for ln, h in ub]
    n_row = [mn_out[0, ln, h] for ln, h in ub]
    qk_raw = [_nt(mq[u].astype(BF16), mk[u].astype(BF16)) for u in range(len(ub))]
    inter = [_nn(mq[u].astype(BF16), ct[u].astype(BF16)) for u in range(len(ub))]
    qkd, m_ts, w_inters = [], [], []
    for u, (ln, h) in enumerate(ub):
        fl = HB_HEADS + h
        fc_h = fc[ln][:, fl:fl + 1]
        fr_h = fr[ln][fl:fl + 1, :]
        igr_h = ir[ln][h:h + 1, :]
        igc_h = ig_t[ln][:, h:h + 1]
        m_prev = m_row[ln][:, fl:fl + 1]
        log_d = jnp.where(causal, fc_h - fr_h + igr_h, -jnp.inf)
        log_inter = fc_h + m_prev
        m_t = jnp.maximum(log_inter, jnp.max(log_d, axis=1, keepdims=True))
        m_ts.append(m_t)
        w_inters.append(jnp.exp(log_inter - m_t))
        qkd.append(qk_raw[u] * jnp.exp(log_d - m_t))
        f_last = fc_h[c - 1:c, :]
        m_new = m_t[c - 1:c, :]
        w_s = jnp.exp(f_last - fc_h + igc_h - m_new)
        decay = jnp.exp(f_last + m_prev - m_new)
        kw = mk[u] * w_s
        mc_out[0, ln, h] = decay * ct[u] + _tn(kw.astype(BF16), mv16[u])
        mn_out[0, ln, h] = decay * n_row[u] + jnp.sum(kw, axis=0, keepdims=True)
        m_row_new[ln] = jnp.where(lane1 == fl, m_new, m_row_new[ln])
    for ln in lanes:
        mm_out[0, ln] = m_row_new[ln]

    o_a = [_nn(scores[u].astype(BF16), v16[u]) + o_inter[u] for u in range(len(ua))]
    num = [_nn(qkd[u].astype(BF16), mv16[u]) + w_inters[u] * inter[u] for u in range(len(ub))]
    for u, (ln, h) in enumerate(ua):
        ga = pa_ref[rows[ln], 3 * hk + h * HA_DV:3 * hk + (h + 1) * HA_DV]
        mrg_ref[rows[ln], h * HA_DV:(h + 1) * HA_DV] = (_rms(o_a[u], hn_ref[:, hsl[h]]) * ga).astype(BF16)
    for u, (ln, h) in enumerate(ub):
        den = jnp.sum(qkd[u], axis=1, keepdims=True) \
            + w_inters[u] * jnp.sum(mq[u] * n_row[u], axis=1, keepdims=True)
        hh = num[u] / jnp.maximum(jnp.abs(den), jnp.exp(-m_ts[u]))
        ob = ob_ref[rows[ln], h * HB_DV:(h + 1) * HB_DV]
        col0 = HA_HEADS * HA_DV + h * HB_DV
        mrg_ref[rows[ln], col0:col0 + HB_DV] = (
            _rms(hh, mnw_ref[:, h * HB_DV:(h + 1) * HB_DV]) * ob).astype(BF16)

    def write_states(hs_o, mc_o, mn_o, mm_o):
        for ln in range(LANES):
            for h in range(HA_HEADS):
                hs_o[0, ln, h] = hs_out[0, ln, h].T
            mm_o[0, ln] = mm_out[0, ln][:, HB_HEADS:2 * HB_HEADS]
        mc_o[...] = mc_out[...]
        mn_o[...] = mn_out[...]

    @pl.when(emit == 1)
    def _():
        write_states(hs_p, mc_p, mn_p, mm_p)

    @pl.when(emit == 2)
    def _():
        write_states(hs_s, mc_s, mn_s, mm_s)

    @pl.when(j == 0)
    def _():
        meta_hs[...] = hs_out[...]
        meta_mc[...] = mc_out[...]
        meta_mn[...] = mn_out[...]
        meta_mm[...] = mm_out[...]


def _even_mixer(p, tables, n_pp, n_sp, lb_logits, lb_index, hgrn_norm, mlstm_norm, gbias,
                hs_in, mc_in, mn_in, mm_in):
    rb, fs, insl, emit, pidx, sidx = tables
    c = CHUNK
    rows = LANES * c
    n_steps = rb.shape[0]
    hk = HA_HEADS * HA_DK
    v_w = HB_HEADS * HB_DV
    n_lb = lb_logits.shape[0]
    crow = jnp.concatenate([lb_logits, hgrn_norm, mlstm_norm, _pad_cols(gbias, hk)], axis=0)
    crow = jnp.pad(crow, ((0, -crow.shape[0] % 8), (0, 0)))
    lvl, lmask, n_levels = _hgrn_tables(c)
    lane_pad = lambda m: np.pad(m, ((0, 0), (0, LANE - m.shape[1])))
    mats = jnp.asarray(np.concatenate([lane_pad(lvl), lane_pad(lmask), lane_pad(np.tril(np.ones((c, c), np.float32))),
                                       np.eye(LANE, dtype=np.float32)], axis=0))
    cst = lambda shape: pl.BlockSpec(shape, lambda j, *_: (0,) * len(shape))
    zeros = lambda shape: (0,) * len(shape)
    st_in = lambda shape: pl.BlockSpec((1,) + shape, lambda j, r, f, i, *_: (i[j],) + zeros(shape))
    st_p = lambda shape: pl.BlockSpec((1,) + shape, lambda j, r, f, i, e, pi, si: (pi[j],) + zeros(shape))
    st_s = lambda shape: pl.BlockSpec((1,) + shape, lambda j, r, f, i, e, pi, si: (si[j],) + zeros(shape))
    sh_hs = (LANES, HA_HEADS, HA_DK, HA_DV)
    sh_mc = (LANES, HB_HEADS, HB_DQK, HB_DV)
    sh_mn = (LANES, HB_HEADS, 1, HB_DQK)
    sh_mm = (LANES, 1, HB_HEADS)
    io_shapes = (sh_hs, sh_mc, sh_mn, sh_mm)
    carry_shapes = ((LANES, HA_HEADS, HA_DV, HA_DK), sh_mc, sh_mn, (LANES, 1, LANE))
    grid_spec = pltpu.PrefetchScalarGridSpec(
        num_scalar_prefetch=6,
        grid=(n_steps,),
        in_specs=[pl.BlockSpec((rows, p.shape[1]), lambda j, r, *_: (r[j], 0)), cst(crow.shape), cst(mats.shape)]
                 + [st_in(sh) for sh in io_shapes],
        out_specs=[pl.BlockSpec((rows, hk + v_w), lambda j, r, *_: (r[j], 0))]
                  + [st_p(sh) for sh in io_shapes] + [st_s(sh) for sh in io_shapes],
        scratch_shapes=[pltpu.VMEM((1,) + sh, F32) for sh in carry_shapes + carry_shapes],
    )
    out_shape = ([jax.ShapeDtypeStruct((p.shape[0], hk + v_w), BF16)]
                 + [jax.ShapeDtypeStruct((n_pp,) + sh, F32) for sh in io_shapes]
                 + [jax.ShapeDtypeStruct((n_sp,) + sh, F32) for sh in io_shapes])
    return pl.pallas_call(
        functools.partial(_even_kernel, lb_index=lb_index, n_levels=n_levels, n_lb=n_lb),
        grid_spec=grid_spec,
        out_shape=out_shape,
        compiler_params=pltpu.CompilerParams(dimension_semantics=("arbitrary",),
                                             vmem_limit_bytes=VMEM_LIMIT),
        name="even_mixer",
    )(rb, fs, insl, emit, pidx, sidx, p, crow, mats, hs_in, mc_in, mn_in, mm_in)


def _gdn_kernel(rb_ref, fs_ref, insl_ref, emit_ref, pidx_ref, sidx_ref,
                p_ref, cw_ref, misc_ref, s_in, cv_in,
                mrg_ref, s_p, cv_p, s_s, cv_s,
                s_out, xpad, meta_s, meta_cv, *, n_sq):
    c = CHUNK
    n_qkv, n_z = 2 * HC_QK_HEADS * HC_DK + HC_V_HEADS * HC_DV, HC_V_HEADS * HC_DV
    qkv_ref = p_ref.at[:, 0:n_qkv]
    z_ref = p_ref.at[:, n_qkv:n_qkv + n_z]
    gt_ref = p_ref.at[:, n_qkv + n_z:n_qkv + n_z + LANE]
    alog_ref, dtb_ref, gnw_ref = misc_ref.at[0:1, :], misc_ref.at[1:2, :], misc_ref.at[2:3, :]
    tri = misc_ref[8:8 + c, 0:c]
    hps = HC_QK_HEADS
    j = pl.program_id(0)
    fs = fs_ref[j]
    emit = emit_ref[j]
    nk = CONV_W - 1
    base = 8 - nk

    @pl.when(fs == 1)
    def _():
        s_out[...] = jnp.zeros_like(s_out)
        xpad[:, base:8, :] = jnp.zeros((LANES, nk, xpad.shape[2]), F32)

    @pl.when(fs == 2)
    def _():
        s_out[...] = meta_s[...]
        xpad[:, base:8, :] = meta_cv[...]

    @pl.when(fs == 3)
    def _():
        s_out[...] = s_in[...]
        xpad[:, base:8, :] = cv_in[0]

    lanes = range(LANES)
    rows = [slice(ln * c, (ln + 1) * c) for ln in lanes]
    lane = lax.broadcasted_iota(jnp.int32, (c, LANE), 1)
    row = lax.broadcasted_iota(jnp.int32, (c, LANE), 0)
    first = lane < c
    s_idx = jnp.where(first, lane, lane - c)
    causal = s_idx <= row
    strict = s_idx < row
    diag = s_idx == row
    eye_c = diag.astype(F32)
    r2 = lax.broadcasted_iota(jnp.int32, (2 * c, LANE), 0)
    l2 = lax.broadcasted_iota(jnp.int32, (2 * c, LANE), 1)
    same_head = (r2 < c) == (l2 < c)
    nq = HC_QK_HEADS * HC_DK

    def col(tile, idx):
        return jnp.sum(jnp.where(lane == idx, tile, 0.0), axis=1, keepdims=True)

    def block_diag(m16):
        return jnp.where(same_head, jnp.concatenate([m16, m16], axis=0), jnp.zeros_like(m16[:1, :1]))

    act, beta_t, gc = [], [], []
    for ln in lanes:
        xpad[ln, 8:8 + c, :] = qkv_ref[rows[ln], :]
        xp = xpad[ln]
        y = cw_ref[nk:CONV_W, :] * xp[8:8 + c]
        for t in range(nk):
            y = y + cw_ref[t:t + 1, :] * pltpu.roll(xp, nk - t, 0)[8:8 + c]
        xpad[ln, base:8, :] = xp[base + c:8 + c]
        act.append(_silu(y))
        gt = gt_ref[rows[ln], :]
        beta_t.append(_sigmoid(gt))
        g_t = -jnp.exp(alog_ref[...]) * _softplus(gt + dtb_ref[...])
        gc.append(_dot_pieces(_nn, tri, g_t, 3))

    units = [(ln, p) for ln in lanes for p in range(hps)]
    nu = range(len(units))
    qn, kn, qk, kk = [], [], [], []
    for ln, p in units:
        q = act[ln][:, p * HC_DK:(p + 1) * HC_DK]
        k = act[ln][:, nq + p * HC_DK:nq + (p + 1) * HC_DK]
        qn.append(q * lax.rsqrt(jnp.sum(q * q, axis=-1, keepdims=True) + NORM_EPS) * (HC_DK ** -0.5))
        kn.append(k * lax.rsqrt(jnp.sum(k * k, axis=-1, keepdims=True) + NORM_EPS))
    for u in nu:
        kn16 = kn[u].astype(BF16)
        k2 = jnp.concatenate([kn16, kn16], axis=0)
        qk.append(_nt(qn[u].astype(BF16), k2))
        kk.append(_nt(kn16, k2))

    beta, gcol, decay, inv, pw = [], [], [], [], []
    for u, (ln, p) in enumerate(units):
        hv = 2 * p
        b_a, b_b = col(beta_t[ln], hv), col(beta_t[ln], hv + 1)
        g_a, g_b = col(gc[ln], HC_V_HEADS + hv), col(gc[ln], HC_V_HEADS + hv + 1)
        beta.append((b_a, b_b))
        gcol.append((g_a, g_b))
        g_c = jnp.where(first, g_a, g_b)
        g_r = jnp.sum(jnp.where(diag, g_c, 0.0), axis=0, keepdims=True)
        decay.append(jnp.exp(jnp.where(causal, g_c - g_r, -jnp.inf)))
        low = jnp.where(strict, jnp.where(first, b_a, b_b) * kk[u] * decay[u], 0.0)
        inv.append(eye_c - low)
        pw.append(low.astype(BF16))
    pw = [_nn(x, block_diag(x)).astype(BF16) for x in pw]
    for i in range(n_sq):
        if i < n_sq - 1:
            prod = [_nn(jnp.concatenate([x.astype(BF16), y], axis=0), block_diag(y)) for x, y in zip(inv, pw)]
            inv = [x + r[:c] for x, r in zip(inv, prod)]
            pw = [r[c:].astype(BF16) for r in prod]
        else:
            inv = [x + _nn(x.astype(BF16), block_diag(y)) for x, y in zip(inv, pw)]

    sol, eg = [], []
    for u, (ln, p) in enumerate(units):
        (b_a, b_b), (g_a, g_b) = beta[u], gcol[u]
        e_a, e_b = jnp.exp(g_a), jnp.exp(g_b)
        eg.append((e_a, e_b))
        v_a = act[ln][:, 2 * nq + 2 * p * HC_DV:2 * nq + (2 * p + 1) * HC_DV]
        v_b = act[ln][:, 2 * nq + (2 * p + 1) * HC_DV:2 * nq + (2 * p + 2) * HC_DV]
        rhs = jnp.concatenate([jnp.concatenate([v_a * b_a, kn[u] * (b_a * e_a)], axis=1),
                               jnp.concatenate([v_b * b_b, kn[u] * (b_b * e_b)], axis=1)],
                              axis=0).astype(BF16)
        sol.append((_nn(jnp.where(first, inv[u], 0.0).astype(BF16), rhs),
                    _nn(jnp.where(first, 0.0, inv[u]).astype(BF16), rhs)))
    s_old = [[s_out[0, ln, 2 * p + jj] for jj in range(2)] for ln, p in units]
    ws = []
    for u in nu:
        for jj in range(2):
            lhs = jnp.concatenate([sol[u][jj][:, HC_DV:], qn[u] * eg[u][jj]], axis=0)
            ws.append(_nn(lhs.astype(BF16), s_old[u][jj].astype(BF16)))
    for u, (ln, p) in enumerate(units):
        v_new = [sol[u][jj][:, :HC_DV] - ws[2 * u + jj][:c] for jj in range(2)]
        v_st = jnp.concatenate(v_new, axis=0).astype(BF16)
        attn = qk[u] * decay[u]
        o = (ws[2 * u][c:] + _nn(jnp.where(first, attn, 0.0).astype(BF16), v_st),
             ws[2 * u + 1][c:] + _nn(jnp.where(first, 0.0, attn).astype(BF16), v_st))
        for jj in range(2):
            n = 2 * p + jj
            g_h = gcol[u][jj]
            g_last = g_h[c - 1:c, :]
            k_til = kn[u] * jnp.exp(g_last - g_h)
            s_out[0, ln, n] = jnp.exp(g_last) * s_old[u][jj] + _tn(k_til.astype(BF16), v_new[jj].astype(BF16))
            z = z_ref[rows[ln], n * HC_DV:(n + 1) * HC_DV]
            mrg_ref[rows[ln], n * HC_DV:(n + 1) * HC_DV] = (_rms(o[jj], gnw_ref[...]) * z).astype(BF16)

    @pl.when(emit == 1)
    def _():
        s_p[...] = s_out[...]
        cv_p[0] = xpad[:, base:8, :]

    @pl.when(emit == 2)
    def _():
        s_s[...] = s_out[...]
        cv_s[0] = xpad[:, base:8, :]

    @pl.when(j == 0)
    def _():
        meta_s[...] = s_out[...]
        meta_cv[...] = xpad[:, base:8, :]


def _gdn_mixer(p, tables, n_pp, n_sp, conv_w, alog_row, dtb_row, gnw, s_in, cv_in):
    rb, fs, insl, emit, pidx, sidx = tables
    c = CHUNK
    rows = LANES * c
    n_steps = rb.shape[0]
    qkv_w = 2 * HC_QK_HEADS * HC_DK + HC_V_HEADS * HC_DV
    z_w = HC_V_HEADS * HC_DV
    n_sq = int(np.log2(c)) - 1
    nk = CONV_W - 1
    tri = np.pad(np.tril(np.ones((c, c), np.float32)), ((0, 0), (0, LANE - c)))
    misc = jnp.concatenate([alog_row, dtb_row, gnw, jnp.zeros((5, LANE), F32), jnp.asarray(tri)], axis=0)
    sh_s = (1, LANES, HC_V_HEADS, HC_DK, HC_DV)
    sh_cv = (1, LANES, nk, qkv_w)
    cst = lambda shape: pl.BlockSpec(shape, lambda j, *_: (0,) * len(shape))
    s_spec = lambda k: pl.BlockSpec(sh_s, lambda j, *t: (t[k][j], 0, 0, 0, 0))
    cv_spec = lambda k: pl.BlockSpec(sh_cv, lambda j, *t: (t[k][j], 0, 0, 0))
    grid_spec = pltpu.PrefetchScalarGridSpec(
        num_scalar_prefetch=6,
        grid=(n_steps,),
        in_specs=[pl.BlockSpec((rows, p.shape[1]), lambda j, r, *_: (r[j], 0)),
                  cst((CONV_W, qkv_w)), cst(misc.shape),
                  s_spec(2), cv_spec(2)],
        out_specs=[pl.BlockSpec((rows, z_w), lambda j, r, *_: (r[j], 0)),
                   s_spec(4), cv_spec(4), s_spec(5), cv_spec(5)],
        scratch_shapes=[pltpu.VMEM(sh_s, F32),
                        pltpu.VMEM((LANES, 8 + c, qkv_w), F32),
                        pltpu.VMEM(sh_s, F32),
                        pltpu.VMEM((LANES, nk, qkv_w), F32)],
    )
    st = lambda n: [jax.ShapeDtypeStruct((n, LANES, HC_V_HEADS, HC_DK, HC_DV), F32),
                    jax.ShapeDtypeStruct((n, LANES, nk, qkv_w), F32)]
    return pl.pallas_call(
        functools.partial(_gdn_kernel, n_sq=n_sq),
        grid_spec=grid_spec,
        out_shape=[jax.ShapeDtypeStruct((p.shape[0], z_w), BF16)] + st(n_pp) + st(n_sp),
        compiler_params=pltpu.CompilerParams(dimension_semantics=("arbitrary",),
                                             vmem_limit_bytes=VMEM_LIMIT),
        name="gdn_mixer",
    )(rb, fs, insl, emit, pidx, sidx, p, conv_w, misc, s_in, cv_in)


def _pad_cols(w, width):
    return jnp.pad(w, ((0, 0), (0, width - w.shape[1])))


def kernel(x_prompt, x_sample, state_hgrn_S, state_mlstm_C, state_mlstm_n, state_mlstm_m, state_gdn_S,
           state_gdn_conv, meta_tokens, norm_mix, norm_ffn, norm_final, even_w_in, even_w_out,
           hgrn_lb_logits, hgrn_norm, mlstm_b_i, mlstm_b_f, mlstm_norm, odd_w_in, odd_conv_w, gdn_a_log,
           gdn_dt_bias, gdn_norm, odd_w_out, ffn_w_in, ffn_w_out):
    n_b, t_len, d = x_prompt.shape
    n_s, l_s, _ = x_sample.shape
    c = CHUNK
    blk = LANES * c
    assert t_len % c == 0 and l_s % c == 0 and meta_tokens.shape[0] == N_META
    assert n_b % LANES == 0 and n_s % LANES == 0, "sequences are processed in pairs"
    assert norm_mix.shape[0] == 2, "one even (HGRN2+mLSTM) and one odd (DeltaNet) layer"
    dt = x_prompt.dtype
    rows_p, rows_s = n_b * t_len, n_s * l_s
    tm = min(DENSE_TM, int(np.gcd(rows_p, rows_s)))
    assert tm % blk == 0
    m_rows = rows_p + rows_s + tm

    meta_job = jnp.concatenate([jnp.zeros((c - N_META, d), dt), meta_tokens.astype(dt)], axis=0)
    x_meta = jnp.concatenate([meta_job, meta_job, jnp.zeros((tm - blk, d), dt)], axis=0)
    x = (x_prompt.reshape(n_b // LANES, LANES, t_len // c, c, d),
         x_sample.reshape(n_s // LANES, LANES, l_s // c, c, d), x_meta)
    tables = _step_tables(n_b // LANES, t_len // c, n_s // LANES, l_s // c, m_rows // blk)
    row = lambda v: v.reshape(1, -1).astype(F32)
    pair = lambda a: a.reshape((a.shape[0] // LANES, LANES) + a.shape[1:])

    e = 0
    n_main = 4 * HA_HEADS * HA_DK + 2 * HB_HEADS * HB_DQK + 2 * HB_HEADS * HB_DV
    w_even = _cast_pad(even_w_in[e], "cast_w_even")
    hk, hv = HA_HEADS * HA_DK, HA_HEADS * HA_DV
    mq, mv = HB_HEADS * HB_DQK, HB_HEADS * HB_DV
    ident = lambda v: v
    segs_even = ((0, hk, _silu), (hk, 2 * hk, _sigmoid), (2 * hk, 2 * hk + hv, ident),
                 (2 * hk + hv, 2 * hk + 2 * hv, _silu),
                 (2 * hk + 2 * hv, n_main - mv, ident), (n_main - mv, n_main, _sigmoid),
                 (n_main, n_main + LANE, ident))
    p_even = _proj(x, row(norm_mix[0]), w_even, tm, segs_even, "proj_even")
    gbias = _pad_cols(jnp.concatenate([mlstm_b_i[e], mlstm_b_f[e]]).reshape(1, -1).astype(F32), LANE)
    np_, ns_ = n_b // LANES, n_s // LANES
    mrg0, *even_states = _even_mixer(
        p_even, tables, np_, ns_, hgrn_lb_logits.astype(F32), e, row(hgrn_norm[e]), row(mlstm_norm[e]), gbias,
        pair(state_hgrn_S[e].astype(F32)), pair(state_mlstm_C[e].astype(F32)),
        pair(state_mlstm_n[e].astype(F32)[:, :, None, :]), pair(state_mlstm_m[e].astype(F32)[:, None, :]))
    ffn_wi, ffn_w2 = ffn_w_in.astype(BF16), ffn_w_out.astype(BF16)
    x = _post(x, mrg0, (even_w_out.astype(BF16), e), row(norm_ffn[0]), (ffn_wi, 0), (ffn_w2, 0),
              row(norm_final), tm, "post_even")

    o = 0
    n_qkv = 2 * HC_QK_HEADS * HC_DK + HC_V_HEADS * HC_DV
    n_z = HC_V_HEADS * HC_DV
    w_odd = _cast_pad(odd_w_in[o], "cast_w_odd")
    segs_odd = ((0, n_qkv, ident), (n_qkv, n_qkv + n_z, _silu), (n_qkv + n_z, n_qkv + n_z + LANE, ident))
    p_odd = _proj(x, row(norm_mix[1]), w_odd, tm, segs_odd, "proj_odd")
    lane_pad = lambda v: jnp.pad(v.reshape(1, -1).astype(F32),
                                 ((0, 0), (HC_V_HEADS, LANE - 2 * HC_V_HEADS)))
    mrg1, *gdn_states = _gdn_mixer(
        p_odd, tables, np_, ns_, odd_conv_w[o].astype(F32), lane_pad(gdn_a_log[o]),
        lane_pad(gdn_dt_bias[o]), row(gdn_norm[o]), pair(state_gdn_S[o].astype(F32)),
        pair(state_gdn_conv[o].astype(F32)))
    post_odd = functools.partial(_post, x, mrg1, (odd_w_out.astype(BF16), o), row(norm_ffn[1]),
                                 (ffn_wi, 1), (ffn_w2, 1), row(norm_final), tm, final_norm=True)
    y_p = post_odd("post_odd_prompt", tile0=0, n_tiles=rows_p // tm, unpair_shape=(n_b // LANES, t_len))
    y_s = post_odd("post_odd_sample", tile0=rows_p // tm, n_tiles=rows_s // tm,
                   unpair_shape=(n_s // LANES, l_s))

    y_prompt = y_p.reshape(n_b, t_len, d)
    y_sample = y_s.reshape(n_s, l_s, d)
    unpair = lambda v: v.reshape((-1,) + v.shape[2:])[None].astype(dt)
    outs = [y_prompt, y_sample]
    for k in range(2):
        hs_o, mc_o, mn_o, mm_o = even_states[4 * k:4 * k + 4]
        gs_o, cv_o = gdn_states[2 * k:2 * k + 2]
        outs += [unpair(hs_o), unpair(mc_o), unpair(mn_o[:, :, :, 0, :]), unpair(mm_o[:, :, 0, :]),
                 unpair(gs_o), unpair(cv_o)]
    return tuple(outs)
```

```python
import functools

import numpy as np
import jax
import jax.numpy as jnp
from jax import lax
from jax.experimental import pallas as pl
from jax.experimental.pallas import tpu as pltpu

F32 = jnp.float32
BF16 = jnp.bfloat16

CHUNK = 64
LANES = 2
N_META = 16
NORM_EPS = 1e-6
GATE_CAP = 15.0
NEG_BIG = -1e30

HA_HEADS, HA_DK, HA_DV = 4, 128, 128
HB_HEADS, HB_DQK, HB_DV = 4, 64, 128
HC_QK_HEADS, HC_V_HEADS, HC_DK, HC_DV = 8, 16, 128, 128
CONV_W = 4
LANE = 128
DENSE_TM = 512
VMEM_LIMIT = 56 * 1024 * 1024


def _nn(a, b):
    return lax.dot_general(a, b, (((1,), (0,)), ((), ())), preferred_element_type=F32)


def _nt(a, b):
    return lax.dot_general(a, b, (((1,), (1,)), ((), ())), preferred_element_type=F32)


def _tn(a, b):
    return lax.dot_general(a, b, (((0,), (0,)), ((), ())), preferred_element_type=F32)


def _dot_pieces(dot, exact, x, n):
    e16 = exact.astype(BF16)
    acc = None
    for _ in range(n):
        piece = x.astype(BF16)
        part = dot(e16, piece)
        acc = part if acc is None else acc + part
        x = x - piece.astype(F32)
    return acc


def _sigmoid(x):
    return 1.0 / (1.0 + jnp.exp(-x))


def _silu(x):
    return x * _sigmoid(x)


def _softplus(x):
    return jnp.maximum(x, 0.0) + jnp.log1p(jnp.exp(-jnp.abs(x)))


def _log_sigmoid(x):
    return jnp.minimum(x, 0.0) - jnp.log1p(jnp.exp(-jnp.abs(x)))


def _rms(x, w):
    ms = jnp.mean(x * x, axis=-1, keepdims=True)
    return x * lax.rsqrt(ms + NORM_EPS) * w


def _pair_block(pairs, seq_len, tm):
    chunks = seq_len // CHUNK
    k_tile = tm // (LANES * CHUNK)
    if chunks % k_tile == 0:
        per_pair = chunks // k_tile
        return (1, LANES, k_tile, CHUNK), (lambda t: (t // per_pair, 0, t % per_pair, 0, 0)), k_tile
    assert k_tile % chunks == 0
    return (k_tile // chunks, LANES, chunks, CHUNK), (lambda t: (t, 0, 0, 0, 0)), chunks


def _tile_units(n_rows):
    return [((k * LANES + ln) * CHUNK, k, ln) for k in range(n_rows // (LANES * CHUNK)) for ln in range(LANES)]


def _load_x_tile(i, xp_ref, xs_ref, xm_ref, xt_ref, geom):
    n_pt, n_st, kc_p, kc_s = geom

    def fill(src, kc):
        for r0, k, ln in _tile_units(xt_ref.shape[0]):
            xt_ref[r0:r0 + CHUNK, :] = src[k // kc, ln, k % kc]

    @pl.when(i < n_pt)
    def _():
        fill(xp_ref, kc_p)

    @pl.when((i >= n_pt) & (i < n_pt + n_st))
    def _():
        fill(xs_ref, kc_s)

    @pl.when(i >= n_pt + n_st)
    def _():
        xt_ref[...] = xm_ref[...]


def _x_specs(xsrc, tm):
    xp5, xs5, xm = xsrc
    d = xp5.shape[-1]
    blk_p, idx_p, kc_p = _pair_block(xp5.shape[0], xp5.shape[2] * CHUNK, tm)
    blk_s, idx_s, kc_s = _pair_block(xs5.shape[0], xs5.shape[2] * CHUNK, tm)
    n_pt = xp5.shape[0] * xp5.shape[2] * LANES * CHUNK // tm
    n_st = xs5.shape[0] * xs5.shape[2] * LANES * CHUNK // tm
    specs = [pl.BlockSpec(blk_p + (d,), lambda i: idx_p(jnp.minimum(i, n_pt - 1))),
             pl.BlockSpec(blk_s + (d,), lambda i: idx_s(jnp.clip(i - n_pt, 0, n_st - 1))),
             pl.BlockSpec((tm, d), lambda i: (0, 0))]
    return specs, (n_pt, n_st, kc_p, kc_s), (n_pt + n_st + 1) * tm, d


def _proj_kernel(*refs, n_chunk, segs, geom):
    if geom is None:
        x_ref, nw_ref, w_ref, o_ref = refs
    else:
        xp_ref, xs_ref, xm_ref, nw_ref, w_ref, o_ref, x_ref = refs
        _load_x_tile(pl.program_id(0), xp_ref, xs_ref, xm_ref, x_ref, geom)
    h = _rms(x_ref[...], nw_ref[...]).astype(BF16)
    for s0, s1, fn in segs:
        for n0 in range(s0, s1, n_chunk):
            n1 = min(n0 + n_chunk, s1)
            o_ref[:, n0:n1] = fn(_nn(h, w_ref[:, n0:n1]))


def _post_kernel(*refs, hidden, h_chunk, final_norm, unpair, geom):
    if geom is None:
        x_ref, m_ref, wo_ref, nw_ref, wi_ref, w2_ref, nf_ref, o_ref, act_ref = refs
    else:
        xp_ref, xs_ref, xm_ref, m_ref, wo_ref, nw_ref, wi_ref, w2_ref, nf_ref, o_ref, act_ref, x_ref = refs
        _load_x_tile(pl.program_id(0), xp_ref, xs_ref, xm_ref, x_ref, geom)
    x1 = x_ref[...] + _nn(m_ref[...], wo_ref[...])
    h = _rms(x1, nw_ref[...]).astype(BF16)
    for c0 in range(0, hidden, h_chunk):
        c1 = min(c0 + h_chunk, hidden)
        gate = _nn(h, wi_ref[:, c0:c1])
        up = _nn(h, wi_ref[:, hidden + c0:hidden + c1])
        act_ref[:, c0:c1] = (_silu(gate) * up).astype(BF16)
    x2 = x1 + _nn(act_ref[...], w2_ref[...])
    if final_norm:
        x2 = _rms(x2, nf_ref[...])
    if unpair:
        for r0, k, ln in _tile_units(x2.shape[0]):
            o_ref[k // unpair, ln, k % unpair] = x2[r0:r0 + CHUNK]
    else:
        o_ref[...] = x2


def _const_spec(shape):
    nd = len(shape)
    return pl.BlockSpec(shape, lambda i, _n=nd: (0,) * _n)


def _proj(x, nw, w, tm, segs, name):
    n = w.shape[1]
    assert segs[0][0] == 0 and segs[-1][1] == n and all(a[1] == b[0] for a, b in zip(segs, segs[1:]))
    if isinstance(x, tuple):
        x_specs, geom, m_rows, d = _x_specs(x, tm)
        x_args, scratch = list(x), [pltpu.VMEM((tm, d), F32)]
    else:
        m_rows, d = x.shape
        x_specs, geom, x_args, scratch = [pl.BlockSpec((tm, d), lambda i: (i, 0))], None, [x], []
    return pl.pallas_call(
        functools.partial(_proj_kernel, n_chunk=512, segs=segs, geom=geom),
        grid=(m_rows // tm,),
        in_specs=x_specs + [_const_spec((1, d)), _const_spec((d, n))],
        out_specs=pl.BlockSpec((tm, n), lambda i: (i, 0)),
        out_shape=jax.ShapeDtypeStruct((m_rows, n), F32),
        scratch_shapes=scratch,
        compiler_params=pltpu.CompilerParams(dimension_semantics=("arbitrary",),
                                             vmem_limit_bytes=VMEM_LIMIT),
        name=name,
    )(*x_args, nw, w)


def _post(x, mrg, wo, nw, wi, w2, nf, tm, name, final_norm=False, tile0=0, n_tiles=None, unpair_shape=None):
    (wo, l_o), (wi, l_i), (w2, l_2) = wo, wi, w2
    km = mrg.shape[1]
    hidden = w2.shape[1]
    layer_spec = lambda w, l: pl.BlockSpec((None,) + w.shape[1:], lambda i: (l, 0, 0))
    if isinstance(x, tuple):
        assert tile0 == 0 and n_tiles is None
        x_specs, geom, m_rows, d = _x_specs(x, tm)
        x_args, x_scratch = list(x), [pltpu.VMEM((tm, d), F32)]
    else:
        m_rows, d = x.shape
        x_specs, geom, x_args, x_scratch = [pl.BlockSpec((tm, d), lambda i: (tile0 + i, 0))], None, [x], []
    n_tiles = m_rows // tm if n_tiles is None else n_tiles
    unpair = 0
    if unpair_shape is None:
        out_spec = pl.BlockSpec((tm, d), lambda i: (i, 0))
        out_shape = jax.ShapeDtypeStruct((n_tiles * tm, d), F32)
    else:
        pairs, seq_len = unpair_shape
        assert pairs * seq_len * LANES == n_tiles * tm
        blk, idx, unpair = _pair_block(pairs, seq_len, tm)
        out_spec = pl.BlockSpec(blk + (d,), lambda i: idx(i))
        out_shape = jax.ShapeDtypeStruct((pairs, LANES, seq_len // CHUNK, CHUNK, d), F32)
    return pl.pallas_call(
        functools.partial(_post_kernel, hidden=hidden, h_chunk=256, final_norm=final_norm,
                          unpair=unpair, geom=geom),
        grid=(n_tiles,),
        in_specs=x_specs + [pl.BlockSpec((tm, km), lambda i: (tile0 + i, 0)),
                            layer_spec(wo, l_o),
                            _const_spec((1, d)),
                            layer_spec(wi, l_i),
                            layer_spec(w2, l_2),
                            _const_spec((1, d))],
        out_specs=out_spec,
        out_shape=out_shape,
        scratch_shapes=[pltpu.VMEM((tm, hidden), BF16)] + x_scratch,
        compiler_params=pltpu.CompilerParams(dimension_semantics=("arbitrary",),
                                             vmem_limit_bytes=VMEM_LIMIT),
        name=name,
    )(*x_args, mrg, wo, nw, wi, w2, nf)


def _step_tables(prompt_pairs, chunks_prompt, sample_pairs, chunks_sample, n_blocks):
    n_prompt = prompt_pairs * chunks_prompt
    n_sample = sample_pairs * chunks_sample
    rb, fs, insl, emit, pidx, sidx = [n_prompt + n_sample], [1], [0], [0], [0], [0]
    for b in range(prompt_pairs):
        for c in range(chunks_prompt):
            rb.append(b * chunks_prompt + c); fs.append(2 if c == 0 else 0); insl.append(0)
            emit.append(1 if c == chunks_prompt - 1 else 0); pidx.append(b); sidx.append(0)
    for q in range(sample_pairs):
        for c in range(chunks_sample):
            rb.append(n_prompt + q * chunks_sample + c); fs.append(3 if c == 0 else 0); insl.append(q)
            emit.append(2 if c == chunks_sample - 1 else 0); pidx.append(prompt_pairs - 1); sidx.append(q)
    for blk in range(n_prompt + n_sample + 1, n_blocks):
        rb.append(blk); fs.append(4); insl.append(0)
        emit.append(0); pidx.append(prompt_pairs - 1); sidx.append(sample_pairs - 1)
    to = lambda v: jnp.asarray(np.asarray(v, np.int32))
    return tuple(to(v) for v in (rb, fs, insl, emit, pidx, sidx))


def _hgrn_tables(c):
    t = np.arange(c)[:, None]
    j = np.arange(c)[None, :]
    mats, masks = [], []
    m = c // 2
    while m >= 1:
        blk = t // (2 * m)
        bd = blk * 2 * m + m - 1
        second = (t % (2 * m)) >= m
        mat = np.where(second, (j > bd) & (j <= t), (j > t) & (j <= bd)).astype(np.float32)
        mats.append(mat)
        masks.append((blk == (j // (2 * m))).astype(np.float32))
        m //= 2
    mats.append((j <= t).astype(np.float32))
    masks.append((j == t).astype(np.float32))
    return np.concatenate(mats, 0), np.concatenate(masks, 0), len(mats) - 1


def _even_step(rb_ref, fs_ref, insl_ref, emit_ref, pidx_ref, sidx_ref,
                 p_ref, crow_ref, mats_ref,
                 hs_in, mc_in, mn_in, mm_in,
                 mrg_ref, hs_p, mc_p, mn_p, mm_p, hs_s, mc_s, mn_s, mm_s,
                 hs_out, mc_out, mn_out, mm_out,
                 meta_hs, meta_mc, meta_mn, meta_mm,
                 *, lb_index, n_levels, n_lb):
    c = CHUNK
    wa, wqk, wv = 4 * HA_HEADS * HA_DK, HB_HEADS * HB_DQK, HB_HEADS * HB_DV
    pa_ref = p_ref.at[:, 0:wa]
    qb_ref = p_ref.at[:, wa:wa + wqk]
    kb_ref = p_ref.at[:, wa + wqk:wa + 2 * wqk]
    vb_ref = p_ref.at[:, wa + 2 * wqk:wa + 2 * wqk + wv]
    ob_ref = p_ref.at[:, wa + 2 * wqk + wv:wa + 2 * wqk + 2 * wv]
    gt_ref = p_ref.at[:, wa + 2 * wqk + 2 * wv:wa + 2 * wqk + 2 * wv + LANE]
    lbl_ref = crow_ref.at[0:n_lb, :]
    hn_ref = crow_ref.at[n_lb:n_lb + 1, :]
    mnw_ref = crow_ref.at[n_lb + 1:n_lb + 2, :]
    gbias_ref = crow_ref.at[n_lb + 2:n_lb + 3, 0:LANE]
    n_mat = (n_levels + 1) * c
    lvl_mat = lambda l: mats_ref[l * c:(l + 1) * c, 0:c]
    lmask = lambda l: mats_ref[n_mat + l * c:n_mat + (l + 1) * c, 0:c]
    tri = mats_ref[2 * n_mat:2 * n_mat + c, 0:c]
    eye = mats_ref[2 * n_mat + c:2 * n_mat + c + LANE, :]
    j = pl.program_id(0)
    fs = fs_ref[j]
    emit = emit_ref[j]

    @pl.when(fs == 1)
    def _():
        hs_out[...] = jnp.zeros_like(hs_out)
        mc_out[...] = jnp.zeros_like(mc_out)
        mn_out[...] = jnp.zeros_like(mn_out)
        mm_out[...] = jnp.zeros_like(mm_out)

    @pl.when(fs == 2)
    def _():
        hs_out[...] = meta_hs[...]
        mc_out[...] = meta_mc[...]
        mn_out[...] = meta_mn[...]
        mm_out[...] = meta_mm[...]

    @pl.when(fs == 3)
    def _():
        for ln in range(LANES):
            for h in range(HA_HEADS):
                hs_out[0, ln, h] = hs_in[0, ln, h].T
            mm_out[0, ln] = jnp.concatenate(
                [jnp.zeros((1, HB_HEADS), F32), mm_in[0, ln], jnp.zeros((1, LANE - 2 * HB_HEADS), F32)], axis=1)
        mc_out[...] = mc_in[...]
        mn_out[...] = mn_in[...]

    lanes = range(LANES)
    rows = [slice(ln * c, (ln + 1) * c) for ln in lanes]
    hk = HA_HEADS * HA_DK
    ua = [(ln, h) for ln in lanes for h in range(HA_HEADS)]
    ub = [(ln, h) for ln in lanes for h in range(HB_HEADS)]
    hsl = [slice(h * HA_DK, (h + 1) * HA_DK) for h in range(HA_HEADS)]

    lbl = lbl_ref[...]
    ex = jnp.exp(lbl - jnp.max(lbl, axis=0, keepdims=True))
    sm = ex / jnp.sum(ex, axis=0, keepdims=True)
    lb = jnp.sum(sm[:lb_index + 1], axis=0, keepdims=True)
    log_f, k_all, q_all, e_lvl, b_all = [], [], [], [], []
    for ln in lanes:
        sf = pa_ref[rows[ln], hk:2 * hk]
        log_f.append(jnp.log(lb + (1.0 - lb) * sf))
        k_all.append((1.0 - lb) * (1.0 - sf))
        q_all.append(pa_ref[rows[ln], 0:hk])
    for ln in lanes:
        b_all.append(_dot_pieces(_nn, lvl_mat(n_levels), log_f[ln], 2))
    for lvl in range(n_levels):
        e_lvl.append([_dot_pieces(_nn, lvl_mat(lvl), log_f[ln], 2) for ln in lanes])

    lane = lax.broadcasted_iota(jnp.int32, (c, LANE), 1)
    rowg = lax.broadcasted_iota(jnp.int32, (c, LANE), 0)
    is_f = (lane >= HB_HEADS) & (lane < 2 * HB_HEADS)
    pad = rowg < jnp.where(j == 0, c - N_META, 0)
    ig_t, fc, fr, ir = [], [], [], []
    for ln in lanes:
        gcap = GATE_CAP * jnp.tanh((gt_ref[rows[ln], :] + gbias_ref[...]) / GATE_CAP)
        lf_t = jnp.where(is_f & jnp.logical_not(pad), _log_sigmoid(gcap), 0.0)
        ig_t.append(jnp.where(pad, NEG_BIG, gcap))
        fc.append(_dot_pieces(_nn, tri, lf_t, 3))
    for ln in lanes:
        fr.append(_dot_pieces(_nt, eye, fc[ln], 3))
        ir.append(_dot_pieces(_nt, eye, ig_t[ln], 3))

    row = lax.broadcasted_iota(jnp.int32, (c, HA_DK), 0)
    q_h = [q_all[ln][:, hsl[h]] for ln, h in ua]
    k_h = [k_all[ln][:, hsl[h]] for ln, h in ua]
    v16 = [pa_ref[rows[ln], 2 * hk + h * HA_DV:2 * hk + (h + 1) * HA_DV].astype(BF16) for ln, h in ua]
    b_h = [b_all[ln][:, hsl[h]] for ln, h in ua]
    st = [hs_out[0, ln, h] for ln, h in ua]
    o_inter = [_nt((q_h[u] * jnp.exp(b_h[u])).astype(BF16), st[u].astype(BF16)) for u in range(len(ua))]
    for u, (ln, h) in enumerate(ua):
        b_last = b_h[u][c - 1:c, :]
        k_til = k_h[u] * jnp.exp(b_last - b_h[u])
        hs_out[0, ln, h] = st[u] * jnp.exp(b_last) + _tn(v16[u], k_til.astype(BF16))
    scores = [lmask(n_levels) * _nt(q_h[u].astype(BF16), k_h[u].astype(BF16)) for u in range(len(ua))]
    m = c // 2
    for lvl in range(n_levels):
        second = (row & m) != 0
        for u, (ln, h) in enumerate(ua):
            dec = jnp.exp(e_lvl[lvl][ln][:, hsl[h]])
            qs = jnp.where(second, q_h[u] * dec, 0.0).astype(BF16)
            ks = jnp.where(second, 0.0, k_h[u] * dec).astype(BF16)
            part = _nt(qs, ks)
            scores[u] = scores[u] + (part if lvl == 0 else lmask(lvl) * part)
        m //= 2

    r_i = lax.broadcasted_iota(jnp.int32, (c, c), 0)
    c_i = lax.broadcasted_iota(jnp.int32, (c, c), 1)
    causal = c_i <= r_i
    lane1 = lax.broadcasted_iota(jnp.int32, (1, LANE), 1)
    m_row = [mm_out[0, ln] for ln in lanes]
    m_row_new = list(m_row)
    mq = [qb_ref[rows[ln], h * HB_DQK:(h + 1) * HB_DQK] for ln, h in ub]
    mk = [kb_ref[rows[ln], h * HB_DQK:(h + 1) * HB_DQK] * (HB_DQK ** -0.5) for ln, h in ub]
    mv16 = [vb_ref[rows[ln], h * HB_DV:(h + 1) * HB_DV].astype(BF16) for ln, h in ub]
    ct = [mc_out[0, ln, h] for ln, h in ub]
    n_row = [mn_out[0, ln, h] for ln, h in ub]
    qk_raw = [_nt(mq[u].astype(BF16), mk[u].astype(BF16)) for u in range(len(ub))]
    inter = [_nn(mq[u].astype(BF16), ct[u].astype(BF16)) for u in range(len(ub))]
    qkd, m_ts, w_inters = [], [], []
    for u, (ln, h) in enumerate(ub):
        fl = HB_HEADS + h
        fc_h = fc[ln][:, fl:fl + 1]
        fr_h = fr[ln][fl:fl + 1, :]
        igr_h = ir[ln][h:h + 1, :]
        igc_h = ig_t[ln][:, h:h + 1]
        m_prev = m_row[ln][:, fl:fl + 1]
        log_d = jnp.where(causal, fc_h - fr_h + igr_h, -jnp.inf)
        log_inter = fc_h + m_prev
        m_t = jnp.maximum(log_inter, jnp.max(log_d, axis=1, keepdims=True))
        m_ts.append(m_t)
        w_inters.append(jnp.exp(log_inter - m_t))
        qkd.append(qk_raw[u] * jnp.exp(log_d - m_t))
        f_last = fc_h[c - 1:c, :]
        m_new = m_t[c - 1:c, :]
        w_s = jnp.exp(f_last - fc_h + igc_h - m_new)
        decay = jnp.exp(f_last + m_prev - m_new)
        kw = mk[u] * w_s
        mc_out[0, ln, h] = decay * ct[u] + _tn(kw.astype(BF16), mv16[u])
        mn_out[0, ln, h] = decay * n_row[u] + jnp.sum(kw, axis=0, keepdims=True)
        m_row_new[ln] = jnp.where(lane1 == fl, m_new, m_row_new[ln])
    for ln in lanes:
        mm_out[0, ln] = m_row_new[ln]

    o_a = [_nn(scores[u].astype(BF16), v16[u]) + o_inter[u] for u in range(len(ua))]
    num = [_nn(qkd[u].astype(BF16), mv16[u]) + w_inters[u] * inter[u] for u in range(len(ub))]
    for u, (ln, h) in enumerate(ua):
        ga = pa_ref[rows[ln], 3 * hk + h * HA_DV:3 * hk + (h + 1) * HA_DV]
        mrg_ref[rows[ln], h * HA_DV:(h + 1) * HA_DV] = (_rms(o_a[u], hn_ref[:, hsl[h]]) * ga).astype(BF16)
    for u, (ln, h) in enumerate(ub):
        den = jnp.sum(qkd[u], axis=1, keepdims=True) \
            + w_inters[u] * jnp.sum(mq[u] * n_row[u], axis=1, keepdims=True)
        hh = num[u] / jnp.maximum(jnp.abs(den), jnp.exp(-m_ts[u]))
        ob = ob_ref[rows[ln], h * HB_DV:(h + 1) * HB_DV]
        col0 = HA_HEADS * HA_DV + h * HB_DV
        mrg_ref[rows[ln], col0:col0 + HB_DV] = (
            _rms(hh, mnw_ref[:, h * HB_DV:(h + 1) * HB_DV]) * ob).astype(BF16)

    def write_states(hs_o, mc_o, mn_o, mm_o):
        for ln in range(LANES):
            for h in range(HA_HEADS):
                hs_o[0, ln, h] = hs_out[0, ln, h].T
            mm_o[0, ln] = mm_out[0, ln][:, HB_HEADS:2 * HB_HEADS]
        mc_o[...] = mc_out[...]
        mn_o[...] = mn_out[...]

    @pl.when(emit == 1)
    def _():
        write_states(hs_p, mc_p, mn_p, mm_p)

    @pl.when(emit == 2)
    def _():
        write_states(hs_s, mc_s, mn_s, mm_s)

    @pl.when(j == 0)
    def _():
        meta_hs[...] = hs_out[...]
        meta_mc[...] = mc_out[...]
        meta_mn[...] = mn_out[...]
        meta_mm[...] = mm_out[...]


def _skip_zero_blocks(step, n_in):
    def body(*refs, **kw):
        fs = refs[1][pl.program_id(0)]
        mrg_ref = refs[n_in]

        @pl.when(fs == 4)
        def _():
            mrg_ref[...] = jnp.zeros_like(mrg_ref)

        @pl.when(fs != 4)
        def _():
            step(*refs, **kw)
    return body


def _even_mixer(p, tables, n_pp, n_sp, lb_logits, lb_index, hgrn_norm, mlstm_norm, gbias,
                hs_in, mc_in, mn_in, mm_in):
    rb, fs, insl, emit, pidx, sidx = tables
    c = CHUNK
    rows = LANES * c
    n_steps = rb.shape[0]
    hk = HA_HEADS * HA_DK
    v_w = HB_HEADS * HB_DV
    n_lb = lb_logits.shape[0]
    crow = jnp.concatenate([lb_logits, hgrn_norm, mlstm_norm, _pad_cols(gbias, hk)], axis=0)
    crow = jnp.pad(crow, ((0, -crow.shape[0] % 8), (0, 0)))
    lvl, lmask, n_levels = _hgrn_tables(c)
    lane_pad = lambda m: np.pad(m, ((0, 0), (0, LANE - m.shape[1])))
    mats = jnp.asarray(np.concatenate([lane_pad(lvl), lane_pad(lmask), lane_pad(np.tril(np.ones((c, c), np.float32))),
                                       np.eye(LANE, dtype=np.float32)], axis=0))
    cst = lambda shape: pl.BlockSpec(shape, lambda j, *_: (0,) * len(shape))
    zeros = lambda shape: (0,) * len(shape)
    st_in = lambda shape: pl.BlockSpec((1,) + shape, lambda j, r, f, i, *_: (i[j],) + zeros(shape))
    st_p = lambda shape: pl.BlockSpec((1,) + shape, lambda j, r, f, i, e, pi, si: (pi[j],) + zeros(shape))
    st_s = lambda shape: pl.BlockSpec((1,) + shape, lambda j, r, f, i, e, pi, si: (si[j],) + zeros(shape))
    sh_hs = (LANES, HA_HEADS, HA_DK, HA_DV)
    sh_mc = (LANES, HB_HEADS, HB_DQK, HB_DV)
    sh_mn = (LANES, HB_HEADS, 1, HB_DQK)
    sh_mm = (LANES, 1, HB_HEADS)
    io_shapes = (sh_hs, sh_mc, sh_mn, sh_mm)
    carry_shapes = ((LANES, HA_HEADS, HA_DV, HA_DK), sh_mc, sh_mn, (LANES, 1, LANE))
    grid_spec = pltpu.PrefetchScalarGridSpec(
        num_scalar_prefetch=6,
        grid=(n_steps,),
        in_specs=[pl.BlockSpec((rows, p.shape[1]), lambda j, r, *_: (r[j], 0)), cst(crow.shape), cst(mats.shape)]
                 + [st_in(sh) for sh in io_shapes],
        out_specs=[pl.BlockSpec((rows, hk + v_w), lambda j, r, *_: (r[j], 0))]
                  + [st_p(sh) for sh in io_shapes] + [st_s(sh) for sh in io_shapes],
        scratch_shapes=[pltpu.VMEM((1,) + sh, F32) for sh in carry_shapes + carry_shapes],
    )
    out_shape = ([jax.ShapeDtypeStruct((p.shape[0], hk + v_w), BF16)]
                 + [jax.ShapeDtypeStruct((n_pp,) + sh, F32) for sh in io_shapes]
                 + [jax.ShapeDtypeStruct((n_sp,) + sh, F32) for sh in io_shapes])
    return pl.pallas_call(
        functools.partial(_skip_zero_blocks(_even_step, 6 + 7), lb_index=lb_index, n_levels=n_levels, n_lb=n_lb),
        grid_spec=grid_spec,
        out_shape=out_shape,
        compiler_params=pltpu.CompilerParams(dimension_semantics=("arbitrary",),
                                             vmem_limit_bytes=VMEM_LIMIT),
        name="even_mixer",
    )(rb, fs, insl, emit, pidx, sidx, p, crow, mats, hs_in, mc_in, mn_in, mm_in)


def _gdn_step(rb_ref, fs_ref, insl_ref, emit_ref, pidx_ref, sidx_ref,
                p_ref, cw_ref, misc_ref, s_in, cv_in,
                mrg_ref, s_p, cv_p, s_s, cv_s,
                s_out, xpad, meta_s, meta_cv, *, n_sq):
    c = CHUNK
    n_qkv, n_z = 2 * HC_QK_HEADS * HC_DK + HC_V_HEADS * HC_DV, HC_V_HEADS * HC_DV
    qkv_ref = p_ref.at[:, 0:n_qkv]
    z_ref = p_ref.at[:, n_qkv:n_qkv + n_z]
    gt_ref = p_ref.at[:, n_qkv + n_z:n_qkv + n_z + LANE]
    alog_ref, dtb_ref, gnw_ref = misc_ref.at[0:1, :], misc_ref.at[1:2, :], misc_ref.at[2:3, :]
    tri = misc_ref[8:8 + c, 0:c]
    hps = HC_QK_HEADS
    j = pl.program_id(0)
    fs = fs_ref[j]
    emit = emit_ref[j]
    nk = CONV_W - 1
    base = 8 - nk

    @pl.when(fs == 1)
    def _():
        s_out[...] = jnp.zeros_like(s_out)
        xpad[:, base:8, :] = jnp.zeros((LANES, nk, xpad.shape[2]), F32)

    @pl.when(fs == 2)
    def _():
        s_out[...] = meta_s[...]
        xpad[:, base:8, :] = meta_cv[...]

    @pl.when(fs == 3)
    def _():
        s_out[...] = s_in[...]
        xpad[:, base:8, :] = cv_in[0]

    lanes = range(LANES)
    rows = [slice(ln * c, (ln + 1) * c) for ln in lanes]
    lane = lax.broadcasted_iota(jnp.int32, (c, LANE), 1)
    row = lax.broadcasted_iota(jnp.int32, (c, LANE), 0)
    first = lane < c
    s_idx = jnp.where(first, lane, lane - c)
    causal = s_idx <= row
    strict = s_idx < row
    diag = s_idx == row
    eye_c = diag.astype(F32)
    r2 = lax.broadcasted_iota(jnp.int32, (2 * c, LANE), 0)
    l2 = lax.broadcasted_iota(jnp.int32, (2 * c, LANE), 1)
    same_head = (r2 < c) == (l2 < c)
    nq = HC_QK_HEADS * HC_DK

    def col(tile, idx):
        return jnp.sum(jnp.where(lane == idx, tile, 0.0), axis=1, keepdims=True)

    def block_diag(m16):
        return jnp.where(same_head, jnp.concatenate([m16, m16], axis=0), jnp.zeros_like(m16[:1, :1]))

    act, beta_t, gc = [], [], []
    for ln in lanes:
        xpad[ln, 8:8 + c, :] = qkv_ref[rows[ln], :]
        xp = xpad[ln]
        y = cw_ref[nk:CONV_W, :] * xp[8:8 + c]
        for t in range(nk):
            y = y + cw_ref[t:t + 1, :] * pltpu.roll(xp, nk - t, 0)[8:8 + c]
        xpad[ln, base:8, :] = xp[base + c:8 + c]
        act.append(_silu(y))
        gt = gt_ref[rows[ln], :]
        beta_t.append(_sigmoid(gt))
        g_t = -jnp.exp(alog_ref[...]) * _softplus(gt + dtb_ref[...])
        gc.append(_dot_pieces(_nn, tri, g_t, 3))

    units = [(ln, p) for ln in lanes for p in range(hps)]
    nu = range(len(units))
    qn, kn, qk, kk = [], [], [], []
    for ln, p in units:
        q = act[ln][:, p * HC_DK:(p + 1) * HC_DK]
        k = act[ln][:, nq + p * HC_DK:nq + (p + 1) * HC_DK]
        qn.append(q * lax.rsqrt(jnp.sum(q * q, axis=-1, keepdims=True) + NORM_EPS) * (HC_DK ** -0.5))
        kn.append(k * lax.rsqrt(jnp.sum(k * k, axis=-1, keepdims=True) + NORM_EPS))
    for u in nu:
        kn16 = kn[u].astype(BF16)
        k2 = jnp.concatenate([kn16, kn16], axis=0)
        qk.append(_nt(qn[u].astype(BF16), k2))
        kk.append(_nt(kn16, k2))

    beta, gcol, decay, inv, pw = [], [], [], [], []
    for u, (ln, p) in enumerate(units):
        hv = 2 * p
        b_a, b_b = col(beta_t[ln], hv), col(beta_t[ln], hv + 1)
        g_a, g_b = col(gc[ln], HC_V_HEADS + hv), col(gc[ln], HC_V_HEADS + hv + 1)
        beta.append((b_a, b_b))
        gcol.append((g_a, g_b))
        g_c = jnp.where(first, g_a, g_b)
        g_r = jnp.sum(jnp.where(diag, g_c, 0.0), axis=0, keepdims=True)
        decay.append(jnp.exp(jnp.where(causal, g_c - g_r, -jnp.inf)))
        low = jnp.where(strict, jnp.where(first, b_a, b_b) * kk[u] * decay[u], 0.0)
        inv.append(eye_c - low)
        pw.append(low.astype(BF16))
    pw = [_nn(x, block_diag(x)).astype(BF16) for x in pw]
    for i in range(n_sq):
        if i < n_sq - 1:
            prod = [_nn(jnp.concatenate([x.astype(BF16), y], axis=0), block_diag(y)) for x, y in zip(inv, pw)]
            inv = [x + r[:c] for x, r in zip(inv, prod)]
            pw = [r[c:].astype(BF16) for r in prod]
        else:
            inv = [x + _nn(x.astype(BF16), block_diag(y)) for x, y in zip(inv, pw)]

    sol, eg = [], []
    for u, (ln, p) in enumerate(units):
        (b_a, b_b), (g_a, g_b) = beta[u], gcol[u]
        e_a, e_b = jnp.exp(g_a), jnp.exp(g_b)
        eg.append((e_a, e_b))
        v_a = act[ln][:, 2 * nq + 2 * p * HC_DV:2 * nq + (2 * p + 1) * HC_DV]
        v_b = act[ln][:, 2 * nq + (2 * p + 1) * HC_DV:2 * nq + (2 * p + 2) * HC_DV]
        rhs = jnp.concatenate([jnp.concatenate([v_a * b_a, kn[u] * (b_a * e_a)], axis=1),
                               jnp.concatenate([v_b * b_b, kn[u] * (b_b * e_b)], axis=1)],
                              axis=0).astype(BF16)
        sol.append((_nn(jnp.where(first, inv[u], 0.0).astype(BF16), rhs),
                    _nn(jnp.where(first, 0.0, inv[u]).astype(BF16), rhs)))
    s_old = [[s_out[0, ln, 2 * p + jj] for jj in range(2)] for ln, p in units]
    ws = []
    for u in nu:
        for jj in range(2):
            lhs = jnp.concatenate([sol[u][jj][:, HC_DV:], qn[u] * eg[u][jj]], axis=0)
            ws.append(_nn(lhs.astype(BF16), s_old[u][jj].astype(BF16)))
    for u, (ln, p) in enumerate(units):
        v_new = [sol[u][jj][:, :HC_DV] - ws[2 * u + jj][:c] for jj in range(2)]
        v_st = jnp.concatenate(v_new, axis=0).astype(BF16)
        attn = qk[u] * decay[u]
        o = (ws[2 * u][c:] + _nn(jnp.where(first, attn, 0.0).astype(BF16), v_st),
             ws[2 * u + 1][c:] + _nn(jnp.where(first, 0.0, attn).astype(BF16), v_st))
        for jj in range(2):
            n = 2 * p + jj
            g_h = gcol[u][jj]
            g_last = g_h[c - 1:c, :]
            k_til = kn[u] * jnp.exp(g_last - g_h)
            s_out[0, ln, n] = jnp.exp(g_last) * s_old[u][jj] + _tn(k_til.astype(BF16), v_new[jj].astype(BF16))
            z = z_ref[rows[ln], n * HC_DV:(n + 1) * HC_DV]
            mrg_ref[rows[ln], n * HC_DV:(n + 1) * HC_DV] = (_rms(o[jj], gnw_ref[...]) * z).astype(BF16)

    @pl.when(emit == 1)
    def _():
        s_p[...] = s_out[...]
        cv_p[0] = xpad[:, base:8, :]

    @pl.when(emit == 2)
    def _():
        s_s[...] = s_out[...]
        cv_s[0] = xpad[:, base:8, :]

    @pl.when(j == 0)
    def _():
        meta_s[...] = s_out[...]
        meta_cv[...] = xpad[:, base:8, :]


def _gdn_mixer(p, tables, n_pp, n_sp, conv_w, alog_row, dtb_row, gnw, s_in, cv_in):
    rb, fs, insl, emit, pidx, sidx = tables
    c = CHUNK
    rows = LANES * c
    n_steps = rb.shape[0]
    qkv_w = 2 * HC_QK_HEADS * HC_DK + HC_V_HEADS * HC_DV
    z_w = HC_V_HEADS * HC_DV
    n_sq = int(np.log2(c)) - 1
    nk = CONV_W - 1
    tri = np.pad(np.tril(np.ones((c, c), np.float32)), ((0, 0), (0, LANE - c)))
    misc = jnp.concatenate([alog_row, dtb_row, gnw, jnp.zeros((5, LANE), F32), jnp.asarray(tri)], axis=0)
    sh_s = (1, LANES, HC_V_HEADS, HC_DK, HC_DV)
    sh_cv = (1, LANES, nk, qkv_w)
    cst = lambda shape: pl.BlockSpec(shape, lambda j, *_: (0,) * len(shape))
    s_spec = lambda k: pl.BlockSpec(sh_s, lambda j, *t: (t[k][j], 0, 0, 0, 0))
    cv_spec = lambda k: pl.BlockSpec(sh_cv, lambda j, *t: (t[k][j], 0, 0, 0))
    grid_spec = pltpu.PrefetchScalarGridSpec(
        num_scalar_prefetch=6,
        grid=(n_steps,),
        in_specs=[pl.BlockSpec((rows, p.shape[1]), lambda j, r, *_: (r[j], 0)),
                  cst((CONV_W, qkv_w)), cst(misc.shape),
                  s_spec(2), cv_spec(2)],
        out_specs=[pl.BlockSpec((rows, z_w), lambda j, r, *_: (r[j], 0)),
                   s_spec(4), cv_spec(4), s_spec(5), cv_spec(5)],
        scratch_shapes=[pltpu.VMEM(sh_s, F32),
                        pltpu.VMEM((LANES, 8 + c, qkv_w), F32),
                        pltpu.VMEM(sh_s, F32),
                        pltpu.VMEM((LANES, nk, qkv_w), F32)],
    )
    st = lambda n: [jax.ShapeDtypeStruct((n, LANES, HC_V_HEADS, HC_DK, HC_DV), F32),
                    jax.ShapeDtypeStruct((n, LANES, nk, qkv_w), F32)]
    return pl.pallas_call(
        functools.partial(_skip_zero_blocks(_gdn_step, 6 + 5), n_sq=n_sq),
        grid_spec=grid_spec,
        out_shape=[jax.ShapeDtypeStruct((p.shape[0], z_w), BF16)] + st(n_pp) + st(n_sp),
        compiler_params=pltpu.CompilerParams(dimension_semantics=("arbitrary",),
                                             vmem_limit_bytes=VMEM_LIMIT),
        name="gdn_mixer",
    )(rb, fs, insl, emit, pidx, sidx, p, conv_w, misc, s_in, cv_in)


def _pad_cols(w, width):
    return jnp.pad(w, ((0, 0), (0, width - w.shape[1])))


def kernel(x_prompt, x_sample, state_hgrn_S, state_mlstm_C, state_mlstm_n, state_mlstm_m, state_gdn_S,
           state_gdn_conv, meta_tokens, norm_mix, norm_ffn, norm_final, even_w_in, even_w_out,
           hgrn_lb_logits, hgrn_norm, mlstm_b_i, mlstm_b_f, mlstm_norm, odd_w_in, odd_conv_w, gdn_a_log,
           gdn_dt_bias, gdn_norm, odd_w_out, ffn_w_in, ffn_w_out):
    n_b, t_len, d = x_prompt.shape
    n_s, l_s, _ = x_sample.shape
    c = CHUNK
    blk = LANES * c
    assert t_len % c == 0 and l_s % c == 0 and meta_tokens.shape[0] == N_META
    assert n_b % LANES == 0 and n_s % LANES == 0, "sequences are processed in pairs"
    assert norm_mix.shape[0] == 2, "one even (HGRN2+mLSTM) and one odd (DeltaNet) layer"
    dt = x_prompt.dtype
    rows_p, rows_s = n_b * t_len, n_s * l_s
    tm = min(DENSE_TM, int(np.gcd(rows_p, rows_s)))
    assert tm % blk == 0
    m_rows = rows_p + rows_s + tm

    meta_job = jnp.concatenate([jnp.zeros((c - N_META, d), dt), meta_tokens.astype(dt)], axis=0)
    x_meta = jnp.concatenate([meta_job, meta_job, jnp.zeros((tm - blk, d), dt)], axis=0)
    x = (x_prompt.reshape(n_b // LANES, LANES, t_len // c, c, d),
         x_sample.reshape(n_s // LANES, LANES, l_s // c, c, d), x_meta)
    tables = _step_tables(n_b // LANES, t_len // c, n_s // LANES, l_s // c, m_rows // blk)
    row = lambda v: v.reshape(1, -1).astype(F32)
    pair = lambda a: a.reshape((a.shape[0] // LANES, LANES) + a.shape[1:])

    e = 0
    n_main = 4 * HA_HEADS * HA_DK + 2 * HB_HEADS * HB_DQK + 2 * HB_HEADS * HB_DV
    w_even = _pad_cols(even_w_in[e].astype(BF16), n_main + LANE)
    hk, hv = HA_HEADS * HA_DK, HA_HEADS * HA_DV
    mq, mv = HB_HEADS * HB_DQK, HB_HEADS * HB_DV
    ident = lambda v: v
    segs_even = ((0, hk, _silu), (hk, 2 * hk, _sigmoid), (2 * hk, 2 * hk + hv, ident),
                 (2 * hk + hv, 2 * hk + 2 * hv, _silu),
                 (2 * hk + 2 * hv, n_main - mv, ident), (n_main - mv, n_main, _sigmoid),
                 (n_main, n_main + LANE, ident))
    p_even = _proj(x, row(norm_mix[0]), w_even, tm, segs_even, "proj_even")
    gbias = _pad_cols(jnp.concatenate([mlstm_b_i[e], mlstm_b_f[e]]).reshape(1, -1).astype(F32), LANE)
    np_, ns_ = n_b // LANES, n_s // LANES
    mrg0, *even_states = _even_mixer(
        p_even, tables, np_, ns_, hgrn_lb_logits.astype(F32), e, row(hgrn_norm[e]), row(mlstm_norm[e]), gbias,
        pair(state_hgrn_S[e].astype(F32)), pair(state_mlstm_C[e].astype(F32)),
        pair(state_mlstm_n[e].astype(F32)[:, :, None, :]), pair(state_mlstm_m[e].astype(F32)[:, None, :]))
    ffn_wi, ffn_w2 = ffn_w_in.astype(BF16), ffn_w_out.astype(BF16)
    x = _post(x, mrg0, (even_w_out.astype(BF16), e), row(norm_ffn[0]), (ffn_wi, 0), (ffn_w2, 0),
              row(norm_final), tm, "post_even")

    o = 0
    n_qkv = 2 * HC_QK_HEADS * HC_DK + HC_V_HEADS * HC_DV
    n_z = HC_V_HEADS * HC_DV
    w_odd = _pad_cols(odd_w_in[o].astype(BF16), n_qkv + n_z + LANE)
    segs_odd = ((0, n_qkv, ident), (n_qkv, n_qkv + n_z, _silu), (n_qkv + n_z, n_qkv + n_z + LANE, ident))
    p_odd = _proj(x, row(norm_mix[1]), w_odd, tm, segs_odd, "proj_odd")
    lane_pad = lambda v: jnp.pad(v.reshape(1, -1).astype(F32),
                                 ((0, 0), (HC_V_HEADS, LANE - 2 * HC_V_HEADS)))
    mrg1, *gdn_states = _gdn_mixer(
        p_odd, tables, np_, ns_, odd_conv_w[o].astype(F32), lane_pad(gdn_a_log[o]),
        lane_pad(gdn_dt_bias[o]), row(gdn_norm[o]), pair(state_gdn_S[o].astype(F32)),
        pair(state_gdn_conv[o].astype(F32)))
    post_odd = functools.partial(_post, x, mrg1, (odd_w_out.astype(BF16), o), row(norm_ffn[1]),
                                 (ffn_wi, 1), (ffn_w2, 1), row(norm_final), tm, final_norm=True)
    y_p = post_odd("post_odd_prompt", tile0=0, n_tiles=rows_p // tm, unpair_shape=(n_b // LANES, t_len))
    y_s = post_odd("post_odd_sample", tile0=rows_p // tm, n_tiles=rows_s // tm,
                   unpair_shape=(n_s // LANES, l_s))

    y_prompt = y_p.reshape(n_b, t_len, d)
    y_sample = y_s.reshape(n_s, l_s, d)
    unpair = lambda v: v.reshape((-1,) + v.shape[2:])[None].astype(dt)
    outs = [y_prompt, y_sample]
    for k in range(2):
        hs_o, mc_o, mn_o, mm_o = even_states[4 * k:4 * k + 4]
        gs_o, cv_o = gdn_states[2 * k:2 * k + 2]
        outs += [unpair(hs_o), unpair(mc_o), unpair(mn_o[:, :, :, 0, :]), unpair(mm_o[:, :, 0, :]),
                 unpair(gs_o), unpair(cv_o)]
    return tuple(outs)
```

```python
import functools

import numpy as np
import jax
import jax.numpy as jnp
from jax import lax
from jax.experimental import pallas as pl
from jax.experimental.pallas import tpu as pltpu

F32 = jnp.float32
BF16 = jnp.bfloat16

CHUNK = 64
LANES = 2
N_META = 16
NORM_EPS = 1e-6
GATE_CAP = 15.0
NEG_BIG = -1e30

HA_HEADS, HA_DK, HA_DV = 4, 128, 128
HB_HEADS, HB_DQK, HB_DV = 4, 64, 128
HC_QK_HEADS, HC_V_HEADS, HC_DK, HC_DV = 8, 16, 128, 128
CONV_W = 4
LANE = 128
DENSE_TM = 512
VMEM_LIMIT = 56 * 1024 * 1024


def _nn(a, b):
    return lax.dot_general(a, b, (((1,), (0,)), ((), ())), preferred_element_type=F32)


def _nt(a, b):
    return lax.dot_general(a, b, (((1,), (1,)), ((), ())), preferred_element_type=F32)


def _tn(a, b):
    return lax.dot_general(a, b, (((0,), (0,)), ((), ())), preferred_element_type=F32)


def _dot_pieces(dot, exact, x, n):
    e16 = exact.astype(BF16)
    acc = None
    for _ in range(n):
        piece = x.astype(BF16)
        part = dot(e16, piece)
        acc = part if acc is None else acc + part
        x = x - piece.astype(F32)
    return acc


def _sigmoid(x):
    return 1.0 / (1.0 + jnp.exp(-x))


def _silu(x):
    return x * _sigmoid(x)


def _softplus(x):
    return jnp.maximum(x, 0.0) + jnp.log1p(jnp.exp(-jnp.abs(x)))


def _log_sigmoid(x):
    return jnp.minimum(x, 0.0) - jnp.log1p(jnp.exp(-jnp.abs(x)))


def _rms(x, w):
    ms = jnp.mean(x * x, axis=-1, keepdims=True)
    return x * lax.rsqrt(ms + NORM_EPS) * w


def _pair_block(pairs, seq_len, tm):
    chunks = seq_len // CHUNK
    k_tile = tm // (LANES * CHUNK)
    if chunks % k_tile == 0:
        per_pair = chunks // k_tile
        return (1, LANES, k_tile, CHUNK), (lambda t: (t // per_pair, 0, t % per_pair, 0, 0)), k_tile
    assert k_tile % chunks == 0
    return (k_tile // chunks, LANES, chunks, CHUNK), (lambda t: (t, 0, 0, 0, 0)), chunks


def _tile_units(n_rows):
    return [((k * LANES + ln) * CHUNK, k, ln) for k in range(n_rows // (LANES * CHUNK)) for ln in range(LANES)]


def _load_x_tile(i, xp_ref, xs_ref, xm_ref, xt_ref, geom):
    n_pt, n_st, kc_p, kc_s = geom

    def fill(src, kc):
        for r0, k, ln in _tile_units(xt_ref.shape[0]):
            xt_ref[r0:r0 + CHUNK, :] = src[k // kc, ln, k % kc]

    @pl.when(i < n_pt)
    def _():
        fill(xp_ref, kc_p)

    @pl.when((i >= n_pt) & (i < n_pt + n_st))
    def _():
        fill(xs_ref, kc_s)

    @pl.when(i >= n_pt + n_st)
    def _():
        xt_ref[...] = xm_ref[...]


def _x_specs(xsrc, tm):
    xp5, xs5, xm = xsrc
    d = xp5.shape[-1]
    blk_p, idx_p, kc_p = _pair_block(xp5.shape[0], xp5.shape[2] * CHUNK, tm)
    blk_s, idx_s, kc_s = _pair_block(xs5.shape[0], xs5.shape[2] * CHUNK, tm)
    n_pt = xp5.shape[0] * xp5.shape[2] * LANES * CHUNK // tm
    n_st = xs5.shape[0] * xs5.shape[2] * LANES * CHUNK // tm
    specs = [pl.BlockSpec(blk_p + (d,), lambda i: idx_p(jnp.minimum(i, n_pt - 1))),
             pl.BlockSpec(blk_s + (d,), lambda i: idx_s(jnp.clip(i - n_pt, 0, n_st - 1))),
             pl.BlockSpec((tm, d), lambda i: (0, 0))]
    return specs, (n_pt, n_st, kc_p, kc_s), (n_pt + n_st + 1) * tm, d


def _proj_kernel(*refs, n_chunk, segs, geom):
    if geom is None:
        x_ref, nw_ref, w_ref, o_ref = refs
    else:
        xp_ref, xs_ref, xm_ref, nw_ref, w_ref, o_ref, x_ref = refs
        _load_x_tile(pl.program_id(0), xp_ref, xs_ref, xm_ref, x_ref, geom)
    h = _rms(x_ref[...], nw_ref[...]).astype(BF16)
    for s0, s1, fn in segs:
        for n0 in range(s0, s1, n_chunk):
            n1 = min(n0 + n_chunk, s1)
            o_ref[:, n0:n1] = fn(_nn(h, w_ref[:, n0:n1]))


def _post_kernel(*refs, hidden, h_chunk, final_norm, unpair, geom):
    if geom is None:
        x_ref, m_ref, wo_ref, nw_ref, wi_ref, w2_ref, nf_ref, o_ref, act_ref = refs
    else:
        xp_ref, xs_ref, xm_ref, m_ref, wo_ref, nw_ref, wi_ref, w2_ref, nf_ref, o_ref, act_ref, x_ref = refs
        _load_x_tile(pl.program_id(0), xp_ref, xs_ref, xm_ref, x_ref, geom)
    x1 = x_ref[...] + _nn(m_ref[...], wo_ref[...])
    h = _rms(x1, nw_ref[...]).astype(BF16)
    for c0 in range(0, hidden, h_chunk):
        c1 = min(c0 + h_chunk, hidden)
        gate = _nn(h, wi_ref[:, c0:c1])
        up = _nn(h, wi_ref[:, hidden + c0:hidden + c1])
        act_ref[:, c0:c1] = (_silu(gate) * up).astype(BF16)
    x2 = x1 + _nn(act_ref[...], w2_ref[...])
    if final_norm:
        x2 = _rms(x2, nf_ref[...])
    if unpair:
        for r0, k, ln in _tile_units(x2.shape[0]):
            o_ref[k // unpair, ln, k % unpair] = x2[r0:r0 + CHUNK]
    else:
        o_ref[...] = x2


def _const_spec(shape):
    nd = len(shape)
    return pl.BlockSpec(shape, lambda i, _n=nd: (0,) * _n)


def _proj(x, nw, w, tm, segs, name):
    n = w.shape[1]
    assert segs[0][0] == 0 and segs[-1][1] == n and all(a[1] == b[0] for a, b in zip(segs, segs[1:]))
    if isinstance(x, tuple):
        x_specs, geom, m_rows, d = _x_specs(x, tm)
        x_args, scratch = list(x), [pltpu.VMEM((tm, d), F32)]
    else:
        m_rows, d = x.shape
        x_specs, geom, x_args, scratch = [pl.BlockSpec((tm, d), lambda i: (i, 0))], None, [x], []
    return pl.pallas_call(
        functools.partial(_proj_kernel, n_chunk=512, segs=segs, geom=geom),
        grid=(m_rows // tm,),
        in_specs=x_specs + [_const_spec((1, d)), _const_spec((d, n))],
        out_specs=pl.BlockSpec((tm, n), lambda i: (i, 0)),
        out_shape=jax.ShapeDtypeStruct((m_rows, n), F32),
        scratch_shapes=scratch,
        compiler_params=pltpu.CompilerParams(dimension_semantics=("arbitrary",),
                                             vmem_limit_bytes=VMEM_LIMIT),
        name=name,
    )(*x_args, nw, w)


def _post(x, mrg, wo, nw, wi, w2, nf, tm, name, final_norm=False, tile0=0, n_tiles=None, unpair_shape=None):
    (wo, l_o), (wi, l_i), (w2, l_2) = wo, wi, w2
    km = mrg.shape[1]
    hidden = w2.shape[1]
    layer_spec = lambda w, l: pl.BlockSpec((None,) + w.shape[1:], lambda i: (l, 0, 0))
    if isinstance(x, tuple):
        assert tile0 == 0 and n_tiles is None
        x_specs, geom, m_rows, d = _x_specs(x, tm)
        x_args, x_scratch = list(x), [pltpu.VMEM((tm, d), F32)]
    else:
        m_rows, d = x.shape
        x_specs, geom, x_args, x_scratch = [pl.BlockSpec((tm, d), lambda i: (tile0 + i, 0))], None, [x], []
    n_tiles = m_rows // tm if n_tiles is None else n_tiles
    unpair = 0
    if unpair_shape is None:
        out_spec = pl.BlockSpec((tm, d), lambda i: (i, 0))
        out_shape = jax.ShapeDtypeStruct((n_tiles * tm, d), F32)
    else:
        pairs, seq_len = unpair_shape
        assert pairs * seq_len * LANES == n_tiles * tm
        blk, idx, unpair = _pair_block(pairs, seq_len, tm)
        out_spec = pl.BlockSpec(blk + (d,), lambda i: idx(i))
        out_shape = jax.ShapeDtypeStruct((pairs, LANES, seq_len // CHUNK, CHUNK, d), F32)
    return pl.pallas_call(
        functools.partial(_post_kernel, hidden=hidden, h_chunk=256, final_norm=final_norm,
                          unpair=unpair, geom=geom),
        grid=(n_tiles,),
        in_specs=x_specs + [pl.BlockSpec((tm, km), lambda i: (tile0 + i, 0)),
                            layer_spec(wo, l_o),
                            _const_spec((1, d)),
                            layer_spec(wi, l_i),
                            layer_spec(w2, l_2),
                            _const_spec((1, d))],
        out_specs=out_spec,
        out_shape=out_shape,
        scratch_shapes=[pltpu.VMEM((tm, hidden), BF16)] + x_scratch,
        compiler_params=pltpu.CompilerParams(dimension_semantics=("arbitrary",),
                                             vmem_limit_bytes=VMEM_LIMIT),
        name=name,
    )(*x_args, mrg, wo, nw, wi, w2, nf)


def _step_tables(prompt_pairs, chunks_prompt, sample_pairs, chunks_sample, n_blocks):
    n_prompt = prompt_pairs * chunks_prompt
    n_sample = sample_pairs * chunks_sample
    rb, fs, insl, emit, pidx, sidx = [n_prompt + n_sample], [1], [0], [0], [0], [0]
    for b in range(prompt_pairs):
        for c in range(chunks_prompt):
            rb.append(b * chunks_prompt + c); fs.append(2 if c == 0 else 0); insl.append(0)
            emit.append(1 if c == chunks_prompt - 1 else 0); pidx.append(b); sidx.append(0)
    for q in range(sample_pairs):
        for c in range(chunks_sample):
            rb.append(n_prompt + q * chunks_sample + c); fs.append(3 if c == 0 else 0); insl.append(q)
            emit.append(2 if c == chunks_sample - 1 else 0); pidx.append(prompt_pairs - 1); sidx.append(q)
    for blk in range(n_prompt + n_sample + 1, n_blocks):
        rb.append(blk); fs.append(4); insl.append(0)
        emit.append(0); pidx.append(prompt_pairs - 1); sidx.append(sample_pairs - 1)
    to = lambda v: jnp.asarray(np.asarray(v, np.int32))
    return tuple(to(v) for v in (rb, fs, insl, emit, pidx, sidx))


def _hgrn_tables(c):
    t = np.arange(c)[:, None]
    j = np.arange(c)[None, :]
    mats, masks = [], []
    m = c // 2
    while m >= 1:
        blk = t // (2 * m)
        bd = blk * 2 * m + m - 1
        second = (t % (2 * m)) >= m
        mat = np.where(second, (j > bd) & (j <= t), (j > t) & (j <= bd)).astype(np.float32)
        mats.append(mat)
        masks.append((blk == (j // (2 * m))).astype(np.float32))
        m //= 2
    mats.append((j <= t).astype(np.float32))
    masks.append((j == t).astype(np.float32))
    return np.concatenate(mats, 0), np.concatenate(masks, 0), len(mats) - 1


def _even_step(rb_ref, fs_ref, insl_ref, emit_ref, pidx_ref, sidx_ref,
                 p_ref, crow_ref, mats_ref,
                 hs_in, mc_in, mn_in, mm_in,
                 mrg_ref, hs_p, mc_p, mn_p, mm_p, hs_s, mc_s, mn_s, mm_s,
                 hs_out, mc_out, mn_out, mm_out,
                 meta_hs, meta_mc, meta_mn, meta_mm,
                 *, lb_index, n_levels, n_lb):
    c = CHUNK
    wa, wqk, wv = 4 * HA_HEADS * HA_DK, HB_HEADS * HB_DQK, HB_HEADS * HB_DV
    pa_ref = p_ref.at[:, 0:wa]
    qb_ref = p_ref.at[:, wa:wa + wqk]
    kb_ref = p_ref.at[:, wa + wqk:wa + 2 * wqk]
    vb_ref = p_ref.at[:, wa + 2 * wqk:wa + 2 * wqk + wv]
    ob_ref = p_ref.at[:, wa + 2 * wqk + wv:wa + 2 * wqk + 2 * wv]
    gt_ref = p_ref.at[:, wa + 2 * wqk + 2 * wv:wa + 2 * wqk + 2 * wv + LANE]
    lbl_ref = crow_ref.at[0:n_lb, :]
    hn_ref = crow_ref.at[n_lb:n_lb + 1, :]
    mnw_ref = crow_ref.at[n_lb + 1:n_lb + 2, :]
    gbias_ref = crow_ref.at[n_lb + 2:n_lb + 3, 0:LANE]
    n_mat = (n_levels + 1) * c
    lvl_mat = lambda l: mats_ref[l * c:(l + 1) * c, 0:c]
    lmask = lambda l: mats_ref[n_mat + l * c:n_mat + (l + 1) * c, 0:c]
    tri = mats_ref[2 * n_mat:2 * n_mat + c, 0:c]
    eye = mats_ref[2 * n_mat + c:2 * n_mat + c + LANE, :]
    j = pl.program_id(0)
    fs = fs_ref[j]
    emit = emit_ref[j]

    @pl.when((fs == 1) | (fs == 4))
    def _():
        hs_out[...] = jnp.zeros_like(hs_out)
        mc_out[...] = jnp.zeros_like(mc_out)
        mn_out[...] = jnp.zeros_like(mn_out)
        mm_out[...] = jnp.zeros_like(mm_out)

    @pl.when(fs == 2)
    def _():
        hs_out[...] = meta_hs[...]
        mc_out[...] = meta_mc[...]
        mn_out[...] = meta_mn[...]
        mm_out[...] = meta_mm[...]

    @pl.when(fs == 3)
    def _():
        for ln in range(LANES):
            for h in range(HA_HEADS):
                hs_out[0, ln, h] = hs_in[0, ln, h].T
            mm_out[0, ln] = jnp.concatenate(
                [jnp.zeros((1, HB_HEADS), F32), mm_in[0, ln], jnp.zeros((1, LANE - 2 * HB_HEADS), F32)], axis=1)
        mc_out[...] = mc_in[...]
        mn_out[...] = mn_in[...]

    lanes = range(LANES)
    rows = [slice(ln * c, (ln + 1) * c) for ln in lanes]
    hk = HA_HEADS * HA_DK
    ua = [(ln, h) for ln in lanes for h in range(HA_HEADS)]
    ub = [(ln, h) for ln in lanes for h in range(HB_HEADS)]
    hsl = [slice(h * HA_DK, (h + 1) * HA_DK) for h in range(HA_HEADS)]

    lbl = lbl_ref[...]
    ex = jnp.exp(lbl - jnp.max(lbl, axis=0, keepdims=True))
    sm = ex / jnp.sum(ex, axis=0, keepdims=True)
    lb = jnp.sum(sm[:lb_index + 1], axis=0, keepdims=True)
    log_f, k_all, q_all, e_lvl, b_all = [], [], [], [], []
    for ln in lanes:
        sf = pa_ref[rows[ln], hk:2 * hk]
        log_f.append(jnp.log(lb + (1.0 - lb) * sf))
        k_all.append((1.0 - lb) * (1.0 - sf))
        q_all.append(pa_ref[rows[ln], 0:hk])
    for ln in lanes:
        b_all.append(_dot_pieces(_nn, lvl_mat(n_levels), log_f[ln], 2))
    for lvl in range(n_levels):
        e_lvl.append([_dot_pieces(_nn, lvl_mat(lvl), log_f[ln], 2) for ln in lanes])

    lane = lax.broadcasted_iota(jnp.int32, (c, LANE), 1)
    rowg = lax.broadcasted_iota(jnp.int32, (c, LANE), 0)
    is_f = (lane >= HB_HEADS) & (lane < 2 * HB_HEADS)
    pad = rowg < jnp.where(j == 0, c - N_META, 0)
    ig_t, fc, fr, ir = [], [], [], []
    for ln in lanes:
        gcap = GATE_CAP * jnp.tanh((gt_ref[rows[ln], :] + gbias_ref[...]) / GATE_CAP)
        lf_t = jnp.where(is_f & jnp.logical_not(pad), _log_sigmoid(gcap), 0.0)
        ig_t.append(jnp.where(pad, NEG_BIG, gcap))
        fc.append(_dot_pieces(_nn, tri, lf_t, 3))
    for ln in lanes:
        fr.append(_dot_pieces(_nt, eye, fc[ln], 3))
        ir.append(_dot_pieces(_nt, eye, ig_t[ln], 3))

    row = lax.broadcasted_iota(jnp.int32, (c, HA_DK), 0)
    q_h = [q_all[ln][:, hsl[h]] for ln, h in ua]
    k_h = [k_all[ln][:, hsl[h]] for ln, h in ua]
    v16 = [pa_ref[rows[ln], 2 * hk + h * HA_DV:2 * hk + (h + 1) * HA_DV].astype(BF16) for ln, h in ua]
    b_h = [b_all[ln][:, hsl[h]] for ln, h in ua]
    st = [hs_out[0, ln, h] for ln, h in ua]
    o_inter = [_nt((q_h[u] * jnp.exp(b_h[u])).astype(BF16), st[u].astype(BF16)) for u in range(len(ua))]
    for u, (ln, h) in enumerate(ua):
        b_last = b_h[u][c - 1:c, :]
        k_til = k_h[u] * jnp.exp(b_last - b_h[u])
        hs_out[0, ln, h] = st[u] * jnp.exp(b_last) + _tn(v16[u], k_til.astype(BF16))
    scores = [lmask(n_levels) * _nt(q_h[u].astype(BF16), k_h[u].astype(BF16)) for u in range(len(ua))]
    m = c // 2
    for lvl in range(n_levels):
        second = (row & m) != 0
        for u, (ln, h) in enumerate(ua):
            dec = jnp.exp(e_lvl[lvl][ln][:, hsl[h]])
            qs = jnp.where(second, q_h[u] * dec, 0.0).astype(BF16)
            ks = jnp.where(second, 0.0, k_h[u] * dec).astype(BF16)
            part = _nt(qs, ks)
            scores[u] = scores[u] + (part if lvl == 0 else lmask(lvl) * part)
        m //= 2

    r_i = lax.broadcasted_iota(jnp.int32, (c, c), 0)
    c_i = lax.broadcasted_iota(jnp.int32, (c, c), 1)
    causal = c_i <= r_i
    lane1 = lax.broadcasted_iota(jnp.int32, (1, LANE), 1)
    m_row = [mm_out[0, ln] for ln in lanes]
    m_row_new = list(m_row)
    mq = [qb_ref[rows[ln], h * HB_DQK:(h + 1) * HB_DQK] for ln, h in ub]
    mk = [kb_ref[rows[ln], h * HB_DQK:(h + 1) * HB_DQK] * (HB_DQK ** -0.5) for ln, h in ub]
    mv16 = [vb_ref[rows[ln], h * HB_DV:(h + 1) * HB_DV].astype(BF16) for ln, h in ub]
    ct = [mc_out[0, ln, h] for ln, h in ub]
    n_row = [mn_out[0, ln, h] for ln, h in ub]
    qk_raw = [_nt(mq[u].astype(BF16), mk[u].astype(BF16)) for u in range(len(ub))]
    inter = [_nn(mq[u].astype(BF16), ct[u].astype(BF16)) for u in range(len(ub))]
    qkd, m_ts, w_inters = [], [], []
    for u, (ln, h) in enumerate(ub):
        fl = HB_HEADS + h
        fc_h = fc[ln][:, fl:fl + 1]
        fr_h = fr[ln][fl:fl + 1, :]
        igr_h = ir[ln][h:h + 1, :]
        igc_h = ig_t[ln][:, h:h + 1]
        m_prev = m_row[ln][:, fl:fl + 1]
        log_d = jnp.where(causal, fc_h - fr_h + igr_h, -jnp.inf)
        log_inter = fc_h + m_prev
        m_t = jnp.maximum(log_inter, jnp.max(log_d, axis=1, keepdims=True))
        m_ts.append(m_t)
        w_inters.append(jnp.exp(log_inter - m_t))
        qkd.append(qk_raw[u] * jnp.exp(log_d - m_t))
        f_last = fc_h[c - 1:c, :]
        m_new = m_t[c - 1:c, :]
        w_s = jnp.exp(f_last - fc_h + igc_h - m_new)
        decay = jnp.exp(f_last + m_prev - m_new)
        kw = mk[u] * w_s
        mc_out[0, ln, h] = decay * ct[u] + _tn(kw.astype(BF16), mv16[u])
        mn_out[0, ln, h] = decay * n_row[u] + jnp.sum(kw, axis=0, keepdims=True)
        m_row_new[ln] = jnp.where(lane1 == fl, m_new, m_row_new[ln])
    for ln in lanes:
        mm_out[0, ln] = m_row_new[ln]

    o_a = [_nn(scores[u].astype(BF16), v16[u]) + o_inter[u] for u in range(len(ua))]
    num = [_nn(qkd[u].astype(BF16), mv16[u]) + w_inters[u] * inter[u] for u in range(len(ub))]
    for u, (ln, h) in enumerate(ua):
        ga = pa_ref[rows[ln], 3 * hk + h * HA_DV:3 * hk + (h + 1) * HA_DV]
        mrg_ref[rows[ln], h * HA_DV:(h + 1) * HA_DV] = (_rms(o_a[u], hn_ref[:, hsl[h]]) * ga).astype(BF16)
    for u, (ln, h) in enumerate(ub):
        den = jnp.sum(qkd[u], axis=1, keepdims=True) \
            + w_inters[u] * jnp.sum(mq[u] * n_row[u], axis=1, keepdims=True)
        hh = num[u] / jnp.maximum(jnp.abs(den), jnp.exp(-m_ts[u]))
        ob = ob_ref[rows[ln], h * HB_DV:(h + 1) * HB_DV]
        col0 = HA_HEADS * HA_DV + h * HB_DV
        mrg_ref[rows[ln], col0:col0 + HB_DV] = (
            _rms(hh, mnw_ref[:, h * HB_DV:(h + 1) * HB_DV]) * ob).astype(BF16)

    def write_states(hs_o, mc_o, mn_o, mm_o):
        for ln in range(LANES):
            for h in range(HA_HEADS):
                hs_o[0, ln, h] = hs_out[0, ln, h].T
            mm_o[0, ln] = mm_out[0, ln][:, HB_HEADS:2 * HB_HEADS]
        mc_o[...] = mc_out[...]
        mn_o[...] = mn_out[...]

    @pl.when(emit == 1)
    def _():
        write_states(hs_p, mc_p, mn_p, mm_p)

    @pl.when(emit == 2)
    def _():
        write_states(hs_s, mc_s, mn_s, mm_s)

    @pl.when(j == 0)
    def _():
        meta_hs[...] = hs_out[...]
        meta_mc[...] = mc_out[...]
        meta_mn[...] = mn_out[...]
        meta_mm[...] = mm_out[...]


def _skip_zero_blocks(step, n_in):
    def body(*refs, **kw):
        fs = refs[1][pl.program_id(0)]
        mrg_ref = refs[n_in]

        @pl.when(fs == 4)
        def _():
            mrg_ref[...] = jnp.zeros_like(mrg_ref)

        @pl.when(fs != 4)
        def _():
            step(*refs, **kw)
    return body


def _even_mixer(p, tables, n_pp, n_sp, lb_logits, lb_index, hgrn_norm, mlstm_norm, gbias,
                hs_in, mc_in, mn_in, mm_in):
    rb, fs, insl, emit, pidx, sidx = tables
    c = CHUNK
    rows = LANES * c
    n_steps = rb.shape[0]
    hk = HA_HEADS * HA_DK
    v_w = HB_HEADS * HB_DV
    n_lb = lb_logits.shape[0]
    crow = jnp.concatenate([lb_logits, hgrn_norm, mlstm_norm, _pad_cols(gbias, hk)], axis=0)
    crow = jnp.pad(crow, ((0, -crow.shape[0] % 8), (0, 0)))
    lvl, lmask, n_levels = _hgrn_tables(c)
    lane_pad = lambda m: np.pad(m, ((0, 0), (0, LANE - m.shape[1])))
    mats = jnp.asarray(np.concatenate([lane_pad(lvl), lane_pad(lmask), lane_pad(np.tril(np.ones((c, c), np.float32))),
                                       np.eye(LANE, dtype=np.float32)], axis=0))
    cst = lambda shape: pl.BlockSpec(shape, lambda j, *_: (0,) * len(shape))
    zeros = lambda shape: (0,) * len(shape)
    st_in = lambda shape: pl.BlockSpec((1,) + shape, lambda j, r, f, i, *_: (i[j],) + zeros(shape))
    st_p = lambda shape: pl.BlockSpec((1,) + shape, lambda j, r, f, i, e, pi, si: (pi[j],) + zeros(shape))
    st_s = lambda shape: pl.BlockSpec((1,) + shape, lambda j, r, f, i, e, pi, si: (si[j],) + zeros(shape))
    sh_hs = (LANES, HA_HEADS, HA_DK, HA_DV)
    sh_mc = (LANES, HB_HEADS, HB_DQK, HB_DV)
    sh_mn = (LANES, HB_HEADS, 1, HB_DQK)
    sh_mm = (LANES, 1, HB_HEADS)
    io_shapes = (sh_hs, sh_mc, sh_mn, sh_mm)
    carry_shapes = ((LANES, HA_HEADS, HA_DV, HA_DK), sh_mc, sh_mn, (LANES, 1, LANE))
    grid_spec = pltpu.PrefetchScalarGridSpec(
        num_scalar_prefetch=6,
        grid=(n_steps,),
        in_specs=[pl.BlockSpec((rows, p.shape[1]), lambda j, r, *_: (r[j], 0)), cst(crow.shape), cst(mats.shape)]
                 + [st_in(sh) for sh in io_shapes],
        out_specs=[pl.BlockSpec((rows, hk + v_w), lambda j, r, *_: (r[j], 0))]
                  + [st_p(sh) for sh in io_shapes] + [st_s(sh) for sh in io_shapes],
        scratch_shapes=[pltpu.VMEM((1,) + sh, F32) for sh in carry_shapes + carry_shapes],
    )
    out_shape = ([jax.ShapeDtypeStruct((p.shape[0], hk + v_w), BF16)]
                 + [jax.ShapeDtypeStruct((n_pp,) + sh, F32) for sh in io_shapes]
                 + [jax.ShapeDtypeStruct((n_sp,) + sh, F32) for sh in io_shapes])
    return pl.pallas_call(
        functools.partial(_even_step, lb_index=lb_index, n_levels=n_levels, n_lb=n_lb),
        grid_spec=grid_spec,
        out_shape=out_shape,
        compiler_params=pltpu.CompilerParams(dimension_semantics=("arbitrary",),
                                             vmem_limit_bytes=VMEM_LIMIT),
        name="even_mixer",
    )(rb, fs, insl, emit, pidx, sidx, p, crow, mats, hs_in, mc_in, mn_in, mm_in)


def _gdn_step(rb_ref, fs_ref, insl_ref, emit_ref, pidx_ref, sidx_ref,
                p_ref, cw_ref, misc_ref, s_in, cv_in,
                mrg_ref, s_p, cv_p, s_s, cv_s,
                s_out, xpad, meta_s, meta_cv, *, n_sq):
    c = CHUNK
    n_qkv, n_z = 2 * HC_QK_HEADS * HC_DK + HC_V_HEADS * HC_DV, HC_V_HEADS * HC_DV
    qkv_ref = p_ref.at[:, 0:n_qkv]
    z_ref = p_ref.at[:, n_qkv:n_qkv + n_z]
    gt_ref = p_ref.at[:, n_qkv + n_z:n_qkv + n_z + LANE]
    alog_ref, dtb_ref, gnw_ref = misc_ref.at[0:1, :], misc_ref.at[1:2, :], misc_ref.at[2:3, :]
    tri = misc_ref[8:8 + c, 0:c]
    hps = HC_QK_HEADS
    j = pl.program_id(0)
    fs = fs_ref[j]
    emit = emit_ref[j]
    nk = CONV_W - 1
    base = 8 - nk

    @pl.when(fs == 1)
    def _():
        s_out[...] = jnp.zeros_like(s_out)
        xpad[:, base:8, :] = jnp.zeros((LANES, nk, xpad.shape[2]), F32)

    @pl.when(fs == 2)
    def _():
        s_out[...] = meta_s[...]
        xpad[:, base:8, :] = meta_cv[...]

    @pl.when(fs == 3)
    def _():
        s_out[...] = s_in[...]
        xpad[:, base:8, :] = cv_in[0]

    lanes = range(LANES)
    rows = [slice(ln * c, (ln + 1) * c) for ln in lanes]
    lane = lax.broadcasted_iota(jnp.int32, (c, LANE), 1)
    row = lax.broadcasted_iota(jnp.int32, (c, LANE), 0)
    first = lane < c
    s_idx = jnp.where(first, lane, lane - c)
    causal = s_idx <= row
    strict = s_idx < row
    diag = s_idx == row
    eye_c = diag.astype(F32)
    r2 = lax.broadcasted_iota(jnp.int32, (2 * c, LANE), 0)
    l2 = lax.broadcasted_iota(jnp.int32, (2 * c, LANE), 1)
    same_head = (r2 < c) == (l2 < c)
    nq = HC_QK_HEADS * HC_DK

    def col(tile, idx):
        return jnp.sum(jnp.where(lane == idx, tile, 0.0), axis=1, keepdims=True)

    def block_diag(m16):
        return jnp.where(same_head, jnp.concatenate([m16, m16], axis=0), jnp.zeros_like(m16[:1, :1]))

    act, beta_t, gc = [], [], []
    for ln in lanes:
        xpad[ln, 8:8 + c, :] = qkv_ref[rows[ln], :]
        xp = xpad[ln]
        y = cw_ref[nk:CONV_W, :] * xp[8:8 + c]
        for t in range(nk):
            y = y + cw_ref[t:t + 1, :] * pltpu.roll(xp, nk - t, 0)[8:8 + c]
        xpad[ln, base:8, :] = xp[base + c:8 + c]
        act.append(_silu(y))
        gt = gt_ref[rows[ln], :]
        beta_t.append(_sigmoid(gt))
        g_t = -jnp.exp(alog_ref[...]) * _softplus(gt + dtb_ref[...])
        gc.append(_dot_pieces(_nn, tri, g_t, 3))

    units = [(ln, p) for ln in lanes for p in range(hps)]
    nu = range(len(units))
    qn, kn, qk, kk = [], [], [], []
    for ln, p in units:
        q = act[ln][:, p * HC_DK:(p + 1) * HC_DK]
        k = act[ln][:, nq + p * HC_DK:nq + (p + 1) * HC_DK]
        qn.append(q * lax.rsqrt(jnp.sum(q * q, axis=-1, keepdims=True) + NORM_EPS) * (HC_DK ** -0.5))
        kn.append(k * lax.rsqrt(jnp.sum(k * k, axis=-1, keepdims=True) + NORM_EPS))
    for u in nu:
        kn16 = kn[u].astype(BF16)
        k2 = jnp.concatenate([kn16, kn16], axis=0)
        qk.append(_nt(qn[u].astype(BF16), k2))
        kk.append(_nt(kn16, k2))

    beta, gcol, decay, inv, pw = [], [], [], [], []
    for u, (ln, p) in enumerate(units):
        hv = 2 * p
        b_a, b_b = col(beta_t[ln], hv), col(beta_t[ln], hv + 1)
        g_a, g_b = col(gc[ln], HC_V_HEADS + hv), col(gc[ln], HC_V_HEADS + hv + 1)
        beta.append((b_a, b_b))
        gcol.append((g_a, g_b))
        g_c = jnp.where(first, g_a, g_b)
        g_r = jnp.sum(jnp.where(diag, g_c, 0.0), axis=0, keepdims=True)
        decay.append(jnp.exp(jnp.where(causal, g_c - g_r, -jnp.inf)))
        low = jnp.where(strict, jnp.where(first, b_a, b_b) * kk[u] * decay[u], 0.0)
        inv.append(eye_c - low)
        pw.append(low.astype(BF16))
    pw = [_nn(x, block_diag(x)).astype(BF16) for x in pw]
    for i in range(n_sq):
        if i < n_sq - 1:
            prod = [_nn(jnp.concatenate([x.astype(BF16), y], axis=0), block_diag(y)) for x, y in zip(inv, pw)]
            inv = [x + r[:c] for x, r in zip(inv, prod)]
            pw = [r[c:].astype(BF16) for r in prod]
        else:
            inv = [x + _nn(x.astype(BF16), block_diag(y)) for x, y in zip(inv, pw)]

    sol, eg = [], []
    for u, (ln, p) in enumerate(units):
        (b_a, b_b), (g_a, g_b) = beta[u], gcol[u]
        e_a, e_b = jnp.exp(g_a), jnp.exp(g_b)
        eg.append((e_a, e_b))
        v_a = act[ln][:, 2 * nq + 2 * p * HC_DV:2 * nq + (2 * p + 1) * HC_DV]
        v_b = act[ln][:, 2 * nq + (2 * p + 1) * HC_DV:2 * nq + (2 * p + 2) * HC_DV]
        rhs = jnp.concatenate([jnp.concatenate([v_a * b_a, kn[u] * (b_a * e_a)], axis=1),
                               jnp.concatenate([v_b * b_b, kn[u] * (b_b * e_b)], axis=1)],
                              axis=0).astype(BF16)
        sol.append((_nn(jnp.where(first, inv[u], 0.0).astype(BF16), rhs),
                    _nn(jnp.where(first, 0.0, inv[u]).astype(BF16), rhs)))
    s_old = [[s_out[0, ln, 2 * p + jj] for jj in range(2)] for ln, p in units]
    ws = []
    for u in nu:
        for jj in range(2):
            lhs = jnp.concatenate([sol[u][jj][:, HC_DV:], qn[u] * eg[u][jj]], axis=0)
            ws.append(_nn(lhs.astype(BF16), s_old[u][jj].astype(BF16)))
    for u, (ln, p) in enumerate(units):
        v_new = [sol[u][jj][:, :HC_DV] - ws[2 * u + jj][:c] for jj in range(2)]
        v_st = jnp.concatenate(v_new, axis=0).astype(BF16)
        attn = qk[u] * decay[u]
        o = (ws[2 * u][c:] + _nn(jnp.where(first, attn, 0.0).astype(BF16), v_st),
             ws[2 * u + 1][c:] + _nn(jnp.where(first, 0.0, attn).astype(BF16), v_st))
        for jj in range(2):
            n = 2 * p + jj
            g_h = gcol[u][jj]
            g_last = g_h[c - 1:c, :]
            k_til = kn[u] * jnp.exp(g_last - g_h)
            s_out[0, ln, n] = jnp.exp(g_last) * s_old[u][jj] + _tn(k_til.astype(BF16), v_new[jj].astype(BF16))
            z = z_ref[rows[ln], n * HC_DV:(n + 1) * HC_DV]
            mrg_ref[rows[ln], n * HC_DV:(n + 1) * HC_DV] = (_rms(o[jj], gnw_ref[...]) * z).astype(BF16)

    @pl.when(emit == 1)
    def _():
        s_p[...] = s_out[...]
        cv_p[0] = xpad[:, base:8, :]

    @pl.when(emit == 2)
    def _():
        s_s[...] = s_out[...]
        cv_s[0] = xpad[:, base:8, :]

    @pl.when(j == 0)
    def _():
        meta_s[...] = s_out[...]
        meta_cv[...] = xpad[:, base:8, :]


def _gdn_mixer(p, tables, n_pp, n_sp, conv_w, alog_row, dtb_row, gnw, s_in, cv_in):
    rb, fs, insl, emit, pidx, sidx = tables
    c = CHUNK
    rows = LANES * c
    n_steps = rb.shape[0]
    qkv_w = 2 * HC_QK_HEADS * HC_DK + HC_V_HEADS * HC_DV
    z_w = HC_V_HEADS * HC_DV
    n_sq = int(np.log2(c)) - 1
    nk = CONV_W - 1
    tri = np.pad(np.tril(np.ones((c, c), np.float32)), ((0, 0), (0, LANE - c)))
    misc = jnp.concatenate([alog_row, dtb_row, gnw, jnp.zeros((5, LANE), F32), jnp.asarray(tri)], axis=0)
    sh_s = (1, LANES, HC_V_HEADS, HC_DK, HC_DV)
    sh_cv = (1, LANES, nk, qkv_w)
    cst = lambda shape: pl.BlockSpec(shape, lambda j, *_: (0,) * len(shape))
    s_spec = lambda k: pl.BlockSpec(sh_s, lambda j, *t: (t[k][j], 0, 0, 0, 0))
    cv_spec = lambda k: pl.BlockSpec(sh_cv, lambda j, *t: (t[k][j], 0, 0, 0))
    grid_spec = pltpu.PrefetchScalarGridSpec(
        num_scalar_prefetch=6,
        grid=(n_steps,),
        in_specs=[pl.BlockSpec((rows, p.shape[1]), lambda j, r, *_: (r[j], 0)),
                  cst((CONV_W, qkv_w)), cst(misc.shape),
                  s_spec(2), cv_spec(2)],
        out_specs=[pl.BlockSpec((rows, z_w), lambda j, r, *_: (r[j], 0)),
                   s_spec(4), cv_spec(4), s_spec(5), cv_spec(5)],
        scratch_shapes=[pltpu.VMEM(sh_s, F32),
                        pltpu.VMEM((LANES, 8 + c, qkv_w), F32),
                        pltpu.VMEM(sh_s, F32),
                        pltpu.VMEM((LANES, nk, qkv_w), F32)],
    )
    st = lambda n: [jax.ShapeDtypeStruct((n, LANES, HC_V_HEADS, HC_DK, HC_DV), F32),
                    jax.ShapeDtypeStruct((n, LANES, nk, qkv_w), F32)]
    return pl.pallas_call(
        functools.partial(_skip_zero_blocks(_gdn_step, 6 + 5), n_sq=n_sq),
        grid_spec=grid_spec,
        out_shape=[jax.ShapeDtypeStruct((p.shape[0], z_w), BF16)] + st(n_pp) + st(n_sp),
        compiler_params=pltpu.CompilerParams(dimension_semantics=("arbitrary",),
                                             vmem_limit_bytes=VMEM_LIMIT),
        name="gdn_mixer",
    )(rb, fs, insl, emit, pidx, sidx, p, conv_w, misc, s_in, cv_in)


def _pad_cols(w, width):
    return jnp.pad(w, ((0, 0), (0, width - w.shape[1])))


def kernel(x_prompt, x_sample, state_hgrn_S, state_mlstm_C, state_mlstm_n, state_mlstm_m, state_gdn_S,
           state_gdn_conv, meta_tokens, norm_mix, norm_ffn, norm_final, even_w_in, even_w_out,
           hgrn_lb_logits, hgrn_norm, mlstm_b_i, mlstm_b_f, mlstm_norm, odd_w_in, odd_conv_w, gdn_a_log,
           gdn_dt_bias, gdn_norm, odd_w_out, ffn_w_in, ffn_w_out):
    n_b, t_len, d = x_prompt.shape
    n_s, l_s, _ = x_sample.shape
    c = CHUNK
    blk = LANES * c
    assert t_len % c == 0 and l_s % c == 0 and meta_tokens.shape[0] == N_META
    assert n_b % LANES == 0 and n_s % LANES == 0, "sequences are processed in pairs"
    assert norm_mix.shape[0] == 2, "one even (HGRN2+mLSTM) and one odd (DeltaNet) layer"
    dt = x_prompt.dtype
    rows_p, rows_s = n_b * t_len, n_s * l_s
    tm = min(DENSE_TM, int(np.gcd(rows_p, rows_s)))
    assert tm % blk == 0
    m_rows = rows_p + rows_s + tm

    meta_job = jnp.concatenate([jnp.zeros((c - N_META, d), dt), meta_tokens.astype(dt)], axis=0)
    x_meta = jnp.concatenate([meta_job, meta_job, jnp.zeros((tm - blk, d), dt)], axis=0)
    x = (x_prompt.reshape(n_b // LANES, LANES, t_len // c, c, d),
         x_sample.reshape(n_s // LANES, LANES, l_s // c, c, d), x_meta)
    tables = _step_tables(n_b // LANES, t_len // c, n_s // LANES, l_s // c, m_rows // blk)
    row = lambda v: v.reshape(1, -1).astype(F32)
    pair = lambda a: a.reshape((a.shape[0] // LANES, LANES) + a.shape[1:])

    e = 0
    n_main = 4 * HA_HEADS * HA_DK + 2 * HB_HEADS * HB_DQK + 2 * HB_HEADS * HB_DV
    w_even = _pad_cols(even_w_in[e].astype(BF16), n_main + LANE)
    hk, hv = HA_HEADS * HA_DK, HA_HEADS * HA_DV
    mq, mv = HB_HEADS * HB_DQK, HB_HEADS * HB_DV
    ident = lambda v: v
    segs_even = ((0, hk, _silu), (hk, 2 * hk, _sigmoid), (2 * hk, 2 * hk + hv, ident),
                 (2 * hk + hv, 2 * hk + 2 * hv, _silu),
                 (2 * hk + 2 * hv, n_main - mv, ident), (n_main - mv, n_main, _sigmoid),
                 (n_main, n_main + LANE, ident))
    p_even = _proj(x, row(norm_mix[0]), w_even, tm, segs_even, "proj_even")
    gbias = _pad_cols(jnp.concatenate([mlstm_b_i[e], mlstm_b_f[e]]).reshape(1, -1).astype(F32), LANE)
    np_, ns_ = n_b // LANES, n_s // LANES
    mrg0, *even_states = _even_mixer(
        p_even, tables, np_, ns_, hgrn_lb_logits.astype(F32), e, row(hgrn_norm[e]), row(mlstm_norm[e]), gbias,
        pair(state_hgrn_S[e].astype(F32)), pair(state_mlstm_C[e].astype(F32)),
        pair(state_mlstm_n[e].astype(F32)[:, :, None, :]), pair(state_mlstm_m[e].astype(F32)[:, None, :]))
    ffn_wi, ffn_w2 = ffn_w_in.astype(BF16), ffn_w_out.astype(BF16)
    x = _post(x, mrg0, (even_w_out.astype(BF16), e), row(norm_ffn[0]), (ffn_wi, 0), (ffn_w2, 0),
              row(norm_final), tm, "post_even")

    o = 0
    n_qkv = 2 * HC_QK_HEADS * HC_DK + HC_V_HEADS * HC_DV
    n_z = HC_V_HEADS * HC_DV
    w_odd = _pad_cols(odd_w_in[o].astype(BF16), n_qkv + n_z + LANE)
    segs_odd = ((0, n_qkv, ident), (n_qkv, n_qkv + n_z, _silu), (n_qkv + n_z, n_qkv + n_z + LANE, ident))
    p_odd = _proj(x, row(norm_mix[1]), w_odd, tm, segs_odd, "proj_odd")
    lane_pad = lambda v: jnp.pad(v.reshape(1, -1).astype(F32),
                                 ((0, 0), (HC_V_HEADS, LANE - 2 * HC_V_HEADS)))
    mrg1, *gdn_states = _gdn_mixer(
        p_odd, tables, np_, ns_, odd_conv_w[o].astype(F32), lane_pad(gdn_a_log[o]),
        lane_pad(gdn_dt_bias[o]), row(gdn_norm[o]), pair(state_gdn_S[o].astype(F32)),
        pair(state_gdn_conv[o].astype(F32)))
    post_odd = functools.partial(_post, x, mrg1, (odd_w_out.astype(BF16), o), row(norm_ffn[1]),
                                 (ffn_wi, 1), (ffn_w2, 1), row(norm_final), tm, final_norm=True)
    y_p = post_odd("post_odd_prompt", tile0=0, n_tiles=rows_p // tm, unpair_shape=(n_b // LANES, t_len))
    y_s = post_odd("post_odd_sample", tile0=rows_p // tm, n_tiles=rows_s // tm,
                   unpair_shape=(n_s // LANES, l_s))

    y_prompt = y_p.reshape(n_b, t_len, d)
    y_sample = y_s.reshape(n_s, l_s, d)
    unpair = lambda v: v.reshape((-1,) + v.shape[2:])[None].astype(dt)
    outs = [y_prompt, y_sample]
    for k in range(2):
        hs_o, mc_o, mn_o, mm_o = even_states[4 * k:4 * k + 4]
        gs_o, cv_o = gdn_states[2 * k:2 * k + 2]
        outs += [unpair(hs_o), unpair(mc_o), unpair(mn_o[:, :, :, 0, :]), unpair(mm_o[:, :, 0, :]),
                 unpair(gs_o), unpair(cv_o)]
    return tuple(outs)
```

```python
import functools

import numpy as np
import jax
import jax.numpy as jnp
from jax import lax
from jax.experimental import pallas as pl
from jax.experimental.pallas import tpu as pltpu

F32 = jnp.float32
BF16 = jnp.bfloat16

CHUNK = 64
LANES = 2
N_META = 16
NORM_EPS = 1e-6
GATE_CAP = 15.0
NEG_BIG = -1e30

HA_HEADS, HA_DK, HA_DV = 4, 128, 128
HB_HEADS, HB_DQK, HB_DV = 4, 64, 128
HC_QK_HEADS, HC_V_HEADS, HC_DK, HC_DV = 8, 16, 128, 128
CONV_W = 4
LANE = 128
DENSE_TM = 512
VMEM_LIMIT = 56 * 1024 * 1024


def _nn(a, b):
    return lax.dot_general(a, b, (((1,), (0,)), ((), ())), preferred_element_type=F32)


def _nt(a, b):
    return lax.dot_general(a, b, (((1,), (1,)), ((), ())), preferred_element_type=F32)


def _tn(a, b):
    return lax.dot_general(a, b, (((0,), (0,)), ((), ())), preferred_element_type=F32)


def _dot_pieces(dot, exact, x, n):
    e16 = exact.astype(BF16)
    acc = None
    for _ in range(n):
        piece = x.astype(BF16)
        part = dot(e16, piece)
        acc = part if acc is None else acc + part
        x = x - piece.astype(F32)
    return acc


def _sigmoid(x):
    return 1.0 / (1.0 + jnp.exp(-x))


def _silu(x):
    return x * _sigmoid(x)


def _softplus(x):
    return jnp.maximum(x, 0.0) + jnp.log1p(jnp.exp(-jnp.abs(x)))


def _log_sigmoid(x):
    return jnp.minimum(x, 0.0) - jnp.log1p(jnp.exp(-jnp.abs(x)))


def _rms(x, w):
    ms = jnp.mean(x * x, axis=-1, keepdims=True)
    return x * lax.rsqrt(ms + NORM_EPS) * w


def _pair_block(pairs, seq_len, tm):
    chunks = seq_len // CHUNK
    k_tile = tm // (LANES * CHUNK)
    if chunks % k_tile == 0:
        per_pair = chunks // k_tile
        return (1, LANES, k_tile, CHUNK), (lambda t: (t // per_pair, 0, t % per_pair, 0, 0)), k_tile
    assert k_tile % chunks == 0
    return (k_tile // chunks, LANES, chunks, CHUNK), (lambda t: (t, 0, 0, 0, 0)), chunks


def _tile_units(n_rows):
    return [((k * LANES + ln) * CHUNK, k, ln) for k in range(n_rows // (LANES * CHUNK)) for ln in range(LANES)]


def _load_x_tile(i, xp_ref, xs_ref, xm_ref, xt_ref, geom):
    n_pt, n_st, kc_p, kc_s = geom

    def fill(src, kc):
        for r0, k, ln in _tile_units(xt_ref.shape[0]):
            xt_ref[r0:r0 + CHUNK, :] = src[k // kc, ln, k % kc]

    @pl.when(i < n_pt)
    def _():
        fill(xp_ref, kc_p)

    @pl.when((i >= n_pt) & (i < n_pt + n_st))
    def _():
        fill(xs_ref, kc_s)

    @pl.when(i >= n_pt + n_st)
    def _():
        xt_ref[...] = xm_ref[...]


def _x_specs(xsrc, tm):
    xp5, xs5, xm = xsrc
    d = xp5.shape[-1]
    blk_p, idx_p, kc_p = _pair_block(xp5.shape[0], xp5.shape[2] * CHUNK, tm)
    blk_s, idx_s, kc_s = _pair_block(xs5.shape[0], xs5.shape[2] * CHUNK, tm)
    n_pt = xp5.shape[0] * xp5.shape[2] * LANES * CHUNK // tm
    n_st = xs5.shape[0] * xs5.shape[2] * LANES * CHUNK // tm
    specs = [pl.BlockSpec(blk_p + (d,), lambda i: idx_p(jnp.minimum(i, n_pt - 1))),
             pl.BlockSpec(blk_s + (d,), lambda i: idx_s(jnp.clip(i - n_pt, 0, n_st - 1))),
             pl.BlockSpec((tm, d), lambda i: (0, 0))]
    return specs, (n_pt, n_st, kc_p, kc_s), (n_pt + n_st + 1) * tm, d


def _proj_kernel(*refs, n_chunk, segs, geom, cast_w):
    if cast_w:
        *refs, wb_ref = refs
    if geom is None:
        x_ref, nw_ref, w_ref, o_ref = refs
    else:
        xp_ref, xs_ref, xm_ref, nw_ref, w_ref, o_ref, x_ref = refs
        _load_x_tile(pl.program_id(0), xp_ref, xs_ref, xm_ref, x_ref, geom)
    if cast_w:
        w_in_ref, w_ref = w_ref, wb_ref

        @pl.when(pl.program_id(0) == 0)
        def _():
            n_w, n = w_in_ref.shape[1], w_ref.shape[1]
            for n0 in range(0, n, n_chunk):
                n1 = min(n0 + n_chunk, n)
                piece = w_in_ref[:, n0:min(n1, n_w)]
                if n1 > n_w:
                    piece = jnp.concatenate([piece, jnp.zeros((piece.shape[0], n1 - n_w), F32)], axis=1)
                w_ref[:, n0:n1] = piece.astype(BF16)
    h = _rms(x_ref[...], nw_ref[...]).astype(BF16)
    for s0, s1, fn in segs:
        for n0 in range(s0, s1, n_chunk):
            n1 = min(n0 + n_chunk, s1)
            o_ref[:, n0:n1] = fn(_nn(h, w_ref[:, n0:n1]))


def _post_kernel(*refs, hidden, h_chunk, final_norm, unpair, geom):
    if geom is None:
        x_ref, m_ref, wo_ref, nw_ref, wi_ref, w2_ref, nf_ref, o_ref, act_ref = refs
    else:
        xp_ref, xs_ref, xm_ref, m_ref, wo_ref, nw_ref, wi_ref, w2_ref, nf_ref, o_ref, act_ref, x_ref = refs
        _load_x_tile(pl.program_id(0), xp_ref, xs_ref, xm_ref, x_ref, geom)
    x1 = x_ref[...] + _nn(m_ref[...], wo_ref[...])
    h = _rms(x1, nw_ref[...]).astype(BF16)
    for c0 in range(0, hidden, h_chunk):
        c1 = min(c0 + h_chunk, hidden)
        gate = _nn(h, wi_ref[:, c0:c1])
        up = _nn(h, wi_ref[:, hidden + c0:hidden + c1])
        act_ref[:, c0:c1] = (_silu(gate) * up).astype(BF16)
    x2 = x1 + _nn(act_ref[...], w2_ref[...])
    if final_norm:
        x2 = _rms(x2, nf_ref[...])
    if unpair:
        for r0, k, ln in _tile_units(x2.shape[0]):
            o_ref[k // unpair, ln, k % unpair] = x2[r0:r0 + CHUNK]
    else:
        o_ref[...] = x2


def _const_spec(shape):
    nd = len(shape)
    return pl.BlockSpec(shape, lambda i, _n=nd: (0,) * _n)


def _proj(x, nw, w, tm, segs, name):
    if isinstance(x, tuple):
        x_specs, geom, m_rows, d = _x_specs(x, tm)
        x_args, scratch = list(x), [pltpu.VMEM((tm, d), F32)]
    else:
        m_rows, d = x.shape
        x_specs, geom, x_args, scratch = [pl.BlockSpec((tm, d), lambda i: (i, 0))], None, [x], []
    cast_w = isinstance(w, tuple)
    if cast_w:
        w, layer = w
        n = -(-w.shape[2] // LANE) * LANE
        w_spec = pl.BlockSpec((None, d, w.shape[2]), lambda i: (layer, 0, 0), pipeline_mode=pl.Buffered(1))
        scratch = scratch + [pltpu.VMEM((d, n), BF16)]
    else:
        n = w.shape[1]
        w_spec = _const_spec((d, n))
    assert segs[0][0] == 0 and segs[-1][1] == n and all(a[1] == b[0] for a, b in zip(segs, segs[1:]))
    return pl.pallas_call(
        functools.partial(_proj_kernel, n_chunk=512, segs=segs, geom=geom, cast_w=cast_w),
        grid=(m_rows // tm,),
        in_specs=x_specs + [_const_spec((1, d)), w_spec],
        out_specs=pl.BlockSpec((tm, n), lambda i: (i, 0)),
        out_shape=jax.ShapeDtypeStruct((m_rows, n), F32),
        scratch_shapes=scratch,
        compiler_params=pltpu.CompilerParams(dimension_semantics=("arbitrary",),
                                             vmem_limit_bytes=VMEM_LIMIT),
        name=name,
    )(*x_args, nw, w)


def _post(x, mrg, wo, nw, wi, w2, nf, tm, name, final_norm=False, tile0=0, n_tiles=None, unpair_shape=None):
    (wo, l_o), (wi, l_i), (w2, l_2) = wo, wi, w2
    km = mrg.shape[1]
    hidden = w2.shape[1]
    layer_spec = lambda w, l: pl.BlockSpec((None,) + w.shape[1:], lambda i: (l, 0, 0))
    if isinstance(x, tuple):
        assert tile0 == 0 and n_tiles is None
        x_specs, geom, m_rows, d = _x_specs(x, tm)
        x_args, x_scratch = list(x), [pltpu.VMEM((tm, d), F32)]
    else:
        m_rows, d = x.shape
        x_specs, geom, x_args, x_scratch = [pl.BlockSpec((tm, d), lambda i: (tile0 + i, 0))], None, [x], []
    n_tiles = m_rows // tm if n_tiles is None else n_tiles
    unpair = 0
    if unpair_shape is None:
        out_spec = pl.BlockSpec((tm, d), lambda i: (i, 0))
        out_shape = jax.ShapeDtypeStruct((n_tiles * tm, d), F32)
    else:
        pairs, seq_len = unpair_shape
        assert pairs * seq_len * LANES == n_tiles * tm
        blk, idx, unpair = _pair_block(pairs, seq_len, tm)
        out_spec = pl.BlockSpec(blk + (d,), lambda i: idx(i))
        out_shape = jax.ShapeDtypeStruct((pairs, LANES, seq_len // CHUNK, CHUNK, d), F32)
    return pl.pallas_call(
        functools.partial(_post_kernel, hidden=hidden, h_chunk=256, final_norm=final_norm,
                          unpair=unpair, geom=geom),
        grid=(n_tiles,),
        in_specs=x_specs + [pl.BlockSpec((tm, km), lambda i: (tile0 + i, 0)),
                            layer_spec(wo, l_o),
                            _const_spec((1, d)),
                            layer_spec(wi, l_i),
                            layer_spec(w2, l_2),
                            _const_spec((1, d))],
        out_specs=out_spec,
        out_shape=out_shape,
        scratch_shapes=[pltpu.VMEM((tm, hidden), BF16)] + x_scratch,
        compiler_params=pltpu.CompilerParams(dimension_semantics=("arbitrary",),
                                             vmem_limit_bytes=VMEM_LIMIT),
        name=name,
    )(*x_args, mrg, wo, nw, wi, w2, nf)


def _step_tables(prompt_pairs, chunks_prompt, sample_pairs, chunks_sample, n_blocks):
    n_prompt = prompt_pairs * chunks_prompt
    n_sample = sample_pairs * chunks_sample
    rb, fs, insl, emit, pidx, sidx = [n_prompt + n_sample], [1], [0], [0], [0], [0]
    for b in range(prompt_pairs):
        for c in range(chunks_prompt):
            rb.append(b * chunks_prompt + c); fs.append(2 if c == 0 else 0); insl.append(0)
            emit.append(1 if c == chunks_prompt - 1 else 0); pidx.append(b); sidx.append(0)
    for q in range(sample_pairs):
        for c in range(chunks_sample):
            rb.append(n_prompt + q * chunks_sample + c); fs.append(3 if c == 0 else 0); insl.append(q)
            emit.append(2 if c == chunks_sample - 1 else 0); pidx.append(prompt_pairs - 1); sidx.append(q)
    for blk in range(n_prompt + n_sample + 1, n_blocks):
        rb.append(blk); fs.append(4); insl.append(0)
        emit.append(0); pidx.append(prompt_pairs - 1); sidx.append(sample_pairs - 1)
    to = lambda v: jnp.asarray(np.asarray(v, np.int32))
    return tuple(to(v) for v in (rb, fs, insl, emit, pidx, sidx))


def _hgrn_tables(c):
    t = np.arange(c)[:, None]
    j = np.arange(c)[None, :]
    mats, masks = [], []
    m = c // 2
    while m >= 1:
        blk = t // (2 * m)
        bd = blk * 2 * m + m - 1
        second = (t % (2 * m)) >= m
        mat = np.where(second, (j > bd) & (j <= t), (j > t) & (j <= bd)).astype(np.float32)
        mats.append(mat)
        masks.append((blk == (j // (2 * m))).astype(np.float32))
        m //= 2
    mats.append((j <= t).astype(np.float32))
    masks.append((j == t).astype(np.float32))
    return np.concatenate(mats, 0), np.concatenate(masks, 0), len(mats) - 1


def _even_step(rb_ref, fs_ref, insl_ref, emit_ref, pidx_ref, sidx_ref,
                 p_ref, crow_ref, mats_ref,
                 hs_in, mc_in, mn_in, mm_in,
                 mrg_ref, hs_p, mc_p, mn_p, mm_p, hs_s, mc_s, mn_s, mm_s,
                 hs_out, mc_out, mn_out, mm_out,
                 meta_hs, meta_mc, meta_mn, meta_mm,
                 *, lb_index, n_levels, n_lb):
    c = CHUNK
    wa, wqk, wv = 4 * HA_HEADS * HA_DK, HB_HEADS * HB_DQK, HB_HEADS * HB_DV
    pa_ref = p_ref.at[:, 0:wa]
    qb_ref = p_ref.at[:, wa:wa + wqk]
    kb_ref = p_ref.at[:, wa + wqk:wa + 2 * wqk]
    vb_ref = p_ref.at[:, wa + 2 * wqk:wa + 2 * wqk + wv]
    ob_ref = p_ref.at[:, wa + 2 * wqk + wv:wa + 2 * wqk + 2 * wv]
    gt_ref = p_ref.at[:, wa + 2 * wqk + 2 * wv:wa + 2 * wqk + 2 * wv + LANE]
    lbl_ref = crow_ref.at[0:n_lb, :]
    hn_ref = crow_ref.at[n_lb:n_lb + 1, :]
    mnw_ref = crow_ref.at[n_lb + 1:n_lb + 2, :]
    gbias_ref = crow_ref.at[n_lb + 2:n_lb + 3, 0:LANE]
    n_mat = (n_levels + 1) * c
    lvl_mat = lambda l: mats_ref[l * c:(l + 1) * c, 0:c]
    lmask = lambda l: mats_ref[n_mat + l * c:n_mat + (l + 1) * c, 0:c]
    tri = mats_ref[2 * n_mat:2 * n_mat + c, 0:c]
    eye = mats_ref[2 * n_mat + c:2 * n_mat + c + LANE, :]
    j = pl.program_id(0)
    fs = fs_ref[j]
    emit = emit_ref[j]

    @pl.when((fs == 1) | (fs == 4))
    def _():
        hs_out[...] = jnp.zeros_like(hs_out)
        mc_out[...] = jnp.zeros_like(mc_out)
        mn_out[...] = jnp.zeros_like(mn_out)
        mm_out[...] = jnp.zeros_like(mm_out)

    @pl.when(fs == 2)
    def _():
        hs_out[...] = meta_hs[...]
        mc_out[...] = meta_mc[...]
        mn_out[...] = meta_mn[...]
        mm_out[...] = meta_mm[...]

    @pl.when(fs == 3)
    def _():
        for ln in range(LANES):
            for h in range(HA_HEADS):
                hs_out[0, ln, h] = hs_in[0, ln, h].T
            mm_out[0, ln] = jnp.concatenate(
                [jnp.zeros((1, HB_HEADS), F32), mm_in[0, ln], jnp.zeros((1, LANE - 2 * HB_HEADS), F32)], axis=1)
        mc_out[...] = mc_in[...]
        mn_out[...] = mn_in[...]

    lanes = range(LANES)
    rows = [slice(ln * c, (ln + 1) * c) for ln in lanes]
    hk = HA_HEADS * HA_DK
    ua = [(ln, h) for ln in lanes for h in range(HA_HEADS)]
    ub = [(ln, h) for ln in lanes for h in range(HB_HEADS)]
    hsl = [slice(h * HA_DK, (h + 1) * HA_DK) for h in range(HA_HEADS)]

    lbl = lbl_ref[...]
    ex = jnp.exp(lbl - jnp.max(lbl, axis=0, keepdims=True))
    sm = ex / jnp.sum(ex, axis=0, keepdims=True)
    lb = jnp.sum(sm[:lb_index + 1], axis=0, keepdims=True)
    log_f, k_all, q_all, e_lvl, b_all = [], [], [], [], []
    for ln in lanes:
        sf = pa_ref[rows[ln], hk:2 * hk]
        log_f.append(jnp.log(lb + (1.0 - lb) * sf))
        k_all.append((1.0 - lb) * (1.0 - sf))
        q_all.append(pa_ref[rows[ln], 0:hk])
    for ln in lanes:
        b_all.append(_dot_pieces(_nn, lvl_mat(n_levels), log_f[ln], 2))
    for lvl in range(n_levels):
        e_lvl.append([_dot_pieces(_nn, lvl_mat(lvl), log_f[ln], 2) for ln in lanes])

    lane = lax.broadcasted_iota(jnp.int32, (c, LANE), 1)
    rowg = lax.broadcasted_iota(jnp.int32, (c, LANE), 0)
    is_f = (lane >= HB_HEADS) & (lane < 2 * HB_HEADS)
    pad = rowg < jnp.where(j == 0, c - N_META, 0)
    ig_t, fc, fr, ir = [], [], [], []
    for ln in lanes:
        gcap = GATE_CAP * jnp.tanh((gt_ref[rows[ln], :] + gbias_ref[...]) / GATE_CAP)
        lf_t = jnp.where(is_f & jnp.logical_not(pad), _log_sigmoid(gcap), 0.0)
        ig_t.append(jnp.where(pad, NEG_BIG, gcap))
        fc.append(_dot_pieces(_nn, tri, lf_t, 3))
    for ln in lanes:
        fr.append(_dot_pieces(_nt, eye, fc[ln], 3))
        ir.append(_dot_pieces(_nt, eye, ig_t[ln], 3))

    row = lax.broadcasted_iota(jnp.int32, (c, HA_DK), 0)
    q_h = [q_all[ln][:, hsl[h]] for ln, h in ua]
    k_h = [k_all[ln][:, hsl[h]] for ln, h in ua]
    v16 = [pa_ref[rows[ln], 2 * hk + h * HA_DV:2 * hk + (h + 1) * HA_DV].astype(BF16) for ln, h in ua]
    b_h = [b_all[ln][:, hsl[h]] for ln, h in ua]
    st = [hs_out[0, ln, h] for ln, h in ua]
    o_inter = [_nt((q_h[u] * jnp.exp(b_h[u])).astype(BF16), st[u].astype(BF16)) for u in range(len(ua))]
    for u, (ln, h) in enumerate(ua):
        b_last = b_h[u][c - 1:c, :]
        k_til = k_h[u] * jnp.exp(b_last - b_h[u])
        hs_out[0, ln, h] = st[u] * jnp.exp(b_last) + _tn(v16[u], k_til.astype(BF16))
    scores = [lmask(n_levels) * _nt(q_h[u].astype(BF16), k_h[u].astype(BF16)) for u in range(len(ua))]
    m = c // 2
    for lvl in range(n_levels):
        second = (row & m) != 0
        for u, (ln, h) in enumerate(ua):
            dec = jnp.exp(e_lvl[lvl][ln][:, hsl[h]])
            qs = jnp.where(second, q_h[u] * dec, 0.0).astype(BF16)
            ks = jnp.where(second, 0.0, k_h[u] * dec).astype(BF16)
            part = _nt(qs, ks)
            scores[u] = scores[u] + (part if lvl == 0 else lmask(lvl) * part)
        m //= 2

    r_i = lax.broadcasted_iota(jnp.int32, (c, c), 0)
    c_i = lax.broadcasted_iota(jnp.int32, (c, c), 1)
    causal = c_i <= r_i
    lane1 = lax.broadcasted_iota(jnp.int32, (1, LANE), 1)
    m_row = [mm_out[0, ln] for ln in lanes]
    m_row_new = list(m_row)
    mq = [qb_ref[rows[ln], h * HB_DQK:(h + 1) * HB_DQK] for ln, h in ub]
    mk = [kb_ref[rows[ln], h * HB_DQK:(h + 1) * HB_DQK] * (HB_DQK ** -0.5) for ln, h in ub]
    mv16 = [vb_ref[rows[ln], h * HB_DV:(h + 1) * HB_DV].astype(BF16) for ln, h in ub]
    ct = [mc_out[0, ln, h] for ln, h in ub]
    n_row = [mn_out[0, ln, h] for ln, h in ub]
    qk_raw = [_nt(mq[u].astype(BF16), mk[u].astype(BF16)) for u in range(len(ub))]
    inter = [_nn(mq[u].astype(BF16), ct[u].astype(BF16)) for u in range(len(ub))]
    qkd, m_ts, w_inters = [], [], []
    for u, (ln, h) in enumerate(ub):
        fl = HB_HEADS + h
        fc_h = fc[ln][:, fl:fl + 1]
        fr_h = fr[ln][fl:fl + 1, :]
        igr_h = ir[ln][h:h + 1, :]
        igc_h = ig_t[ln][:, h:h + 1]
        m_prev = m_row[ln][:, fl:fl + 1]
        log_d = jnp.where(causal, fc_h - fr_h + igr_h, -jnp.inf)
        log_inter = fc_h + m_prev
        m_t = jnp.maximum(log_inter, jnp.max(log_d, axis=1, keepdims=True))
        m_ts.append(m_t)
        w_inters.append(jnp.exp(log_inter - m_t))
        qkd.append(qk_raw[u] * jnp.exp(log_d - m_t))
        f_last = fc_h[c - 1:c, :]
        m_new = m_t[c - 1:c, :]
        w_s = jnp.exp(f_last - fc_h + igc_h - m_new)
        decay = jnp.exp(f_last + m_prev - m_new)
        kw = mk[u] * w_s
        mc_out[0, ln, h] = decay * ct[u] + _tn(kw.astype(BF16), mv16[u])
        mn_out[0, ln, h] = decay * n_row[u] + jnp.sum(kw, axis=0, keepdims=True)
        m_row_new[ln] = jnp.where(lane1 == fl, m_new, m_row_new[ln])
    for ln in lanes:
        mm_out[0, ln] = m_row_new[ln]

    o_a = [_nn(scores[u].astype(BF16), v16[u]) + o_inter[u] for u in range(len(ua))]
    num = [_nn(qkd[u].astype(BF16), mv16[u]) + w_inters[u] * inter[u] for u in range(len(ub))]
    for u, (ln, h) in enumerate(ua):
        ga = pa_ref[rows[ln], 3 * hk + h * HA_DV:3 * hk + (h + 1) * HA_DV]
        mrg_ref[rows[ln], h * HA_DV:(h + 1) * HA_DV] = (_rms(o_a[u], hn_ref[:, hsl[h]]) * ga).astype(BF16)
    for u, (ln, h) in enumerate(ub):
        den = jnp.sum(qkd[u], axis=1, keepdims=True) \
            + w_inters[u] * jnp.sum(mq[u] * n_row[u], axis=1, keepdims=True)
        hh = num[u] / jnp.maximum(jnp.abs(den), jnp.exp(-m_ts[u]))
        ob = ob_ref[rows[ln], h * HB_DV:(h + 1) * HB_DV]
        col0 = HA_HEADS * HA_DV + h * HB_DV
        mrg_ref[rows[ln], col0:col0 + HB_DV] = (
            _rms(hh, mnw_ref[:, h * HB_DV:(h + 1) * HB_DV]) * ob).astype(BF16)

    def write_states(hs_o, mc_o, mn_o, mm_o):
        for ln in range(LANES):
            for h in range(HA_HEADS):
                hs_o[0, ln, h] = hs_out[0, ln, h].T
            mm_o[0, ln] = mm_out[0, ln][:, HB_HEADS:2 * HB_HEADS]
        mc_o[...] = mc_out[...]
        mn_o[...] = mn_out[...]

    @pl.when(emit == 1)
    def _():
        write_states(hs_p, mc_p, mn_p, mm_p)

    @pl.when(emit == 2)
    def _():
        write_states(hs_s, mc_s, mn_s, mm_s)

    @pl.when(j == 0)
    def _():
        meta_hs[...] = hs_out[...]
        meta_mc[...] = mc_out[...]
        meta_mn[...] = mn_out[...]
        meta_mm[...] = mm_out[...]


def _skip_zero_blocks(step, n_in):
    def body(*refs, **kw):
        fs = refs[1][pl.program_id(0)]
        mrg_ref = refs[n_in]

        @pl.when(fs == 4)
        def _():
            mrg_ref[...] = jnp.zeros_like(mrg_ref)

        @pl.when(fs != 4)
        def _():
            step(*refs, **kw)
    return body


def _even_mixer(p, tables, n_pp, n_sp, lb_logits, lb_index, hgrn_norm, mlstm_norm, gbias,
                hs_in, mc_in, mn_in, mm_in):
    rb, fs, insl, emit, pidx, sidx = tables
    c = CHUNK
    rows = LANES * c
    n_steps = rb.shape[0]
    hk = HA_HEADS * HA_DK
    v_w = HB_HEADS * HB_DV
    n_lb = lb_logits.shape[0]
    crow = jnp.concatenate([lb_logits, hgrn_norm, mlstm_norm, _pad_cols(gbias, hk)], axis=0)
    crow = jnp.pad(crow, ((0, -crow.shape[0] % 8), (0, 0)))
    lvl, lmask, n_levels = _hgrn_tables(c)
    lane_pad = lambda m: np.pad(m, ((0, 0), (0, LANE - m.shape[1])))
    mats = jnp.asarray(np.concatenate([lane_pad(lvl), lane_pad(lmask), lane_pad(np.tril(np.ones((c, c), np.float32))),
                                       np.eye(LANE, dtype=np.float32)], axis=0))
    cst = lambda shape: pl.BlockSpec(shape, lambda j, *_: (0,) * len(shape))
    zeros = lambda shape: (0,) * len(shape)
    st_in = lambda shape: pl.BlockSpec((1,) + shape, lambda j, r, f, i, *_: (i[j],) + zeros(shape))
    st_p = lambda shape: pl.BlockSpec((1,) + shape, lambda j, r, f, i, e, pi, si: (pi[j],) + zeros(shape))
    st_s = lambda shape: pl.BlockSpec((1,) + shape, lambda j, r, f, i, e, pi, si: (si[j],) + zeros(shape))
    sh_hs = (LANES, HA_HEADS, HA_DK, HA_DV)
    sh_mc = (LANES, HB_HEADS, HB_DQK, HB_DV)
    sh_mn = (LANES, HB_HEADS, 1, HB_DQK)
    sh_mm = (LANES, 1, HB_HEADS)
    io_shapes = (sh_hs, sh_mc, sh_mn, sh_mm)
    carry_shapes = ((LANES, HA_HEADS, HA_DV, HA_DK), sh_mc, sh_mn, (LANES, 1, LANE))
    grid_spec = pltpu.PrefetchScalarGridSpec(
        num_scalar_prefetch=6,
        grid=(n_steps,),
        in_specs=[pl.BlockSpec((rows, p.shape[1]), lambda j, r, *_: (r[j], 0)), cst(crow.shape), cst(mats.shape)]
                 + [st_in(sh) for sh in io_shapes],
        out_specs=[pl.BlockSpec((rows, hk + v_w), lambda j, r, *_: (r[j], 0))]
                  + [st_p(sh) for sh in io_shapes] + [st_s(sh) for sh in io_shapes],
        scratch_shapes=[pltpu.VMEM((1,) + sh, F32) for sh in carry_shapes + carry_shapes],
    )
    out_shape = ([jax.ShapeDtypeStruct((p.shape[0], hk + v_w), BF16)]
                 + [jax.ShapeDtypeStruct((n_pp,) + sh, F32) for sh in io_shapes]
                 + [jax.ShapeDtypeStruct((n_sp,) + sh, F32) for sh in io_shapes])
    return pl.pallas_call(
        functools.partial(_even_step, lb_index=lb_index, n_levels=n_levels, n_lb=n_lb),
        grid_spec=grid_spec,
        out_shape=out_shape,
        compiler_params=pltpu.CompilerParams(dimension_semantics=("arbitrary",),
                                             vmem_limit_bytes=VMEM_LIMIT),
        name="even_mixer",
    )(rb, fs, insl, emit, pidx, sidx, p, crow, mats, hs_in, mc_in, mn_in, mm_in)


def _gdn_step(rb_ref, fs_ref, insl_ref, emit_ref, pidx_ref, sidx_ref,
                p_ref, cw_ref, misc_ref, s_in, cv_in,
                mrg_ref, s_p, cv_p, s_s, cv_s,
                s_out, xpad, meta_s, meta_cv, *, n_sq):
    c = CHUNK
    n_qkv, n_z = 2 * HC_QK_HEADS * HC_DK + HC_V_HEADS * HC_DV, HC_V_HEADS * HC_DV
    qkv_ref = p_ref.at[:, 0:n_qkv]
    z_ref = p_ref.at[:, n_qkv:n_qkv + n_z]
    gt_ref = p_ref.at[:, n_qkv + n_z:n_qkv + n_z + LANE]
    alog_ref, dtb_ref, gnw_ref = misc_ref.at[0:1, :], misc_ref.at[1:2, :], misc_ref.at[2:3, :]
    tri = misc_ref[8:8 + c, 0:c]
    hps = HC_QK_HEADS
    j = pl.program_id(0)
    fs = fs_ref[j]
    emit = emit_ref[j]
    nk = CONV_W - 1
    base = 8 - nk

    @pl.when(fs == 1)
    def _():
        s_out[...] = jnp.zeros_like(s_out)
        xpad[:, base:8, :] = jnp.zeros((LANES, nk, xpad.shape[2]), F32)

    @pl.when(fs == 2)
    def _():
        s_out[...] = meta_s[...]
        xpad[:, base:8, :] = meta_cv[...]

    @pl.when(fs == 3)
    def _():
        s_out[...] = s_in[...]
        xpad[:, base:8, :] = cv_in[0]

    lanes = range(LANES)
    rows = [slice(ln * c, (ln + 1) * c) for ln in lanes]
    lane = lax.broadcasted_iota(jnp.int32, (c, LANE), 1)
    row = lax.broadcasted_iota(jnp.int32, (c, LANE), 0)
    first = lane < c
    s_idx = jnp.where(first, lane, lane - c)
    causal = s_idx <= row
    strict = s_idx < row
    diag = s_idx == row
    eye_c = diag.astype(F32)
    r2 = lax.broadcasted_iota(jnp.int32, (2 * c, LANE), 0)
    l2 = lax.broadcasted_iota(jnp.int32, (2 * c, LANE), 1)
    same_head = (r2 < c) == (l2 < c)
    nq = HC_QK_HEADS * HC_DK

    def col(tile, idx):
        return jnp.sum(jnp.where(lane == idx, tile, 0.0), axis=1, keepdims=True)

    def block_diag(m16):
        return jnp.where(same_head, jnp.concatenate([m16, m16], axis=0), jnp.zeros_like(m16[:1, :1]))

    act, beta_t, gc = [], [], []
    for ln in lanes:
        xpad[ln, 8:8 + c, :] = qkv_ref[rows[ln], :]
        xp = xpad[ln]
        y = cw_ref[nk:CONV_W, :] * xp[8:8 + c]
        for t in range(nk):
            y = y + cw_ref[t:t + 1, :] * pltpu.roll(xp, nk - t, 0)[8:8 + c]
        xpad[ln, base:8, :] = xp[base + c:8 + c]
        act.append(_silu(y))
        gt = gt_ref[rows[ln], :]
        beta_t.append(_sigmoid(gt))
        g_t = -jnp.exp(alog_ref[...]) * _softplus(gt + dtb_ref[...])
        gc.append(_dot_pieces(_nn, tri, g_t, 3))

    units = [(ln, p) for ln in lanes for p in range(hps)]
    nu = range(len(units))
    qn, kn, qk, kk = [], [], [], []
    for ln, p in units:
        q = act[ln][:, p * HC_DK:(p + 1) * HC_DK]
        k = act[ln][:, nq + p * HC_DK:nq + (p + 1) * HC_DK]
        qn.append(q * lax.rsqrt(jnp.sum(q * q, axis=-1, keepdims=True) + NORM_EPS) * (HC_DK ** -0.5))
        kn.append(k * lax.rsqrt(jnp.sum(k * k, axis=-1, keepdims=True) + NORM_EPS))
    for u in nu:
        kn16 = kn[u].astype(BF16)
        k2 = jnp.concatenate([kn16, kn16], axis=0)
        qk.append(_nt(qn[u].astype(BF16), k2))
        kk.append(_nt(kn16, k2))

    beta, gcol, decay, inv, pw = [], [], [], [], []
    for u, (ln, p) in enumerate(units):
        hv = 2 * p
        b_a, b_b = col(beta_t[ln], hv), col(beta_t[ln], hv + 1)
        g_a, g_b = col(gc[ln], HC_V_HEADS + hv), col(gc[ln], HC_V_HEADS + hv + 1)
        beta.append((b_a, b_b))
        gcol.append((g_a, g_b))
        g_c = jnp.where(first, g_a, g_b)
        g_r = jnp.sum(jnp.where(diag, g_c, 0.0), axis=0, keepdims=True)
        decay.append(jnp.exp(jnp.where(causal, g_c - g_r, -jnp.inf)))
        low = jnp.where(strict, jnp.where(first, b_a, b_b) * kk[u] * decay[u], 0.0)
        inv.append(eye_c - low)
        pw.append(low.astype(BF16))
    pw = [_nn(x, block_diag(x)).astype(BF16) for x in pw]
    for i in range(n_sq):
        if i < n_sq - 1:
            prod = [_nn(jnp.concatenate([x.astype(BF16), y], axis=0), block_diag(y)) for x, y in zip(inv, pw)]
            inv = [x + r[:c] for x, r in zip(inv, prod)]
            pw = [r[c:].astype(BF16) for r in prod]
        else:
            inv = [x + _nn(x.astype(BF16), block_diag(y)) for x, y in zip(inv, pw)]

    sol, eg = [], []
    for u, (ln, p) in enumerate(units):
        (b_a, b_b), (g_a, g_b) = beta[u], gcol[u]
        e_a, e_b = jnp.exp(g_a), jnp.exp(g_b)
        eg.append((e_a, e_b))
        v_a = act[ln][:, 2 * nq + 2 * p * HC_DV:2 * nq + (2 * p + 1) * HC_DV]
        v_b = act[ln][:, 2 * nq + (2 * p + 1) * HC_DV:2 * nq + (2 * p + 2) * HC_DV]
        rhs = jnp.concatenate([jnp.concatenate([v_a * b_a, kn[u] * (b_a * e_a)], axis=1),
                               jnp.concatenate([v_b * b_b, kn[u] * (b_b * e_b)], axis=1)],
                              axis=0).astype(BF16)
        sol.append((_nn(jnp.where(first, inv[u], 0.0).astype(BF16), rhs),
                    _nn(jnp.where(first, 0.0, inv[u]).astype(BF16), rhs)))
    s_old = [[s_out[0, ln, 2 * p + jj] for jj in range(2)] for ln, p in units]
    ws = []
    for u in nu:
        for jj in range(2):
            lhs = jnp.concatenate([sol[u][jj][:, HC_DV:], qn[u] * eg[u][jj]], axis=0)
            ws.append(_nn(lhs.astype(BF16), s_old[u][jj].astype(BF16)))
    for u, (ln, p) in enumerate(units):
        v_new = [sol[u][jj][:, :HC_DV] - ws[2 * u + jj][:c] for jj in range(2)]
        v_st = jnp.concatenate(v_new, axis=0).astype(BF16)
        attn = qk[u] * decay[u]
        o = (ws[2 * u][c:] + _nn(jnp.where(first, attn, 0.0).astype(BF16), v_st),
             ws[2 * u + 1][c:] + _nn(jnp.where(first, 0.0, attn).astype(BF16), v_st))
        for jj in range(2):
            n = 2 * p + jj
            g_h = gcol[u][jj]
            g_last = g_h[c - 1:c, :]
            k_til = kn[u] * jnp.exp(g_last - g_h)
            s_out[0, ln, n] = jnp.exp(g_last) * s_old[u][jj] + _tn(k_til.astype(BF16), v_new[jj].astype(BF16))
            z = z_ref[rows[ln], n * HC_DV:(n + 1) * HC_DV]
            mrg_ref[rows[ln], n * HC_DV:(n + 1) * HC_DV] = (_rms(o[jj], gnw_ref[...]) * z).astype(BF16)

    @pl.when(emit == 1)
    def _():
        s_p[...] = s_out[...]
        cv_p[0] = xpad[:, base:8, :]

    @pl.when(emit == 2)
    def _():
        s_s[...] = s_out[...]
        cv_s[0] = xpad[:, base:8, :]

    @pl.when(j == 0)
    def _():
        meta_s[...] = s_out[...]
        meta_cv[...] = xpad[:, base:8, :]


def _gdn_mixer(p, tables, n_pp, n_sp, conv_w, alog_row, dtb_row, gnw, s_in, cv_in):
    rb, fs, insl, emit, pidx, sidx = tables
    c = CHUNK
    rows = LANES * c
    n_steps = rb.shape[0]
    qkv_w = 2 * HC_QK_HEADS * HC_DK + HC_V_HEADS * HC_DV
    z_w = HC_V_HEADS * HC_DV
    n_sq = int(np.log2(c)) - 1
    nk = CONV_W - 1
    tri = np.pad(np.tril(np.ones((c, c), np.float32)), ((0, 0), (0, LANE - c)))
    misc = jnp.concatenate([alog_row, dtb_row, gnw, jnp.zeros((5, LANE), F32), jnp.asarray(tri)], axis=0)
    sh_s = (1, LANES, HC_V_HEADS, HC_DK, HC_DV)
    sh_cv = (1, LANES, nk, qkv_w)
    cst = lambda shape: pl.BlockSpec(shape, lambda j, *_: (0,) * len(shape))
    s_spec = lambda k: pl.BlockSpec(sh_s, lambda j, *t: (t[k][j], 0, 0, 0, 0))
    cv_spec = lambda k: pl.BlockSpec(sh_cv, lambda j, *t: (t[k][j], 0, 0, 0))
    grid_spec = pltpu.PrefetchScalarGridSpec(
        num_scalar_prefetch=6,
        grid=(n_steps,),
        in_specs=[pl.BlockSpec((rows, p.shape[1]), lambda j, r, *_: (r[j], 0)),
                  cst((CONV_W, qkv_w)), cst(misc.shape),
                  s_spec(2), cv_spec(2)],
        out_specs=[pl.BlockSpec((rows, z_w), lambda j, r, *_: (r[j], 0)),
                   s_spec(4), cv_spec(4), s_spec(5), cv_spec(5)],
        scratch_shapes=[pltpu.VMEM(sh_s, F32),
                        pltpu.VMEM((LANES, 8 + c, qkv_w), F32),
                        pltpu.VMEM(sh_s, F32),
                        pltpu.VMEM((LANES, nk, qkv_w), F32)],
    )
    st = lambda n: [jax.ShapeDtypeStruct((n, LANES, HC_V_HEADS, HC_DK, HC_DV), F32),
                    jax.ShapeDtypeStruct((n, LANES, nk, qkv_w), F32)]
    return pl.pallas_call(
        functools.partial(_skip_zero_blocks(_gdn_step, 6 + 5), n_sq=n_sq),
        grid_spec=grid_spec,
        out_shape=[jax.ShapeDtypeStruct((p.shape[0], z_w), BF16)] + st(n_pp) + st(n_sp),
        compiler_params=pltpu.CompilerParams(dimension_semantics=("arbitrary",),
                                             vmem_limit_bytes=VMEM_LIMIT),
        name="gdn_mixer",
    )(rb, fs, insl, emit, pidx, sidx, p, conv_w, misc, s_in, cv_in)


def _pad_cols(w, width):
    return jnp.pad(w, ((0, 0), (0, width - w.shape[1])))


def kernel(x_prompt, x_sample, state_hgrn_S, state_mlstm_C, state_mlstm_n, state_mlstm_m, state_gdn_S,
           state_gdn_conv, meta_tokens, norm_mix, norm_ffn, norm_final, even_w_in, even_w_out,
           hgrn_lb_logits, hgrn_norm, mlstm_b_i, mlstm_b_f, mlstm_norm, odd_w_in, odd_conv_w, gdn_a_log,
           gdn_dt_bias, gdn_norm, odd_w_out, ffn_w_in, ffn_w_out):
    n_b, t_len, d = x_prompt.shape
    n_s, l_s, _ = x_sample.shape
    c = CHUNK
    blk = LANES * c
    assert t_len % c == 0 and l_s % c == 0 and meta_tokens.shape[0] == N_META
    assert n_b % LANES == 0 and n_s % LANES == 0, "sequences are processed in pairs"
    assert norm_mix.shape[0] == 2, "one even (HGRN2+mLSTM) and one odd (DeltaNet) layer"
    dt = x_prompt.dtype
    rows_p, rows_s = n_b * t_len, n_s * l_s
    tm = min(DENSE_TM, int(np.gcd(rows_p, rows_s)))
    assert tm % blk == 0
    m_rows = rows_p + rows_s + tm

    meta_job = jnp.concatenate([jnp.zeros((c - N_META, d), dt), meta_tokens.astype(dt)], axis=0)
    x_meta = jnp.concatenate([meta_job, meta_job, jnp.zeros((tm - blk, d), dt)], axis=0)
    x = (x_prompt.reshape(n_b // LANES, LANES, t_len // c, c, d),
         x_sample.reshape(n_s // LANES, LANES, l_s // c, c, d), x_meta)
    tables = _step_tables(n_b // LANES, t_len // c, n_s // LANES, l_s // c, m_rows // blk)
    row = lambda v: v.reshape(1, -1).astype(F32)
    pair = lambda a: a.reshape((a.shape[0] // LANES, LANES) + a.shape[1:])

    e = 0
    n_main = 4 * HA_HEADS * HA_DK + 2 * HB_HEADS * HB_DQK + 2 * HB_HEADS * HB_DV
    w_even = (even_w_in, e)
    hk, hv = HA_HEADS * HA_DK, HA_HEADS * HA_DV
    mq, mv = HB_HEADS * HB_DQK, HB_HEADS * HB_DV
    ident = lambda v: v
    segs_even = ((0, hk, _silu), (hk, 2 * hk, _sigmoid), (2 * hk, 2 * hk + hv, ident),
                 (2 * hk + hv, 2 * hk + 2 * hv, _silu),
                 (2 * hk + 2 * hv, n_main - mv, ident), (n_main - mv, n_main, _sigmoid),
                 (n_main, n_main + LANE, ident))
    p_even = _proj(x, row(norm_mix[0]), w_even, tm, segs_even, "proj_even")
    gbias = _pad_cols(jnp.concatenate([mlstm_b_i[e], mlstm_b_f[e]]).reshape(1, -1).astype(F32), LANE)
    np_, ns_ = n_b // LANES, n_s // LANES
    mrg0, *even_states = _even_mixer(
        p_even, tables, np_, ns_, hgrn_lb_logits.astype(F32), e, row(hgrn_norm[e]), row(mlstm_norm[e]), gbias,
        pair(state_hgrn_S[e].astype(F32)), pair(state_mlstm_C[e].astype(F32)),
        pair(state_mlstm_n[e].astype(F32)[:, :, None, :]), pair(state_mlstm_m[e].astype(F32)[:, None, :]))
    ffn_wi, ffn_w2 = ffn_w_in.astype(BF16), ffn_w_out.astype(BF16)
    x = _post(x, mrg0, (even_w_out.astype(BF16), e), row(norm_ffn[0]), (ffn_wi, 0), (ffn_w2, 0),
              row(norm_final), tm, "post_even")

    o = 0
    n_qkv = 2 * HC_QK_HEADS * HC_DK + HC_V_HEADS * HC_DV
    n_z = HC_V_HEADS * HC_DV
    w_odd = _pad_cols(odd_w_in[o].astype(BF16), n_qkv + n_z + LANE)
    segs_odd = ((0, n_qkv, ident), (n_qkv, n_qkv + n_z, _silu), (n_qkv + n_z, n_qkv + n_z + LANE, ident))
    p_odd = _proj(x, row(norm_mix[1]), w_odd, tm, segs_odd, "proj_odd")
    lane_pad = lambda v: jnp.pad(v.reshape(1, -1).astype(F32),
                                 ((0, 0), (HC_V_HEADS, LANE - 2 * HC_V_HEADS)))
    mrg1, *gdn_states = _gdn_mixer(
        p_odd, tables, np_, ns_, odd_conv_w[o].astype(F32), lane_pad(gdn_a_log[o]),
        lane_pad(gdn_dt_bias[o]), row(gdn_norm[o]), pair(state_gdn_S[o].astype(F32)),
        pair(state_gdn_conv[o].astype(F32)))
    post_odd = functools.partial(_post, x, mrg1, (odd_w_out.astype(BF16), o), row(norm_ffn[1]),
                                 (ffn_wi, 1), (ffn_w2, 1), row(norm_final), tm, final_norm=True)
    y_p = post_odd("post_odd_prompt", tile0=0, n_tiles=rows_p // tm, unpair_shape=(n_b // LANES, t_len))
    y_s = post_odd("post_odd_sample", tile0=rows_p // tm, n_tiles=rows_s // tm,
                   unpair_shape=(n_s // LANES, l_s))

    y_prompt = y_p.reshape(n_b, t_len, d)
    y_sample = y_s.reshape(n_s, l_s, d)
    unpair = lambda v: v.reshape((-1,) + v.shape[2:])[None].astype(dt)
    outs = [y_prompt, y_sample]
    for k in range(2):
        hs_o, mc_o, mn_o, mm_o = even_states[4 * k:4 * k + 4]
        gs_o, cv_o = gdn_states[2 * k:2 * k + 2]
        outs += [unpair(hs_o), unpair(mc_o), unpair(mn_o[:, :, :, 0, :]), unpair(mm_o[:, :, 0, :]),
                 unpair(gs_o), unpair(cv_o)]
    return tuple(outs)
```

```python
import functools

import numpy as np
import jax
import jax.numpy as jnp
from jax import lax
from jax.experimental import pallas as pl
from jax.experimental.pallas import tpu as pltpu

F32 = jnp.float32
BF16 = jnp.bfloat16

CHUNK = 64
LANES = 2
N_META = 16
NORM_EPS = 1e-6
GATE_CAP = 15.0
NEG_BIG = -1e30

HA_HEADS, HA_DK, HA_DV = 4, 128, 128
HB_HEADS, HB_DQK, HB_DV = 4, 64, 128
HC_QK_HEADS, HC_V_HEADS, HC_DK, HC_DV = 8, 16, 128, 128
CONV_W = 4
LANE = 128
DENSE_TM = 512
VMEM_LIMIT = 56 * 1024 * 1024


def _nn(a, b):
    return lax.dot_general(a, b, (((1,), (0,)), ((), ())), preferred_element_type=F32)


def _nt(a, b):
    return lax.dot_general(a, b, (((1,), (1,)), ((), ())), preferred_element_type=F32)


def _tn(a, b):
    return lax.dot_general(a, b, (((0,), (0,)), ((), ())), preferred_element_type=F32)


def _dot_pieces(dot, exact, x, n):
    e16 = exact.astype(BF16)
    acc = None
    for _ in range(n):
        piece = x.astype(BF16)
        part = dot(e16, piece)
        acc = part if acc is None else acc + part
        x = x - piece.astype(F32)
    return acc


def _sigmoid(x):
    return 1.0 / (1.0 + jnp.exp(-x))


def _silu(x):
    return x * _sigmoid(x)


def _softplus(x):
    return jnp.maximum(x, 0.0) + jnp.log1p(jnp.exp(-jnp.abs(x)))


def _log_sigmoid(x):
    return jnp.minimum(x, 0.0) - jnp.log1p(jnp.exp(-jnp.abs(x)))


def _rms(x, w):
    ms = jnp.mean(x * x, axis=-1, keepdims=True)
    return x * lax.rsqrt(ms + NORM_EPS) * w


def _pair_block(pairs, seq_len, tm):
    chunks = seq_len // CHUNK
    k_tile = tm // (LANES * CHUNK)
    if chunks % k_tile == 0:
        per_pair = chunks // k_tile
        return (1, LANES, k_tile, CHUNK), (lambda t: (t // per_pair, 0, t % per_pair, 0, 0)), k_tile
    assert k_tile % chunks == 0
    return (k_tile // chunks, LANES, chunks, CHUNK), (lambda t: (t, 0, 0, 0, 0)), chunks


def _tile_units(n_rows):
    return [((k * LANES + ln) * CHUNK, k, ln) for k in range(n_rows // (LANES * CHUNK)) for ln in range(LANES)]


def _load_x_tile(i, xp_ref, xs_ref, xm_ref, xt_ref, geom):
    n_pt, n_st, kc_p, kc_s = geom

    def fill(src, kc):
        for r0, k, ln in _tile_units(xt_ref.shape[0]):
            xt_ref[r0:r0 + CHUNK, :] = src[k // kc, ln, k % kc]

    @pl.when(i < n_pt)
    def _():
        fill(xp_ref, kc_p)

    @pl.when((i >= n_pt) & (i < n_pt + n_st))
    def _():
        fill(xs_ref, kc_s)

    @pl.when(i >= n_pt + n_st)
    def _():
        xt_ref[...] = xm_ref[...]


def _x_specs(xsrc, tm):
    xp5, xs5, xm = xsrc
    d = xp5.shape[-1]
    blk_p, idx_p, kc_p = _pair_block(xp5.shape[0], xp5.shape[2] * CHUNK, tm)
    blk_s, idx_s, kc_s = _pair_block(xs5.shape[0], xs5.shape[2] * CHUNK, tm)
    n_pt = xp5.shape[0] * xp5.shape[2] * LANES * CHUNK // tm
    n_st = xs5.shape[0] * xs5.shape[2] * LANES * CHUNK // tm
    specs = [pl.BlockSpec(blk_p + (d,), lambda i: idx_p(jnp.minimum(i, n_pt - 1))),
             pl.BlockSpec(blk_s + (d,), lambda i: idx_s(jnp.clip(i - n_pt, 0, n_st - 1))),
             pl.BlockSpec((tm, d), lambda i: (0, 0))]
    return specs, (n_pt, n_st, kc_p, kc_s), (n_pt + n_st + 1) * tm, d


def _proj_kernel(*refs, n_chunk, segs, geom):
    if geom is None:
        x_ref, nw_ref, w_ref, o_ref = refs
    else:
        xp_ref, xs_ref, xm_ref, nw_ref, w_ref, o_ref, x_ref = refs
        _load_x_tile(pl.program_id(0), xp_ref, xs_ref, xm_ref, x_ref, geom)
    h = _rms(x_ref[...], nw_ref[...]).astype(BF16)
    for s0, s1, fn in segs:
        for n0 in range(s0, s1, n_chunk):
            n1 = min(n0 + n_chunk, s1)
            o_ref[:, n0:n1] = fn(_nt(h, w_ref[n0:n1, :]))


def _post_kernel(*refs, hidden, h_chunk, final_norm, unpair, geom):
    if geom is None:
        x_ref, m_ref, wo_ref, nw_ref, wi_ref, w2_ref, nf_ref, o_ref, act_ref = refs
    else:
        xp_ref, xs_ref, xm_ref, m_ref, wo_ref, nw_ref, wi_ref, w2_ref, nf_ref, o_ref, act_ref, x_ref = refs
        _load_x_tile(pl.program_id(0), xp_ref, xs_ref, xm_ref, x_ref, geom)
    x1 = x_ref[...] + _nn(m_ref[...], wo_ref[...])
    h = _rms(x1, nw_ref[...]).astype(BF16)
    for c0 in range(0, hidden, h_chunk):
        c1 = min(c0 + h_chunk, hidden)
        gate = _nn(h, wi_ref[:, c0:c1])
        up = _nn(h, wi_ref[:, hidden + c0:hidden + c1])
        act_ref[:, c0:c1] = (_silu(gate) * up).astype(BF16)
    x2 = x1 + _nn(act_ref[...], w2_ref[...])
    if final_norm:
        x2 = _rms(x2, nf_ref[...])
    if unpair:
        for r0, k, ln in _tile_units(x2.shape[0]):
            o_ref[k // unpair, ln, k % unpair] = x2[r0:r0 + CHUNK]
    else:
        o_ref[...] = x2


def _const_spec(shape):
    nd = len(shape)
    return pl.BlockSpec(shape, lambda i, _n=nd: (0,) * _n)


def _proj(x, nw, w, tm, segs, name):
    n = w.shape[0]
    assert segs[0][0] == 0 and segs[-1][1] == n and all(a[1] == b[0] for a, b in zip(segs, segs[1:]))
    if isinstance(x, tuple):
        x_specs, geom, m_rows, d = _x_specs(x, tm)
        x_args, scratch = list(x), [pltpu.VMEM((tm, d), F32)]
    else:
        m_rows, d = x.shape
        x_specs, geom, x_args, scratch = [pl.BlockSpec((tm, d), lambda i: (i, 0))], None, [x], []
    return pl.pallas_call(
        functools.partial(_proj_kernel, n_chunk=512, segs=segs, geom=geom),
        grid=(m_rows // tm,),
        in_specs=x_specs + [_const_spec((1, d)), _const_spec((n, d))],
        out_specs=pl.BlockSpec((tm, n), lambda i: (i, 0)),
        out_shape=jax.ShapeDtypeStruct((m_rows, n), F32),
        scratch_shapes=scratch,
        compiler_params=pltpu.CompilerParams(dimension_semantics=("arbitrary",),
                                             vmem_limit_bytes=VMEM_LIMIT),
        name=name,
    )(*x_args, nw, w)


def _post(x, mrg, wo, nw, wi, w2, nf, tm, name, final_norm=False, tile0=0, n_tiles=None, unpair_shape=None):
    (wo, l_o), (wi, l_i), (w2, l_2) = wo, wi, w2
    km = mrg.shape[1]
    hidden = w2.shape[1]
    layer_spec = lambda w, l: pl.BlockSpec((None,) + w.shape[1:], lambda i: (l, 0, 0))
    if isinstance(x, tuple):
        assert tile0 == 0 and n_tiles is None
        x_specs, geom, m_rows, d = _x_specs(x, tm)
        x_args, x_scratch = list(x), [pltpu.VMEM((tm, d), F32)]
    else:
        m_rows, d = x.shape
        x_specs, geom, x_args, x_scratch = [pl.BlockSpec((tm, d), lambda i: (tile0 + i, 0))], None, [x], []
    n_tiles = m_rows // tm if n_tiles is None else n_tiles
    unpair = 0
    if unpair_shape is None:
        out_spec = pl.BlockSpec((tm, d), lambda i: (i, 0))
        out_shape = jax.ShapeDtypeStruct((n_tiles * tm, d), F32)
    else:
        pairs, seq_len = unpair_shape
        assert pairs * seq_len * LANES == n_tiles * tm
        blk, idx, unpair = _pair_block(pairs, seq_len, tm)
        out_spec = pl.BlockSpec(blk + (d,), lambda i: idx(i))
        out_shape = jax.ShapeDtypeStruct((pairs, LANES, seq_len // CHUNK, CHUNK, d), F32)
    return pl.pallas_call(
        functools.partial(_post_kernel, hidden=hidden, h_chunk=256, final_norm=final_norm,
                          unpair=unpair, geom=geom),
        grid=(n_tiles,),
        in_specs=x_specs + [pl.BlockSpec((tm, km), lambda i: (tile0 + i, 0)),
                            layer_spec(wo, l_o),
                            _const_spec((1, d)),
                            layer_spec(wi, l_i),
                            layer_spec(w2, l_2),
                            _const_spec((1, d))],
        out_specs=out_spec,
        out_shape=out_shape,
        scratch_shapes=[pltpu.VMEM((tm, hidden), BF16)] + x_scratch,
        compiler_params=pltpu.CompilerParams(dimension_semantics=("arbitrary",),
                                             vmem_limit_bytes=VMEM_LIMIT),
        name=name,
    )(*x_args, mrg, wo, nw, wi, w2, nf)


def _step_tables(prompt_pairs, chunks_prompt, sample_pairs, chunks_sample, n_blocks):
    n_prompt = prompt_pairs * chunks_prompt
    n_sample = sample_pairs * chunks_sample
    rb, fs, insl, emit, pidx, sidx = [n_prompt + n_sample], [1], [0], [0], [0], [0]
    for b in range(prompt_pairs):
        for c in range(chunks_prompt):
            rb.append(b * chunks_prompt + c); fs.append(2 if c == 0 else 0); insl.append(0)
            emit.append(1 if c == chunks_prompt - 1 else 0); pidx.append(b); sidx.append(0)
    for q in range(sample_pairs):
        for c in range(chunks_sample):
            rb.append(n_prompt + q * chunks_sample + c); fs.append(3 if c == 0 else 0); insl.append(q)
            emit.append(2 if c == chunks_sample - 1 else 0); pidx.append(prompt_pairs - 1); sidx.append(q)
    for blk in range(n_prompt + n_sample + 1, n_blocks):
        rb.append(blk); fs.append(4); insl.append(0)
        emit.append(0); pidx.append(prompt_pairs - 1); sidx.append(sample_pairs - 1)
    to = lambda v: jnp.asarray(np.asarray(v, np.int32))
    return tuple(to(v) for v in (rb, fs, insl, emit, pidx, sidx))


def _hgrn_tables(c):
    t = np.arange(c)[:, None]
    j = np.arange(c)[None, :]
    mats, masks = [], []
    m = c // 2
    while m >= 1:
        blk = t // (2 * m)
        bd = blk * 2 * m + m - 1
        second = (t % (2 * m)) >= m
        mat = np.where(second, (j > bd) & (j <= t), (j > t) & (j <= bd)).astype(np.float32)
        mats.append(mat)
        masks.append((blk == (j // (2 * m))).astype(np.float32))
        m //= 2
    mats.append((j <= t).astype(np.float32))
    masks.append((j == t).astype(np.float32))
    return np.concatenate(mats, 0), np.concatenate(masks, 0), len(mats) - 1


def _even_step(rb_ref, fs_ref, insl_ref, emit_ref, pidx_ref, sidx_ref,
                 p_ref, crow_ref, mats_ref,
                 hs_in, mc_in, mn_in, mm_in,
                 mrg_ref, hs_p, mc_p, mn_p, mm_p, hs_s, mc_s, mn_s, mm_s,
                 hs_out, mc_out, mn_out, mm_out,
                 meta_hs, meta_mc, meta_mn, meta_mm,
                 *, lb_index, n_levels, n_lb):
    c = CHUNK
    wa, wqk, wv = 4 * HA_HEADS * HA_DK, HB_HEADS * HB_DQK, HB_HEADS * HB_DV
    pa_ref = p_ref.at[:, 0:wa]
    qb_ref = p_ref.at[:, wa:wa + wqk]
    kb_ref = p_ref.at[:, wa + wqk:wa + 2 * wqk]
    vb_ref = p_ref.at[:, wa + 2 * wqk:wa + 2 * wqk + wv]
    ob_ref = p_ref.at[:, wa + 2 * wqk + wv:wa + 2 * wqk + 2 * wv]
    gt_ref = p_ref.at[:, wa + 2 * wqk + 2 * wv:wa + 2 * wqk + 2 * wv + LANE]
    lbl_ref = crow_ref.at[0:n_lb, :]
    hn_ref = crow_ref.at[n_lb:n_lb + 1, :]
    mnw_ref = crow_ref.at[n_lb + 1:n_lb + 2, :]
    gbias_ref = crow_ref.at[n_lb + 2:n_lb + 3, 0:LANE]
    n_mat = (n_levels + 1) * c
    lvl_mat = lambda l: mats_ref[l * c:(l + 1) * c, 0:c]
    lmask = lambda l: mats_ref[n_mat + l * c:n_mat + (l + 1) * c, 0:c]
    tri = mats_ref[2 * n_mat:2 * n_mat + c, 0:c]
    eye = mats_ref[2 * n_mat + c:2 * n_mat + c + LANE, :]
    j = pl.program_id(0)
    fs = fs_ref[j]
    emit = emit_ref[j]

    @pl.when((fs == 1) | (fs == 4))
    def _():
        hs_out[...] = jnp.zeros_like(hs_out)
        mc_out[...] = jnp.zeros_like(mc_out)
        mn_out[...] = jnp.zeros_like(mn_out)
        mm_out[...] = jnp.zeros_like(mm_out)

    @pl.when(fs == 2)
    def _():
        hs_out[...] = meta_hs[...]
        mc_out[...] = meta_mc[...]
        mn_out[...] = meta_mn[...]
        mm_out[...] = meta_mm[...]

    @pl.when(fs == 3)
    def _():
        for ln in range(LANES):
            for h in range(HA_HEADS):
                hs_out[0, ln, h] = hs_in[0, ln, h].T
            mm_out[0, ln] = jnp.concatenate(
                [jnp.zeros((1, HB_HEADS), F32), mm_in[0, ln], jnp.zeros((1, LANE - 2 * HB_HEADS), F32)], axis=1)
        mc_out[...] = mc_in[...]
        mn_out[...] = mn_in[...]

    lanes = range(LANES)
    rows = [slice(ln * c, (ln + 1) * c) for ln in lanes]
    hk = HA_HEADS * HA_DK
    ua = [(ln, h) for ln in lanes for h in range(HA_HEADS)]
    ub = [(ln, h) for ln in lanes for h in range(HB_HEADS)]
    hsl = [slice(h * HA_DK, (h + 1) * HA_DK) for h in range(HA_HEADS)]

    lbl = lbl_ref[...]
    ex = jnp.exp(lbl - jnp.max(lbl, axis=0, keepdims=True))
    sm = ex / jnp.sum(ex, axis=0, keepdims=True)
    lb = jnp.sum(sm[:lb_index + 1], axis=0, keepdims=True)
    log_f, k_all, q_all, e_lvl, b_all = [], [], [], [], []
    for ln in lanes:
        sf = pa_ref[rows[ln], hk:2 * hk]
        log_f.append(jnp.log(lb + (1.0 - lb) * sf))
        k_all.append((1.0 - lb) * (1.0 - sf))
        q_all.append(pa_ref[rows[ln], 0:hk])
    for ln in lanes:
        b_all.append(_dot_pieces(_nn, lvl_mat(n_levels), log_f[ln], 2))
    for lvl in range(n_levels):
        e_lvl.append([_dot_pieces(_nn, lvl_mat(lvl), log_f[ln], 2) for ln in lanes])

    lane = lax.broadcasted_iota(jnp.int32, (c, LANE), 1)
    rowg = lax.broadcasted_iota(jnp.int32, (c, LANE), 0)
    is_f = (lane >= HB_HEADS) & (lane < 2 * HB_HEADS)
    pad = rowg < jnp.where(j == 0, c - N_META, 0)
    ig_t, fc, fr, ir = [], [], [], []
    for ln in lanes:
        gcap = GATE_CAP * jnp.tanh((gt_ref[rows[ln], :] + gbias_ref[...]) / GATE_CAP)
        lf_t = jnp.where(is_f & jnp.logical_not(pad), _log_sigmoid(gcap), 0.0)
        ig_t.append(jnp.where(pad, NEG_BIG, gcap))
        fc.append(_dot_pieces(_nn, tri, lf_t, 3))
    for ln in lanes:
        fr.append(_dot_pieces(_nt, eye, fc[ln], 3))
        ir.append(_dot_pieces(_nt, eye, ig_t[ln], 3))

    row = lax.broadcasted_iota(jnp.int32, (c, HA_DK), 0)
    q_h = [q_all[ln][:, hsl[h]] for ln, h in ua]
    k_h = [k_all[ln][:, hsl[h]] for ln, h in ua]
    v16 = [pa_ref[rows[ln], 2 * hk + h * HA_DV:2 * hk + (h + 1) * HA_DV].astype(BF16) for ln, h in ua]
    b_h = [b_all[ln][:, hsl[h]] for ln, h in ua]
    st = [hs_out[0, ln, h] for ln, h in ua]
    o_inter = [_nt((q_h[u] * jnp.exp(b_h[u])).astype(BF16), st[u].astype(BF16)) for u in range(len(ua))]
    for u, (ln, h) in enumerate(ua):
        b_last = b_h[u][c - 1:c, :]
        k_til = k_h[u] * jnp.exp(b_last - b_h[u])
        hs_out[0, ln, h] = st[u] * jnp.exp(b_last) + _tn(v16[u], k_til.astype(BF16))
    scores = [lmask(n_levels) * _nt(q_h[u].astype(BF16), k_h[u].astype(BF16)) for u in range(len(ua))]
    m = c // 2
    for lvl in range(n_levels):
        second = (row & m) != 0
        for u, (ln, h) in enumerate(ua):
            dec = jnp.exp(e_lvl[lvl][ln][:, hsl[h]])
            qs = jnp.where(second, q_h[u] * dec, 0.0).astype(BF16)
            ks = jnp.where(second, 0.0, k_h[u] * dec).astype(BF16)
            part = _nt(qs, ks)
            scores[u] = scores[u] + (part if lvl == 0 else lmask(lvl) * part)
        m //= 2

    r_i = lax.broadcasted_iota(jnp.int32, (c, c), 0)
    c_i = lax.broadcasted_iota(jnp.int32, (c, c), 1)
    causal = c_i <= r_i
    lane1 = lax.broadcasted_iota(jnp.int32, (1, LANE), 1)
    m_row = [mm_out[0, ln] for ln in lanes]
    m_row_new = list(m_row)
    mq = [qb_ref[rows[ln], h * HB_DQK:(h + 1) * HB_DQK] for ln, h in ub]
    mk = [kb_ref[rows[ln], h * HB_DQK:(h + 1) * HB_DQK] * (HB_DQK ** -0.5) for ln, h in ub]
    mv16 = [vb_ref[rows[ln], h * HB_DV:(h + 1) * HB_DV].astype(BF16) for ln, h in ub]
    ct = [mc_out[0, ln, h] for ln, h in ub]
    n_row = [mn_out[0, ln, h] for ln, h in ub]
    qk_raw = [_nt(mq[u].astype(BF16), mk[u].astype(BF16)) for u in range(len(ub))]
    inter = [_nn(mq[u].astype(BF16), ct[u].astype(BF16)) for u in range(len(ub))]
    qkd, m_ts, w_inters = [], [], []
    for u, (ln, h) in enumerate(ub):
        fl = HB_HEADS + h
        fc_h = fc[ln][:, fl:fl + 1]
        fr_h = fr[ln][fl:fl + 1, :]
        igr_h = ir[ln][h:h + 1, :]
        igc_h = ig_t[ln][:, h:h + 1]
        m_prev = m_row[ln][:, fl:fl + 1]
        log_d = jnp.where(causal, fc_h - fr_h + igr_h, -jnp.inf)
        log_inter = fc_h + m_prev
        m_t = jnp.maximum(log_inter, jnp.max(log_d, axis=1, keepdims=True))
        m_ts.append(m_t)
        w_inters.append(jnp.exp(log_inter - m_t))
        qkd.append(qk_raw[u] * jnp.exp(log_d - m_t))
        f_last = fc_h[c - 1:c, :]
        m_new = m_t[c - 1:c, :]
        w_s = jnp.exp(f_last - fc_h + igc_h - m_new)
        decay = jnp.exp(f_last + m_prev - m_new)
        kw = mk[u] * w_s
        mc_out[0, ln, h] = decay * ct[u] + _tn(kw.astype(BF16), mv16[u])
        mn_out[0, ln, h] = decay * n_row[u] + jnp.sum(kw, axis=0, keepdims=True)
        m_row_new[ln] = jnp.where(lane1 == fl, m_new, m_row_new[ln])
    for ln in lanes:
        mm_out[0, ln] = m_row_new[ln]

    o_a = [_nn(scores[u].astype(BF16), v16[u]) + o_inter[u] for u in range(len(ua))]
    num = [_nn(qkd[u].astype(BF16), mv16[u]) + w_inters[u] * inter[u] for u in range(len(ub))]
    for u, (ln, h) in enumerate(ua):
        ga = pa_ref[rows[ln], 3 * hk + h * HA_DV:3 * hk + (h + 1) * HA_DV]
        mrg_ref[rows[ln], h * HA_DV:(h + 1) * HA_DV] = (_rms(o_a[u], hn_ref[:, hsl[h]]) * ga).astype(BF16)
    for u, (ln, h) in enumerate(ub):
        den = jnp.sum(qkd[u], axis=1, keepdims=True) \
            + w_inters[u] * jnp.sum(mq[u] * n_row[u], axis=1, keepdims=True)
        hh = num[u] / jnp.maximum(jnp.abs(den), jnp.exp(-m_ts[u]))
        ob = ob_ref[rows[ln], h * HB_DV:(h + 1) * HB_DV]
        col0 = HA_HEADS * HA_DV + h * HB_DV
        mrg_ref[rows[ln], col0:col0 + HB_DV] = (
            _rms(hh, mnw_ref[:, h * HB_DV:(h + 1) * HB_DV]) * ob).astype(BF16)

    def write_states(hs_o, mc_o, mn_o, mm_o):
        for ln in range(LANES):
            for h in range(HA_HEADS):
                hs_o[0, ln, h] = hs_out[0, ln, h].T
            mm_o[0, ln] = mm_out[0, ln][:, HB_HEADS:2 * HB_HEADS]
        mc_o[...] = mc_out[...]
        mn_o[...] = mn_out[...]

    @pl.when(emit == 1)
    def _():
        write_states(hs_p, mc_p, mn_p, mm_p)

    @pl.when(emit == 2)
    def _():
        write_states(hs_s, mc_s, mn_s, mm_s)

    @pl.when(j == 0)
    def _():
        meta_hs[...] = hs_out[...]
        meta_mc[...] = mc_out[...]
        meta_mn[...] = mn_out[...]
        meta_mm[...] = mm_out[...]


def _skip_zero_blocks(step, n_in):
    def body(*refs, **kw):
        fs = refs[1][pl.program_id(0)]
        mrg_ref = refs[n_in]

        @pl.when(fs == 4)
        def _():
            mrg_ref[...] = jnp.zeros_like(mrg_ref)

        @pl.when(fs != 4)
        def _():
            step(*refs, **kw)
    return body


def _even_mixer(p, tables, n_pp, n_sp, lb_logits, lb_index, hgrn_norm, mlstm_norm, gbias,
                hs_in, mc_in, mn_in, mm_in):
    rb, fs, insl, emit, pidx, sidx = tables
    c = CHUNK
    rows = LANES * c
    n_steps = rb.shape[0]
    hk = HA_HEADS * HA_DK
    v_w = HB_HEADS * HB_DV
    n_lb = lb_logits.shape[0]
    crow = jnp.concatenate([lb_logits, hgrn_norm, mlstm_norm, _pad_cols(gbias, hk)], axis=0)
    crow = jnp.pad(crow, ((0, -crow.shape[0] % 8), (0, 0)))
    lvl, lmask, n_levels = _hgrn_tables(c)
    lane_pad = lambda m: np.pad(m, ((0, 0), (0, LANE - m.shape[1])))
    mats = jnp.asarray(np.concatenate([lane_pad(lvl), lane_pad(lmask), lane_pad(np.tril(np.ones((c, c), np.float32))),
                                       np.eye(LANE, dtype=np.float32)], axis=0))
    cst = lambda shape: pl.BlockSpec(shape, lambda j, *_: (0,) * len(shape))
    zeros = lambda shape: (0,) * len(shape)
    st_in = lambda shape: pl.BlockSpec((1,) + shape, lambda j, r, f, i, *_: (i[j],) + zeros(shape))
    st_p = lambda shape: pl.BlockSpec((1,) + shape, lambda j, r, f, i, e, pi, si: (pi[j],) + zeros(shape))
    st_s = lambda shape: pl.BlockSpec((1,) + shape, lambda j, r, f, i, e, pi, si: (si[j],) + zeros(shape))
    sh_hs = (LANES, HA_HEADS, HA_DK, HA_DV)
    sh_mc = (LANES, HB_HEADS, HB_DQK, HB_DV)
    sh_mn = (LANES, HB_HEADS, 1, HB_DQK)
    sh_mm = (LANES, 1, HB_HEADS)
    io_shapes = (sh_hs, sh_mc, sh_mn, sh_mm)
    carry_shapes = ((LANES, HA_HEADS, HA_DV, HA_DK), sh_mc, sh_mn, (LANES, 1, LANE))
    grid_spec = pltpu.PrefetchScalarGridSpec(
        num_scalar_prefetch=6,
        grid=(n_steps,),
        in_specs=[pl.BlockSpec((rows, p.shape[1]), lambda j, r, *_: (r[j], 0)), cst(crow.shape), cst(mats.shape)]
                 + [st_in(sh) for sh in io_shapes],
        out_specs=[pl.BlockSpec((rows, hk + v_w), lambda j, r, *_: (r[j], 0))]
                  + [st_p(sh) for sh in io_shapes] + [st_s(sh) for sh in io_shapes],
        scratch_shapes=[pltpu.VMEM((1,) + sh, F32) for sh in carry_shapes + carry_shapes],
    )
    out_shape = ([jax.ShapeDtypeStruct((p.shape[0], hk + v_w), BF16)]
                 + [jax.ShapeDtypeStruct((n_pp,) + sh, F32) for sh in io_shapes]
                 + [jax.ShapeDtypeStruct((n_sp,) + sh, F32) for sh in io_shapes])
    return pl.pallas_call(
        functools.partial(_even_step, lb_index=lb_index, n_levels=n_levels, n_lb=n_lb),
        grid_spec=grid_spec,
        out_shape=out_shape,
        compiler_params=pltpu.CompilerParams(dimension_semantics=("arbitrary",),
                                             vmem_limit_bytes=VMEM_LIMIT),
        name="even_mixer",
    )(rb, fs, insl, emit, pidx, sidx, p, crow, mats, hs_in, mc_in, mn_in, mm_in)


def _gdn_step(rb_ref, fs_ref, insl_ref, emit_ref, pidx_ref, sidx_ref,
                p_ref, cw_ref, misc_ref, s_in, cv_in,
                mrg_ref, s_p, cv_p, s_s, cv_s,
                s_out, xpad, meta_s, meta_cv, *, n_sq):
    c = CHUNK
    n_qkv, n_z = 2 * HC_QK_HEADS * HC_DK + HC_V_HEADS * HC_DV, HC_V_HEADS * HC_DV
    qkv_ref = p_ref.at[:, 0:n_qkv]
    z_ref = p_ref.at[:, n_qkv:n_qkv + n_z]
    gt_ref = p_ref.at[:, n_qkv + n_z:n_qkv + n_z + LANE]
    alog_ref, dtb_ref, gnw_ref = misc_ref.at[0:1, :], misc_ref.at[1:2, :], misc_ref.at[2:3, :]
    tri = misc_ref[8:8 + c, 0:c]
    hps = HC_QK_HEADS
    j = pl.program_id(0)
    fs = fs_ref[j]
    emit = emit_ref[j]
    nk = CONV_W - 1
    base = 8 - nk

    @pl.when(fs == 1)
    def _():
        s_out[...] = jnp.zeros_like(s_out)
        xpad[:, base:8, :] = jnp.zeros((LANES, nk, xpad.shape[2]), F32)

    @pl.when(fs == 2)
    def _():
        s_out[...] = meta_s[...]
        xpad[:, base:8, :] = meta_cv[...]

    @pl.when(fs == 3)
    def _():
        s_out[...] = s_in[...]
        xpad[:, base:8, :] = cv_in[0]

    lanes = range(LANES)
    rows = [slice(ln * c, (ln + 1) * c) for ln in lanes]
    lane = lax.broadcasted_iota(jnp.int32, (c, LANE), 1)
    row = lax.broadcasted_iota(jnp.int32, (c, LANE), 0)
    first = lane < c
    s_idx = jnp.where(first, lane, lane - c)
    causal = s_idx <= row
    strict = s_idx < row
    diag = s_idx == row
    eye_c = diag.astype(F32)
    r2 = lax.broadcasted_iota(jnp.int32, (2 * c, LANE), 0)
    l2 = lax.broadcasted_iota(jnp.int32, (2 * c, LANE), 1)
    same_head = (r2 < c) == (l2 < c)
    nq = HC_QK_HEADS * HC_DK

    def col(tile, idx):
        return jnp.sum(jnp.where(lane == idx, tile, 0.0), axis=1, keepdims=True)

    def block_diag(m16):
        return jnp.where(same_head, jnp.concatenate([m16, m16], axis=0), jnp.zeros_like(m16[:1, :1]))

    act, beta_t, gc = [], [], []
    for ln in lanes:
        xpad[ln, 8:8 + c, :] = qkv_ref[rows[ln], :]
        xp = xpad[ln]
        y = cw_ref[nk:CONV_W, :] * xp[8:8 + c]
        for t in range(nk):
            y = y + cw_ref[t:t + 1, :] * pltpu.roll(xp, nk - t, 0)[8:8 + c]
        xpad[ln, base:8, :] = xp[base + c:8 + c]
        act.append(_silu(y))
        gt = gt_ref[rows[ln], :]
        beta_t.append(_sigmoid(gt))
        g_t = -jnp.exp(alog_ref[...]) * _softplus(gt + dtb_ref[...])
        gc.append(_dot_pieces(_nn, tri, g_t, 3))

    units = [(ln, p) for ln in lanes for p in range(hps)]
    nu = range(len(units))
    qn, kn, qk, kk = [], [], [], []
    for ln, p in units:
        q = act[ln][:, p * HC_DK:(p + 1) * HC_DK]
        k = act[ln][:, nq + p * HC_DK:nq + (p + 1) * HC_DK]
        qn.append(q * lax.rsqrt(jnp.sum(q * q, axis=-1, keepdims=True) + NORM_EPS) * (HC_DK ** -0.5))
        kn.append(k * lax.rsqrt(jnp.sum(k * k, axis=-1, keepdims=True) + NORM_EPS))
    for u in nu:
        kn16 = kn[u].astype(BF16)
        k2 = jnp.concatenate([kn16, kn16], axis=0)
        qk.append(_nt(qn[u].astype(BF16), k2))
        kk.append(_nt(kn16, k2))

    beta, gcol, decay, inv, pw = [], [], [], [], []
    for u, (ln, p) in enumerate(units):
        hv = 2 * p
        b_a, b_b = col(beta_t[ln], hv), col(beta_t[ln], hv + 1)
        g_a, g_b = col(gc[ln], HC_V_HEADS + hv), col(gc[ln], HC_V_HEADS + hv + 1)
        beta.append((b_a, b_b))
        gcol.append((g_a, g_b))
        g_c = jnp.where(first, g_a, g_b)
        g_r = jnp.sum(jnp.where(diag, g_c, 0.0), axis=0, keepdims=True)
        decay.append(jnp.exp(jnp.where(causal, g_c - g_r, -jnp.inf)))
        low = jnp.where(strict, jnp.where(first, b_a, b_b) * kk[u] * decay[u], 0.0)
        inv.append(eye_c - low)
        pw.append(low.astype(BF16))
    pw = [_nn(x, block_diag(x)).astype(BF16) for x in pw]
    for i in range(n_sq):
        if i < n_sq - 1:
            prod = [_nn(jnp.concatenate([x.astype(BF16), y], axis=0), block_diag(y)) for x, y in zip(inv, pw)]
            inv = [x + r[:c] for x, r in zip(inv, prod)]
            pw = [r[c:].astype(BF16) for r in prod]
        else:
            inv = [x + _nn(x.astype(BF16), block_diag(y)) for x, y in zip(inv, pw)]

    sol, eg = [], []
    for u, (ln, p) in enumerate(units):
        (b_a, b_b), (g_a, g_b) = beta[u], gcol[u]
        e_a, e_b = jnp.exp(g_a), jnp.exp(g_b)
        eg.append((e_a, e_b))
        v_a = act[ln][:, 2 * nq + 2 * p * HC_DV:2 * nq + (2 * p + 1) * HC_DV]
        v_b = act[ln][:, 2 * nq + (2 * p + 1) * HC_DV:2 * nq + (2 * p + 2) * HC_DV]
        rhs = jnp.concatenate([jnp.concatenate([v_a * b_a, kn[u] * (b_a * e_a)], axis=1),
                               jnp.concatenate([v_b * b_b, kn[u] * (b_b * e_b)], axis=1)],
                              axis=0).astype(BF16)
        sol.append((_nn(jnp.where(first, inv[u], 0.0).astype(BF16), rhs),
                    _nn(jnp.where(first, 0.0, inv[u]).astype(BF16), rhs)))
    s_old = [[s_out[0, ln, 2 * p + jj] for jj in range(2)] for ln, p in units]
    ws = []
    for u in nu:
        for jj in range(2):
            lhs = jnp.concatenate([sol[u][jj][:, HC_DV:], qn[u] * eg[u][jj]], axis=0)
            ws.append(_nn(lhs.astype(BF16), s_old[u][jj].astype(BF16)))
    for u, (ln, p) in enumerate(units):
        v_new = [sol[u][jj][:, :HC_DV] - ws[2 * u + jj][:c] for jj in range(2)]
        v_st = jnp.concatenate(v_new, axis=0).astype(BF16)
        attn = qk[u] * decay[u]
        o = (ws[2 * u][c:] + _nn(jnp.where(first, attn, 0.0).astype(BF16), v_st),
             ws[2 * u + 1][c:] + _nn(jnp.where(first, 0.0, attn).astype(BF16), v_st))
        for jj in range(2):
            n = 2 * p + jj
            g_h = gcol[u][jj]
            g_last = g_h[c - 1:c, :]
            k_til = kn[u] * jnp.exp(g_last - g_h)
            s_out[0, ln, n] = jnp.exp(g_last) * s_old[u][jj] + _tn(k_til.astype(BF16), v_new[jj].astype(BF16))
            z = z_ref[rows[ln], n * HC_DV:(n + 1) * HC_DV]
            mrg_ref[rows[ln], n * HC_DV:(n + 1) * HC_DV] = (_rms(o[jj], gnw_ref[...]) * z).astype(BF16)

    @pl.when(emit == 1)
    def _():
        s_p[...] = s_out[...]
        cv_p[0] = xpad[:, base:8, :]

    @pl.when(emit == 2)
    def _():
        s_s[...] = s_out[...]
        cv_s[0] = xpad[:, base:8, :]

    @pl.when(j == 0)
    def _():
        meta_s[...] = s_out[...]
        meta_cv[...] = xpad[:, base:8, :]


def _gdn_mixer(p, tables, n_pp, n_sp, conv_w, alog_row, dtb_row, gnw, s_in, cv_in):
    rb, fs, insl, emit, pidx, sidx = tables
    c = CHUNK
    rows = LANES * c
    n_steps = rb.shape[0]
    qkv_w = 2 * HC_QK_HEADS * HC_DK + HC_V_HEADS * HC_DV
    z_w = HC_V_HEADS * HC_DV
    n_sq = int(np.log2(c)) - 1
    nk = CONV_W - 1
    tri = np.pad(np.tril(np.ones((c, c), np.float32)), ((0, 0), (0, LANE - c)))
    misc = jnp.concatenate([alog_row, dtb_row, gnw, jnp.zeros((5, LANE), F32), jnp.asarray(tri)], axis=0)
    sh_s = (1, LANES, HC_V_HEADS, HC_DK, HC_DV)
    sh_cv = (1, LANES, nk, qkv_w)
    cst = lambda shape: pl.BlockSpec(shape, lambda j, *_: (0,) * len(shape))
    s_spec = lambda k: pl.BlockSpec(sh_s, lambda j, *t: (t[k][j], 0, 0, 0, 0))
    cv_spec = lambda k: pl.BlockSpec(sh_cv, lambda j, *t: (t[k][j], 0, 0, 0))
    grid_spec = pltpu.PrefetchScalarGridSpec(
        num_scalar_prefetch=6,
        grid=(n_steps,),
        in_specs=[pl.BlockSpec((rows, p.shape[1]), lambda j, r, *_: (r[j], 0)),
                  cst((CONV_W, qkv_w)), cst(misc.shape),
                  s_spec(2), cv_spec(2)],
        out_specs=[pl.BlockSpec((rows, z_w), lambda j, r, *_: (r[j], 0)),
                   s_spec(4), cv_spec(4), s_spec(5), cv_spec(5)],
        scratch_shapes=[pltpu.VMEM(sh_s, F32),
                        pltpu.VMEM((LANES, 8 + c, qkv_w), F32),
                        pltpu.VMEM(sh_s, F32),
                        pltpu.VMEM((LANES, nk, qkv_w), F32)],
    )
    st = lambda n: [jax.ShapeDtypeStruct((n, LANES, HC_V_HEADS, HC_DK, HC_DV), F32),
                    jax.ShapeDtypeStruct((n, LANES, nk, qkv_w), F32)]
    return pl.pallas_call(
        functools.partial(_skip_zero_blocks(_gdn_step, 6 + 5), n_sq=n_sq),
        grid_spec=grid_spec,
        out_shape=[jax.ShapeDtypeStruct((p.shape[0], z_w), BF16)] + st(n_pp) + st(n_sp),
        compiler_params=pltpu.CompilerParams(dimension_semantics=("arbitrary",),
                                             vmem_limit_bytes=VMEM_LIMIT),
        name="gdn_mixer",
    )(rb, fs, insl, emit, pidx, sidx, p, conv_w, misc, s_in, cv_in)


def _pad_cols(w, width):
    return jnp.pad(w, ((0, 0), (0, width - w.shape[1])))


def _proj_weight(w, width):
    return jnp.pad(w.T, ((0, width - w.shape[1]), (0, 0)))


def kernel(x_prompt, x_sample, state_hgrn_S, state_mlstm_C, state_mlstm_n, state_mlstm_m, state_gdn_S,
           state_gdn_conv, meta_tokens, norm_mix, norm_ffn, norm_final, even_w_in, even_w_out,
           hgrn_lb_logits, hgrn_norm, mlstm_b_i, mlstm_b_f, mlstm_norm, odd_w_in, odd_conv_w, gdn_a_log,
           gdn_dt_bias, gdn_norm, odd_w_out, ffn_w_in, ffn_w_out):
    n_b, t_len, d = x_prompt.shape
    n_s, l_s, _ = x_sample.shape
    c = CHUNK
    blk = LANES * c
    assert t_len % c == 0 and l_s % c == 0 and meta_tokens.shape[0] == N_META
    assert n_b % LANES == 0 and n_s % LANES == 0, "sequences are processed in pairs"
    assert norm_mix.shape[0] == 2, "one even (HGRN2+mLSTM) and one odd (DeltaNet) layer"
    dt = x_prompt.dtype
    rows_p, rows_s = n_b * t_len, n_s * l_s
    tm = min(DENSE_TM, int(np.gcd(rows_p, rows_s)))
    assert tm % blk == 0
    m_rows = rows_p + rows_s + tm

    meta_job = jnp.concatenate([jnp.zeros((c - N_META, d), dt), meta_tokens.astype(dt)], axis=0)
    x_meta = jnp.concatenate([meta_job, meta_job, jnp.zeros((tm - blk, d), dt)], axis=0)
    x = (x_prompt.reshape(n_b // LANES, LANES, t_len // c, c, d),
         x_sample.reshape(n_s // LANES, LANES, l_s // c, c, d), x_meta)
    tables = _step_tables(n_b // LANES, t_len // c, n_s // LANES, l_s // c, m_rows // blk)
    row = lambda v: v.reshape(1, -1).astype(F32)
    pair = lambda a: a.reshape((a.shape[0] // LANES, LANES) + a.shape[1:])

    e = 0
    n_main = 4 * HA_HEADS * HA_DK + 2 * HB_HEADS * HB_DQK + 2 * HB_HEADS * HB_DV
    w_even = _proj_weight(even_w_in[e].astype(BF16), n_main + LANE)
    hk, hv = HA_HEADS * HA_DK, HA_HEADS * HA_DV
    mq, mv = HB_HEADS * HB_DQK, HB_HEADS * HB_DV
    ident = lambda v: v
    segs_even = ((0, hk, _silu), (hk, 2 * hk, _sigmoid), (2 * hk, 2 * hk + hv, ident),
                 (2 * hk + hv, 2 * hk + 2 * hv, _silu),
                 (2 * hk + 2 * hv, n_main - mv, ident), (n_main - mv, n_main, _sigmoid),
                 (n_main, n_main + LANE, ident))
    p_even = _proj(x, row(norm_mix[0]), w_even, tm, segs_even, "proj_even")
    gbias = _pad_cols(jnp.concatenate([mlstm_b_i[e], mlstm_b_f[e]]).reshape(1, -1).astype(F32), LANE)
    np_, ns_ = n_b // LANES, n_s // LANES
    mrg0, *even_states = _even_mixer(
        p_even, tables, np_, ns_, hgrn_lb_logits.astype(F32), e, row(hgrn_norm[e]), row(mlstm_norm[e]), gbias,
        pair(state_hgrn_S[e].astype(F32)), pair(state_mlstm_C[e].astype(F32)),
        pair(state_mlstm_n[e].astype(F32)[:, :, None, :]), pair(state_mlstm_m[e].astype(F32)[:, None, :]))
    ffn_wi, ffn_w2 = ffn_w_in.astype(BF16), ffn_w_out.astype(BF16)
    x = _post(x, mrg0, (even_w_out.astype(BF16), e), row(norm_ffn[0]), (ffn_wi, 0), (ffn_w2, 0),
              row(norm_final), tm, "post_even")

    o = 0
    n_qkv = 2 * HC_QK_HEADS * HC_DK + HC_V_HEADS * HC_DV
    n_z = HC_V_HEADS * HC_DV
    w_odd = _proj_weight(odd_w_in[o].astype(BF16), n_qkv + n_z + LANE)
    segs_odd = ((0, n_qkv, ident), (n_qkv, n_qkv + n_z, _silu), (n_qkv + n_z, n_qkv + n_z + LANE, ident))
    p_odd = _proj(x, row(norm_mix[1]), w_odd, tm, segs_odd, "proj_odd")
    lane_pad = lambda v: jnp.pad(v.reshape(1, -1).astype(F32),
                                 ((0, 0), (HC_V_HEADS, LANE - 2 * HC_V_HEADS)))
    mrg1, *gdn_states = _gdn_mixer(
        p_odd, tables, np_, ns_, odd_conv_w[o].astype(F32), lane_pad(gdn_a_log[o]),
        lane_pad(gdn_dt_bias[o]), row(gdn_norm[o]), pair(state_gdn_S[o].astype(F32)),
        pair(state_gdn_conv[o].astype(F32)))
    post_odd = functools.partial(_post, x, mrg1, (odd_w_out.astype(BF16), o), row(norm_ffn[1]),
                                 (ffn_wi, 1), (ffn_w2, 1), row(norm_final), tm, final_norm=True)
    y_p = post_odd("post_odd_prompt", tile0=0, n_tiles=rows_p // tm, unpair_shape=(n_b // LANES, t_len))
    y_s = post_odd("post_odd_sample", tile0=rows_p // tm, n_tiles=rows_s // tm,
                   unpair_shape=(n_s // LANES, l_s))

    y_prompt = y_p.reshape(n_b, t_len, d)
    y_sample = y_s.reshape(n_s, l_s, d)
    unpair = lambda v: v.reshape((-1,) + v.shape[2:])[None].astype(dt)
    outs = [y_prompt, y_sample]
    for k in range(2):
        hs_o, mc_o, mn_o, mm_o = even_states[4 * k:4 * k + 4]
        gs_o, cv_o = gdn_states[2 * k:2 * k + 2]
        outs += [unpair(hs_o), unpair(mc_o), unpair(mn_o[:, :, :, 0, :]), unpair(mm_o[:, :, 0, :]),
                 unpair(gs_o), unpair(cv_o)]
    return tuple(outs)
```
